```python
import math
import jax, jax.numpy as jnp
from jax import lax
import numpy as np

D_MODEL = 1024
BATCH = 2
SEQ = 8192
DEPTH = 4
DEC_BATCH = 32
DEC_SEQ = 8
PAST_LEN = 8192
PAGE_SIZE = 128

N_EVEN = (DEPTH + 1) // 2
N_ODD = DEPTH // 2
D_FF = 2816
RMS_EPS = 1e-6
ROPE_THETA = 10000.0

POOL_WINDOWS = (2, 4, 8, 16)
POOL_GROUPS = 4
POOL_GROUP_DIM = D_MODEL // 16
POOL_WIDTH = POOL_GROUPS * POOL_GROUP_DIM
POOL_HIST = max(POOL_WINDOWS) - 1
DA_HEADS = 6
DA_HEAD_DIM = 64
DA_V_DIM = 2 * DA_HEAD_DIM
DA_QK_WIDTH = DA_HEADS * 2 * DA_HEAD_DIM
DA_WIDTH = DA_HEADS * DA_V_DIM
Q_BLOCK = 128
SGU_GROUPS = 4
SGU_GROUP_DIM = 128
SGU_WIDTH = SGU_GROUPS * SGU_GROUP_DIM
SGU_CHUNK = 128
GLA_HEADS = 4
GLA_K_DIM = 64
GLA_V_DIM = 128
GLA_K_WIDTH = GLA_HEADS * GLA_K_DIM
GLA_V_WIDTH = GLA_HEADS * GLA_V_DIM
GLA_GATE_RANK = 16
GLA_GATE_NORMALIZER = 16.0
GLA_CHUNK = 64

EVEN_IN = POOL_WIDTH + 2 * DA_QK_WIDTH + DA_WIDTH
EVEN_OUT = POOL_WIDTH + DA_WIDTH
ODD_IN = 2 * SGU_WIDTH + 2 * GLA_K_WIDTH + 2 * GLA_V_WIDTH + GLA_GATE_RANK
ODD_OUT = SGU_WIDTH + GLA_V_WIDTH

kernel_name = 'hybrid_pool_diffattn_sgu_gla_macaron_step'


def rmsnorm(x, g, eps=RMS_EPS):
    xf = x.astype(jnp.float32)
    y = xf * lax.rsqrt(jnp.mean(xf * xf, axis=-1, keepdims=True) + eps)
    return (y * g.astype(jnp.float32)).astype(x.dtype)


def swiglu(x, w_gate, w_up, w_down):
    return (jax.nn.silu(x @ w_gate) * (x @ w_up)) @ w_down


def rope(x, pos):
    half = x.shape[-1] // 2
    inv = ROPE_THETA ** (-jnp.arange(half, dtype=jnp.float32) / half)
    ang = pos.astype(jnp.float32)[:, None] * inv[None, :]
    cos = jnp.cos(ang)[:, None, None, :]
    sin = jnp.sin(ang)[:, None, None, :]
    xf = x.astype(jnp.float32)
    x1, x2 = xf[..., :half], xf[..., half:]
    return jnp.concatenate([x1 * cos - x2 * sin, x2 * cos + x1 * sin], axis=-1).astype(x.dtype)


def pool_mixer(p, prev, pos, pool_w, pool_scale):
    B, T, _ = p.shape
    ext = jnp.concatenate([prev.astype(p.dtype), p], axis=1).astype(jnp.float32)
    cs = jnp.concatenate([jnp.zeros_like(ext[:, :1]), jnp.cumsum(ext, axis=1)], axis=1)
    end = cs[:, POOL_HIST + 1:]
    means = []
    for g, win in enumerate(POOL_WINDOWS):
        c0, c1 = g * POOL_GROUP_DIM, (g + 1) * POOL_GROUP_DIM
        start = cs[:, POOL_HIST + 1 - win:POOL_HIST + 1 - win + T, c0:c1]
        count = jnp.minimum(pos + 1, win).astype(jnp.float32)[None, :, None]
        means.append((end[:, :, c0:c1] - start) / count)
    diff = (jnp.concatenate(means, axis=-1) - p.astype(jnp.float32)).astype(p.dtype)
    diff = diff.reshape(B, T, POOL_GROUPS, POOL_GROUP_DIM)
    y = jnp.einsum('btgc,gce->btge', diff, pool_w).reshape(B, T, POOL_WIDTH) * pool_scale
    return y, ext[:, -POOL_HIST:].astype(p.dtype)


def diff_attn_core(q, k, v, q_pos, k_pos, lam):
    s = jnp.einsum('bqhmd,bkhmd->bhmqk', q, k, preferred_element_type=jnp.float32) * (DA_HEAD_DIM ** -0.5)
    mask = k_pos[None, :] <= q_pos[:, None]
    p = jax.nn.softmax(jnp.where(mask, s, -jnp.inf), axis=-1)
    a = p[:, :, 0] - lam * p[:, :, 1]
    return jnp.einsum('bhqk,bkhd->bqhd', a.astype(v.dtype), v)


def even_mix(h, pos, pool_prev, k_past, v_past, w_in, w_out, pool_w, pool_scale, lam_vecs, subln_g, lam_init):
    B, T, _ = h.shape
    z = h @ w_in
    p, q, k, v = jnp.split(z, [POOL_WIDTH, POOL_WIDTH + DA_QK_WIDTH, POOL_WIDTH + 2 * DA_QK_WIDTH], axis=-1)
    y_pool, pool_state = pool_mixer(p, pool_prev, pos, pool_w, pool_scale)
    q = rope(q.reshape(B, T, DA_HEADS, 2, DA_HEAD_DIM), pos)
    k = rope(k.reshape(B, T, DA_HEADS, 2, DA_HEAD_DIM), pos)
    v = v.reshape(B, T, DA_HEADS, DA_V_DIM)
    lv = lam_vecs.astype(jnp.float32)
    lam = jnp.exp(jnp.sum(lv[0] * lv[1])) - jnp.exp(jnp.sum(lv[2] * lv[3])) + lam_init
    if k_past is None:
        k_all, v_all = k, v
    else:
        k_all = jnp.concatenate([k_past.astype(k.dtype), k], axis=1)
        v_all = jnp.concatenate([v_past.astype(v.dtype), v], axis=1)
    k_pos = jnp.arange(k_all.shape[1])
    if T > Q_BLOCK and T % Q_BLOCK == 0:
        nb = T // Q_BLOCK
        qb = q.reshape(B, nb, Q_BLOCK, DA_HEADS, 2, DA_HEAD_DIM).swapaxes(0, 1)
        pb = pos.reshape(nb, Q_BLOCK)
        o = lax.map(lambda blk: diff_attn_core(blk[0], k_all, v_all, blk[1], k_pos, lam), (qb, pb))
        o = o.swapaxes(0, 1).reshape(B, T, DA_HEADS, DA_V_DIM)
    else:
        o = diff_attn_core(q, k_all, v_all, pos, k_pos, lam)
    o = rmsnorm(o, subln_g) * (1.0 - lam_init)
    y = jnp.concatenate([y_pool, o.reshape(B, T, DA_WIDTH)], axis=-1) @ w_out
    return y, pool_state, k, v


def spatial_gating(u, v, norm_g, w_s, b_s):
    B, T, _ = u.shape
    C = min(SGU_CHUNK, T)
    n = T // C
    u = jax.nn.gelu(u)
    v = rmsnorm(jax.nn.gelu(v).reshape(B, T, SGU_GROUPS, SGU_GROUP_DIM), norm_g)
    w = jnp.tril(w_s[:, :C, :C])
    vc = v.reshape(B, n, C, SGU_GROUPS, SGU_GROUP_DIM)
    mixed = jnp.einsum('gij,bnjgc->bnigc', w, vc) + b_s[:, :C].T[None, None, :, :, None]
    y = u * mixed.reshape(B, T, SGU_WIDTH)
    last_start = ((T - 1) // SGU_CHUNK) * SGU_CHUNK
    return y, v.reshape(B, T, SGU_WIDTH)[:, last_start:]


def gla_chunked(q, k, v, gk, S0):
    B, T, H, dk = q.shape
    C = min(GLA_CHUNK, T)
    n = T // C

    def to_chunks(a):
        return a.astype(jnp.float32).reshape(B, n, C, H, a.shape[-1]).transpose(1, 0, 3, 2, 4)

    qc, kc, vc, gc = to_chunks(q * (dk ** -0.5)), to_chunks(k), to_chunks(v), to_chunks(gk)
    mask = jnp.tril(jnp.ones((C, C), dtype=bool))[:, :, None]

    def step(S, inp):
        qi, ki, vi, gi = inp
        b = jnp.cumsum(gi, axis=2)
        rel = b[:, :, :, None, :] - b[:, :, None, :, :]
        decay = jnp.exp(jnp.where(mask, rel, -jnp.inf))
        A = jnp.einsum('bhid,bhjd,bhijd->bhij', qi, ki, decay)
        o = A @ vi + jnp.einsum('bhid,bhde->bhie', qi * jnp.exp(b), S)
        b_last = b[:, :, -1:, :]
        S = jnp.exp(b_last[:, :, 0, :, None]) * S + jnp.einsum('bhjd,bhje->bhde', ki * jnp.exp(b_last - b), vi)
        return S, o

    S, o = lax.scan(step, S0, (qc, kc, vc, gc))
    o = o.transpose(1, 0, 3, 2, 4).reshape(B, T, H, v.shape[-1])
    return o, S


def odd_mix(h, S0, w_in, w_out, sgu_norm, sgu_w, sgu_b, gate_w2, gate_b, gla_norm):
    B, T, _ = h.shape
    z = h @ w_in
    cuts = np.cumsum([SGU_WIDTH, SGU_WIDTH, GLA_K_WIDTH, GLA_K_WIDTH, GLA_V_WIDTH, GLA_V_WIDTH]).tolist()
    u, sv, q, k, v, g, g_lr = jnp.split(z, cuts, axis=-1)
    y_c, v_rows = spatial_gating(u, sv, sgu_norm, sgu_w, sgu_b)
    gk = jax.nn.log_sigmoid((g_lr @ gate_w2 + gate_b).astype(jnp.float32)) / GLA_GATE_NORMALIZER
    o, S = gla_chunked(q.reshape(B, T, GLA_HEADS, GLA_K_DIM), k.reshape(B, T, GLA_HEADS, GLA_K_DIM),
                       v.reshape(B, T, GLA_HEADS, GLA_V_DIM), gk.reshape(B, T, GLA_HEADS, GLA_K_DIM), S0)
    o = rmsnorm(o.astype(h.dtype), gla_norm) * jax.nn.silu(g.reshape(B, T, GLA_HEADS, GLA_V_DIM))
    y = jnp.concatenate([y_c, o.reshape(B, T, GLA_V_WIDTH)], axis=-1) @ w_out
    return y, v_rows, S


def run_trunk(x, pos, past, w):
    B, T, _ = x.shape
    k_rows, v_rows, pool_states, sgu_rows, gla_states = [], [], [], [], []
    for l in range(DEPTH):
        h = rmsnorm(x, w['ffn1_norm'][l])
        x = x + 0.5 * swiglu(h, w['ffn1_w_gate'][l], w['ffn1_w_up'][l], w['ffn1_w_down'][l])
        h = rmsnorm(x, w['mix_norm'][l])
        i = l // 2
        if l % 2 == 0:
            if past is None:
                pool_prev = jnp.zeros((B, POOL_HIST, POOL_WIDTH), x.dtype)
                k_past = v_past = None
            else:
                pt = past['page_table']
                n_past = pt.shape[1] * PAGE_SIZE
                pool_prev = past['state_pool'][i]
                k_past = past['cache_k'][pt, i].reshape(B, n_past, DA_HEADS, 2, DA_HEAD_DIM)
                v_past = past['cache_v'][pt, i].reshape(B, n_past, DA_HEADS, DA_V_DIM)
            y, ps, kn, vn = even_mix(h, pos, pool_prev, k_past, v_past, w['even_w_in'][i], w['even_w_out'][i],
                                     w['pool_w'][i], w['pool_scale'][i], w['diff_lambda'][i], w['diff_subln'][i],
                                     0.8 - 0.6 * math.exp(-0.3 * l))
            k_rows.append(kn)
            v_rows.append(vn)
            pool_states.append(ps)
        else:
            if past is None:
                S0 = jnp.zeros((B, GLA_HEADS, GLA_K_DIM, GLA_V_DIM), jnp.float32)
            else:
                S0 = past['state_gla'][i].astype(jnp.float32)
            y, vr, S = odd_mix(h, S0, w['odd_w_in'][i], w['odd_w_out'][i], w['sgu_norm'][i], w['sgu_w'][i],
                               w['sgu_b'][i], w['gla_gate_w2'][i], w['gla_gate_b'][i], w['gla_norm'][i])
            sgu_rows.append(vr)
            gla_states.append(S.astype(x.dtype))
        x = x + y
        h = rmsnorm(x, w['ffn2_norm'][l])
        x = x + 0.5 * swiglu(h, w['ffn2_w_gate'][l], w['ffn2_w_up'][l], w['ffn2_w_down'][l])
    y = rmsnorm(x, w['final_norm'])
    return (y, jnp.stack(k_rows, axis=1), jnp.stack(v_rows, axis=1), jnp.stack(pool_states, axis=0),
            jnp.stack(sgu_rows, axis=0), jnp.stack(gla_states, axis=0))


def setup_inputs(seed: int = 0) -> dict:
    key = jax.random.key(seed)
    keys = iter(jax.random.split(key, 48))

    def nrm(shape, scale):
        return scale * jax.random.normal(next(keys), shape, jnp.float32)

    def gain(shape):
        return 1.0 + nrm(shape, 0.05)

    n_pages = PAST_LEN // PAGE_SIZE
    n_used = DEC_BATCH * n_pages
    n_phys = n_used + max(1, n_used // 4)
    page_table = jax.random.permutation(next(keys), n_phys)[:n_used].reshape(DEC_BATCH, n_pages).astype(jnp.int32)
    D, F = D_MODEL, D_FF
    return {
        'x_prompt': nrm((BATCH, SEQ, D), 1.0),
        'x_sample': nrm((DEC_BATCH, DEC_SEQ, D), 1.0),
        'cache_k': nrm((n_phys, N_EVEN, PAGE_SIZE, DA_HEADS, 2, DA_HEAD_DIM), 1.0),
        'cache_v': nrm((n_phys, N_EVEN, PAGE_SIZE, DA_HEADS, DA_V_DIM), 1.0),
        'page_table': page_table,
        'state_pool': nrm((N_EVEN, DEC_BATCH, POOL_HIST, POOL_WIDTH), 1.0),
        'state_gla': nrm((N_ODD, DEC_BATCH, GLA_HEADS, GLA_K_DIM, GLA_V_DIM), 1.0),
        'ffn1_norm': gain((DEPTH, D)),
        'ffn1_w_gate': nrm((DEPTH, D, F), D ** -0.5),
        'ffn1_w_up': nrm((DEPTH, D, F), D ** -0.5),
        'ffn1_w_down': nrm((DEPTH, F, D), F ** -0.5),
        'mix_norm': gain((DEPTH, D)),
        'ffn2_norm': gain((DEPTH, D)),
        'ffn2_w_gate': nrm((DEPTH, D, F), D ** -0.5),
        'ffn2_w_up': nrm((DEPTH, D, F), D ** -0.5),
        'ffn2_w_down': nrm((DEPTH, F, D), F ** -0.5),
        'even_w_in': nrm((N_EVEN, D, EVEN_IN), D ** -0.5),
        'even_w_out': nrm((N_EVEN, EVEN_OUT, D), EVEN_OUT ** -0.5),
        'pool_w': nrm((N_EVEN, POOL_GROUPS, POOL_GROUP_DIM, POOL_GROUP_DIM), POOL_GROUP_DIM ** -0.5),
        'pool_scale': 1.0 + nrm((N_EVEN, POOL_WIDTH), 0.1),
        'diff_lambda': nrm((N_EVEN, 4, DA_HEAD_DIM), 0.1),
        'diff_subln': gain((N_EVEN, DA_V_DIM)),
        'odd_w_in': nrm((N_ODD, D, ODD_IN), D ** -0.5),
        'odd_w_out': nrm((N_ODD, ODD_OUT, D), ODD_OUT ** -0.5),
        'sgu_norm': gain((N_ODD, SGU_GROUPS, SGU_GROUP_DIM)),
        'sgu_w': nrm((N_ODD, SGU_GROUPS, SGU_CHUNK, SGU_CHUNK), SGU_CHUNK ** -0.5),
        'sgu_b': 1.0 + nrm((N_ODD, SGU_GROUPS, SGU_CHUNK), 0.1),
        'gla_gate_w2': nrm((N_ODD, GLA_GATE_RANK, GLA_K_WIDTH), GLA_GATE_RANK ** -0.5),
        'gla_gate_b': nrm((N_ODD, GLA_K_WIDTH), 0.1),
        'gla_norm': gain((N_ODD, GLA_V_DIM)),
        'final_norm': gain((D,)),
    }


def reference(x_prompt, x_sample, cache_k, cache_v, page_table, state_pool, state_gla,
              ffn1_norm, ffn1_w_gate, ffn1_w_up, ffn1_w_down, mix_norm,
              ffn2_norm, ffn2_w_gate, ffn2_w_up, ffn2_w_down,
              even_w_in, even_w_out, pool_w, pool_scale, diff_lambda, diff_subln,
              odd_w_in, odd_w_out, sgu_norm, sgu_w, sgu_b, gla_gate_w2, gla_gate_b, gla_norm,
              final_norm):
    w = dict(ffn1_norm=ffn1_norm, ffn1_w_gate=ffn1_w_gate, ffn1_w_up=ffn1_w_up, ffn1_w_down=ffn1_w_down,
             mix_norm=mix_norm, ffn2_norm=ffn2_norm, ffn2_w_gate=ffn2_w_gate, ffn2_w_up=ffn2_w_up,
             ffn2_w_down=ffn2_w_down, even_w_in=even_w_in, even_w_out=even_w_out, pool_w=pool_w,
             pool_scale=pool_scale, diff_lambda=diff_lambda, diff_subln=diff_subln, odd_w_in=odd_w_in,
             odd_w_out=odd_w_out, sgu_norm=sgu_norm, sgu_w=sgu_w, sgu_b=sgu_b, gla_gate_w2=gla_gate_w2,
             gla_gate_b=gla_gate_b, gla_norm=gla_norm, final_norm=final_norm)
    pos_prompt = jnp.arange(x_prompt.shape[1])
    y_prompt, k_p, v_p, pool_p, sgu_p, gla_p = run_trunk(x_prompt, pos_prompt, None, w)
    past = dict(cache_k=cache_k, cache_v=cache_v, page_table=page_table, state_pool=state_pool, state_gla=state_gla)
    pos_sample = page_table.shape[1] * PAGE_SIZE + jnp.arange(x_sample.shape[1])
    y_sample, k_s, v_s, pool_s, sgu_s, gla_s = run_trunk(x_sample, pos_sample, past, w)
    return (y_prompt, y_sample, k_p, v_p, pool_p, sgu_p, gla_p, k_s, v_s, pool_s, sgu_s, gla_s)
```

```python
import functools
import math

import numpy as np
import jax
import jax.numpy as jnp
from jax import lax
from jax.experimental import pallas as pl
from jax.experimental.pallas import tpu as pltpu

F32 = jnp.float32
BF = jnp.bfloat16

RMS_EPS = 1e-6
ROPE_THETA = 10000.0
PAGE_SIZE = 128

POOL_WINDOWS = (2, 4, 8, 16)
POOL_GROUP_DIM = 64
POOL_WIDTH = 256
POOL_HIST = 15
POOL_PAD = 16

DA_HEADS = 6
DA_HEAD_DIM = 64
DA_PAIR = 2 * DA_HEAD_DIM
DA_WIDTH = DA_HEADS * DA_PAIR

SGU_GROUPS = 4
SGU_GROUP_DIM = 128
SGU_WIDTH = 512
SGU_CHUNK = 128

GLA_HEADS = 4
GLA_K_DIM = 64
GLA_V_DIM = 128
GLA_K_WIDTH = 256
GLA_V_WIDTH = 512
GLA_GATE_RANK = 16
GLA_GATE_NORMALIZER = 16.0
GLA_CHUNK = 64
GLA_RANK_PAD = 128

LANES = 128
VMEM_LIMIT = 56 * 1024 * 1024


def _cparams(*sem):
    return pltpu.CompilerParams(dimension_semantics=sem, vmem_limit_bytes=VMEM_LIMIT)


def _pick_tile(n, cap, mult=8):
    best = None
    for t in range(mult, min(n, cap) + 1, mult):
        if n % t == 0:
            best = t
    assert best is not None, (n, cap, mult)
    return best


def _shr(x, pow2):
    assert pow2 & (pow2 - 1) == 0
    return x >> (pow2.bit_length() - 1)


def _block_diag(w):
    g, a, b = w.shape
    out = jnp.zeros((g * a, g * b), w.dtype)
    for i in range(g):
        out = out.at[i * a:(i + 1) * a, i * b:(i + 1) * b].set(w[i])
    return out


def _rms(x, axis):
    return x * lax.rsqrt(jnp.mean(x * x, axis=axis, keepdims=True) + RMS_EPS)


def _ffn_kernel(x_ref, g_ref, wg_ref, wu_ref, wd_ref, o_ref, hn_ref, acc_ref, *, nf):
    f = pl.program_id(1)

    @pl.when(f == 0)
    def _():
        hn_ref[...] = (_rms(x_ref[...], -1) * g_ref[...]).astype(BF)
        acc_ref[...] = jnp.zeros_like(acc_ref)

    hn = hn_ref[...]
    a = jnp.dot(hn, wg_ref[...], preferred_element_type=F32)
    u = jnp.dot(hn, wu_ref[...], preferred_element_type=F32)
    h = (a * jax.nn.sigmoid(a) * u).astype(BF)
    acc_ref[...] += jnp.dot(h, wd_ref[...], preferred_element_type=F32)

    @pl.when(f == nf - 1)
    def _():
        o_ref[...] = x_ref[...] + 0.5 * acc_ref[...]


def _ffn(x, g, wg, wu, wd):
    n, d = x.shape
    ff = wg.shape[1]
    tm = _pick_tile(n, 1280)
    tf = _pick_tile(ff, 256, LANES)
    nf = ff // tf
    return pl.pallas_call(
        functools.partial(_ffn_kernel, nf=nf),
        grid=(n // tm, nf),
        in_specs=[
            pl.BlockSpec((tm, d), lambda m, f: (m, 0)),
            pl.BlockSpec((1, d), lambda m, f: (0, 0)),
            pl.BlockSpec((d, tf), lambda m, f: (0, f)),
            pl.BlockSpec((d, tf), lambda m, f: (0, f)),
            pl.BlockSpec((tf, d), lambda m, f: (f, 0)),
        ],
        out_specs=pl.BlockSpec((tm, d), lambda m, f: (m, 0)),
        out_shape=jax.ShapeDtypeStruct((n, d), F32),
        scratch_shapes=[pltpu.VMEM((tm, d), BF), pltpu.VMEM((tm, d), F32)],
        compiler_params=_cparams("parallel", "arbitrary"),
        name="ffn",
    )(x, g.reshape(1, d), wg, wu, wd)


def _inproj_kernel(x_ref, g_ref, w_ref, o_ref, hn_ref):
    @pl.when(pl.program_id(1) == 0)
    def _():
        hn_ref[...] = (_rms(x_ref[...], -1) * g_ref[...]).astype(BF)

    o_ref[...] = jnp.dot(hn_ref[...], w_ref[...], preferred_element_type=F32)


def _inproj(x, g, w):
    n, d = x.shape
    nout = w.shape[1]
    tm = _pick_tile(n, 1280)
    tn = _pick_tile(nout, 512, LANES)
    return pl.pallas_call(
        _inproj_kernel,
        grid=(n // tm, nout // tn),
        in_specs=[
            pl.BlockSpec((tm, d), lambda m, j: (m, 0)),
            pl.BlockSpec((1, d), lambda m, j: (0, 0)),
            pl.BlockSpec((d, tn), lambda m, j: (0, j)),
        ],
        out_specs=pl.BlockSpec((tm, tn), lambda m, j: (m, j)),
        out_shape=jax.ShapeDtypeStruct((n, nout), F32),
        scratch_shapes=[pltpu.VMEM((tm, d), BF)],
        compiler_params=_cparams("parallel", "arbitrary"),
        name="inproj",
    )(x, g.reshape(1, d), w)


def _outproj_kernel(x_ref, a1_ref, a2_ref, w1_ref, w2_ref, o_ref):
    y = jnp.dot(a1_ref[...], w1_ref[...], preferred_element_type=F32)
    y += jnp.dot(a2_ref[...], w2_ref[...], preferred_element_type=F32)
    o_ref[...] = x_ref[...] + y


def _outproj(x, a1, a2, w1, w2):
    n, d = x.shape
    k1, k2 = a1.shape[1], a2.shape[1]
    tm = _pick_tile(n, 1280)
    return pl.pallas_call(
        _outproj_kernel,
        grid=(n // tm,),
        in_specs=[
            pl.BlockSpec((tm, d), lambda m: (m, 0)),
            pl.BlockSpec((tm, k1), lambda m: (m, 0)),
            pl.BlockSpec((tm, k2), lambda m: (m, 0)),
            pl.BlockSpec((k1, d), lambda m: (0, 0)),
            pl.BlockSpec((k2, d), lambda m: (0, 0)),
        ],
        out_specs=pl.BlockSpec((tm, d), lambda m: (m, 0)),
        out_shape=jax.ShapeDtypeStruct((n, d), F32),
        compiler_params=_cparams("parallel"),
        name="outproj",
    )(x, a1, a2, w1, w2)


def _final_norm_kernel(x_ref, g_ref, o_ref):
    o_ref[...] = _rms(x_ref[...], -1) * g_ref[...]


def _final_norm(x, g):
    n, d = x.shape
    tm = _pick_tile(n, 1280)
    return pl.pallas_call(
        _final_norm_kernel,
        grid=(n // tm,),
        in_specs=[pl.BlockSpec((tm, d), lambda m: (m, 0)), pl.BlockSpec((1, d), lambda m: (0, 0))],
        out_specs=pl.BlockSpec((tm, d), lambda m: (m, 0)),
        out_shape=jax.ShapeDtypeStruct((n, d), F32),
        compiler_params=_cparams("parallel"),
        name="final_norm",
    )(x, g.reshape(1, d))


def _pool_kernel(pc_ref, pp_ref, w_ref, sc_ref, o_ref, *, tm, pos0, prev_at_first):
    i = pl.program_id(1)
    p = pc_ref[...]
    prev = pp_ref[...]
    if not prev_at_first:
        prev = jnp.where(i > 0, prev, 0.0)
    ext = jnp.concatenate([prev, p], axis=0)
    s2 = ext + pltpu.roll(ext, 1, 0)
    s4 = s2 + pltpu.roll(s2, 2, 0)
    s8 = s4 + pltpu.roll(s4, 4, 0)
    s16 = s8 + pltpu.roll(s8, 8, 0)
    grp = _shr(lax.broadcasted_iota(jnp.int32, (tm, POOL_WIDTH), 1), POOL_GROUP_DIM)
    pos = pos0 + i * tm + lax.broadcasted_iota(jnp.int32, (tm, POOL_WIDTH), 0)
    sums = (s2, s4, s8, s16)
    s = sums[3][POOL_PAD:]
    win = jnp.full((tm, POOL_WIDTH), POOL_WINDOWS[3], jnp.int32)
    for gi in (2, 1, 0):
        s = jnp.where(grp == gi, sums[gi][POOL_PAD:], s)
        win = jnp.where(grp == gi, POOL_WINDOWS[gi], win)
    cnt = jnp.minimum(pos + 1, win).astype(F32)
    diff = s / cnt - p
    y = jnp.dot(diff.astype(BF), w_ref[...], preferred_element_type=F32) * sc_ref[...]
    o_ref[...] = y.astype(BF)


def _pool(z, prev_src, w_bd, scale, *, nb, t, row0, col_blk, prev_is_state, pos0):
    tm = _pick_tile(t, 512)
    nt = t // tm
    rb0 = row0 // tm
    if prev_is_state:
        assert nt == 1
        prev_spec = pl.BlockSpec((POOL_PAD, POOL_WIDTH), lambda b, i: (b, 0))
    else:
        r16 = tm // POOL_PAD
        base16 = row0 // POOL_PAD
        prev_spec = pl.BlockSpec(
            (POOL_PAD, POOL_WIDTH),
            lambda b, i: (jnp.maximum(base16 + (b * nt + i) * r16 - 1, 0), col_blk))
    return pl.pallas_call(
        functools.partial(_pool_kernel, tm=tm, pos0=pos0, prev_at_first=prev_is_state),
        grid=(nb, nt),
        in_specs=[
            pl.BlockSpec((tm, POOL_WIDTH), lambda b, i: (rb0 + b * nt + i, col_blk)),
            prev_spec,
            pl.BlockSpec((POOL_WIDTH, POOL_WIDTH), lambda b, i: (0, 0)),
            pl.BlockSpec((1, POOL_WIDTH), lambda b, i: (0, 0)),
        ],
        out_specs=pl.BlockSpec((tm, POOL_WIDTH), lambda b, i: (b * nt + i, 0)),
        out_shape=jax.ShapeDtypeStruct((nb * t, POOL_WIDTH), BF),
        compiler_params=_cparams("parallel", "arbitrary"),
        name="pool",
    )(z, prev_src, w_bd, scale.reshape(1, POOL_WIDTH))


def _rope_tables(pos):
    half = DA_HEAD_DIM // 2
    inv = ROPE_THETA ** (-jnp.arange(half, dtype=F32) / half)
    ang = pos.astype(F32)[:, None] * inv[None, :]
    cos, sin = jnp.cos(ang), jnp.sin(ang)
    cos128 = jnp.concatenate([cos, cos, cos, cos], axis=1)
    sin128 = jnp.concatenate([-sin, sin, -sin, sin], axis=1)
    return cos128, sin128


def _rope128(x, cos, sin):
    lane = lax.broadcasted_iota(jnp.int32, x.shape, 1)
    first = (lane & (DA_HEAD_DIM - 1)) < (DA_HEAD_DIM // 2)
    partner = jnp.where(first, pltpu.roll(x, LANES - 32, 1), pltpu.roll(x, 32, 1))
    return x * cos + partner * sin


def _prep_prompt_kernel(zq_ref, zk_ref, zv_ref, cos_ref, sin_ref, qt_ref, kb_ref, vt_ref, kr_ref):
    cos, sin = cos_ref[...], sin_ref[...]
    scale = DA_HEAD_DIM ** -0.5
    for h in range(DA_HEADS):
        sl = slice(h * DA_PAIR, (h + 1) * DA_PAIR)
        q = _rope128(zq_ref[:, sl], cos, sin) * scale
        qt_ref[h] = q.T.astype(BF)
        k = _rope128(zk_ref[:, sl], cos, sin)
        kr_ref[:, sl] = k
        kb_ref[h] = k.astype(BF)
        vt_ref[h] = zv_ref[:, sl].T.astype(BF)


def _prep_prompt(z, cos, sin, *, nb, t, tk):
    nt = t // tk
    return pl.pallas_call(
        _prep_prompt_kernel,
        grid=(nb, nt),
        in_specs=[
            pl.BlockSpec((tk, DA_WIDTH), lambda b, i: (b * nt + i, 0)),
            pl.BlockSpec((tk, DA_WIDTH), lambda b, i: (b * nt + i, 1)),
            pl.BlockSpec((tk, DA_WIDTH), lambda b, i: (b * nt + i, 2)),
            pl.BlockSpec((tk, LANES), lambda b, i: (i, 0)),
            pl.BlockSpec((tk, LANES), lambda b, i: (i, 0)),
        ],
        out_specs=[
            pl.BlockSpec((None, DA_HEADS, DA_PAIR, tk), lambda b, i: (b, 0, 0, i)),
            pl.BlockSpec((None, DA_HEADS, tk, DA_PAIR), lambda b, i: (b, 0, i, 0)),
            pl.BlockSpec((None, DA_HEADS, None, DA_PAIR, tk), lambda b, i: (b, 0, i, 0, 0)),
            pl.BlockSpec((None, tk, DA_WIDTH), lambda b, i: (b, i, 0)),
        ],
        out_shape=[
            jax.ShapeDtypeStruct((nb, DA_HEADS, DA_PAIR, t), BF),
            jax.ShapeDtypeStruct((nb, DA_HEADS, t, DA_PAIR), BF),
            jax.ShapeDtypeStruct((nb, DA_HEADS, nt, DA_PAIR, tk), BF),
            jax.ShapeDtypeStruct((nb, t, DA_WIDTH), F32),
        ],
        compiler_params=_cparams("parallel", "parallel"),
        name="prep_prompt",
    )(z, z, z, cos, sin)


def _lambda_value(lamv_ref, lam_init):
    lv = lamv_ref[...]
    s01 = jnp.sum(lv[0:1] * lv[1:2], axis=1, keepdims=True)
    s23 = jnp.sum(lv[2:3] * lv[3:4], axis=1, keepdims=True)
    return jnp.exp(s01) - jnp.exp(s23) + lam_init


def _attn_prompt_kernel(qt_ref, k_ref, vt_ref, lamv_ref, g_ref, o_ref, *, tq, lam_init):
    qi = pl.program_id(2)
    qt = qt_ref[...]
    row = lax.broadcasted_iota(jnp.int32, qt.shape, 0)
    zero = jnp.zeros_like(qt)
    qx = jnp.concatenate([jnp.where(row < DA_HEAD_DIM, qt, zero),
                          jnp.where(row >= DA_HEAD_DIM, qt, zero)], axis=1)

    def step(j, carry, masked):
        m, l, acc = carry
        kj = k_ref[pl.ds(pl.multiple_of(j * tq, tq), tq), :]
        s = jnp.dot(kj, qx, preferred_element_type=F32)
        if masked:
            kpos = lax.broadcasted_iota(jnp.int32, s.shape, 0)
            qpos = lax.broadcasted_iota(jnp.int32, s.shape, 1) & (tq - 1)
            s = jnp.where(kpos <= qpos, s, -jnp.inf)
        m_new = jnp.maximum(m, jnp.max(s, axis=0, keepdims=True))
        alpha = jnp.exp(m - m_new)
        p = jnp.exp(s - m_new)
        l = alpha * l + jnp.sum(p, axis=0, keepdims=True)
        acc = alpha * acc + jnp.dot(vt_ref[j], p.astype(BF), preferred_element_type=F32)
        return m_new, l, acc

    init = (jnp.full((1, 2 * tq), -jnp.inf, F32), jnp.zeros((1, 2 * tq), F32),
            jnp.zeros((DA_PAIR, 2 * tq), F32))
    carry = lax.fori_loop(0, qi, lambda j, c: step(j, c, False), init)
    _, l, acc = step(qi, carry, True)
    on = acc / l
    lam = _lambda_value(lamv_ref, lam_init)
    ot = on[:, :tq] - lam * on[:, tq:]
    ot = _rms(ot, 0) * g_ref[...] * (1.0 - lam_init)
    o_ref[...] = ot.T.astype(BF)


def _attn_prompt(qt, kb, vt, lamv, subln_g, *, tq, lam_init):
    nb, nh, _, t = qt.shape
    nk = t // tq
    return pl.pallas_call(
        functools.partial(_attn_prompt_kernel, tq=tq, lam_init=lam_init),
        grid=(nb, nh, t // tq),
        in_specs=[
            pl.BlockSpec((None, None, DA_PAIR, tq), lambda b, h, i: (b, h, 0, i)),
            pl.BlockSpec((None, None, t, DA_PAIR), lambda b, h, i: (b, h, 0, 0)),
            pl.BlockSpec((None, None, nk, DA_PAIR, tq), lambda b, h, i: (b, h, 0, 0, 0)),
            pl.BlockSpec((4, DA_HEAD_DIM), lambda b, h, i: (0, 0)),
            pl.BlockSpec((DA_PAIR, 1), lambda b, h, i: (0, 0)),
        ],
        out_specs=pl.BlockSpec((None, tq, DA_PAIR), lambda b, h, i: (b, i, h)),
        out_shape=jax.ShapeDtypeStruct((nb, t, DA_WIDTH), BF),
        compiler_params=_cparams("parallel", "parallel", "arbitrary"),
        name="attn_prompt",
    )(qt, kb, vt, lamv, subln_g.reshape(DA_PAIR, 1))


def _rope_rows_kernel(zq_ref, zk_ref, cos_ref, sin_ref, q_ref, k_ref):
    cos, sin = cos_ref[...], sin_ref[...]
    scale = DA_HEAD_DIM ** -0.5
    for h in range(DA_HEADS):
        sl = slice(h * DA_PAIR, (h + 1) * DA_PAIR)
        q_ref[:, sl] = (_rope128(zq_ref[:, sl], cos, sin) * scale).astype(BF)
        k_ref[:, sl] = _rope128(zk_ref[:, sl], cos, sin)


def _rope_rows(z, cos, sin, *, row0, nrows):
    rb = row0 // nrows
    return pl.pallas_call(
        _rope_rows_kernel,
        grid=(1,),
        in_specs=[
            pl.BlockSpec((nrows, DA_WIDTH), lambda i: (rb, 0)),
            pl.BlockSpec((nrows, DA_WIDTH), lambda i: (rb, 1)),
            pl.BlockSpec((nrows, LANES), lambda i: (0, 0)),
            pl.BlockSpec((nrows, LANES), lambda i: (0, 0)),
        ],
        out_specs=[pl.BlockSpec((nrows, DA_WIDTH), lambda i: (0, 0)),
                   pl.BlockSpec((nrows, DA_WIDTH), lambda i: (0, 0))],
        out_shape=[jax.ShapeDtypeStruct((nrows, DA_WIDTH), BF),
                   jax.ShapeDtypeStruct((nrows, DA_WIDTH), F32)],
        compiler_params=_cparams("arbitrary"),
        name="rope_rows",
    )(z, z, cos, sin)


def _attn_sample_kernel(pt_ref, q_ref, kn_ref, vn_ref, sel_ref, lamv_ref, g_ref, *rest,
                        pp, nsteps, ts, lam_init):
    kpages, vpages = rest[:pp], rest[pp:2 * pp]
    o_ref, m_ref, l_ref, acc_ref, qbd_ref = rest[2 * pp:]
    s = pl.program_id(1)
    nrow = 2 * DA_HEADS * ts

    @pl.when(s == 0)
    def _():
        q = q_ref[...]
        qrep = jnp.concatenate([q] * (2 * DA_HEADS), axis=0)
        qbd_ref[...] = jnp.where(sel_ref[...] > 0, qrep, jnp.zeros_like(qrep))
        m_ref[...] = jnp.full(m_ref.shape, -jnp.inf, F32)
        l_ref[...] = jnp.zeros_like(l_ref)
        acc_ref[...] = jnp.zeros_like(acc_ref)

    def update(kk, vv, causal):
        sc = lax.dot_general(qbd_ref[...], kk, (((1,), (1,)), ((), ())),
                             preferred_element_type=F32)
        if causal:
            tq = lax.broadcasted_iota(jnp.int32, sc.shape, 0) & (ts - 1)
            tk = lax.broadcasted_iota(jnp.int32, sc.shape, 1)
            sc = jnp.where(tk <= tq, sc, -jnp.inf)
        m_prev = m_ref[...]
        m_new = jnp.maximum(m_prev, jnp.max(sc, axis=1, keepdims=True))
        alpha = jnp.exp(m_prev - m_new)
        p = jnp.exp(sc - m_new)
        l_ref[...] = alpha * l_ref[...] + jnp.sum(p, axis=1, keepdims=True)
        acc_ref[...] = alpha * acc_ref[...] + jnp.dot(p.astype(BF), vv, preferred_element_type=F32)
        m_ref[...] = m_new

    if pp:
        kk = jnp.concatenate([r[...] for r in kpages], axis=0).astype(BF)
        vv = jnp.concatenate([r[...] for r in vpages], axis=0).astype(BF)
        update(kk, vv, False)

    @pl.when(s == nsteps - 1)
    def _():
        update(kn_ref[...].astype(BF), vn_ref[...].astype(BF), True)
        on = acc_ref[...] / l_ref[...]
        lam = _lambda_value(lamv_ref, lam_init)
        half = nrow // 2
        d = on[:half] - lam * on[half:]
        outs = []
        for h in range(DA_HEADS):
            blk = d[h * ts:(h + 1) * ts, h * DA_PAIR:(h + 1) * DA_PAIR]
            outs.append(_rms(blk, -1) * g_ref[...] * (1.0 - lam_init))
        o_ref[...] = jnp.concatenate(outs, axis=1).astype(BF)


def _attn_sample(q_s, k_new, z, cache_k, cache_v, page_table, lamv, subln_g, *,
                 layer, vrow0, lam_init):
    nb, ts, _ = q_s.shape
    n_pages = page_table.shape[1]
    pp = 0
    for cand in (4, 2, 1):
        if n_pages and n_pages % cand == 0:
            pp = cand
            break
    nsteps = max(n_pages // pp, 1) if pp else 1
    nrow = 2 * DA_HEADS * ts
    r = np.arange(nrow)[:, None] // ts
    c = np.arange(DA_WIDTH)[None, :] // DA_HEAD_DIM
    sel = jnp.asarray(((r % DA_HEADS) * 2 + r // DA_HEADS == c).astype(np.float32))
    ck = cache_k.reshape(cache_k.shape[0], cache_k.shape[1], PAGE_SIZE, DA_WIDTH)
    cv = cache_v.reshape(cache_v.shape[0], cache_v.shape[1], PAGE_SIZE, DA_WIDTH)
    vrb = vrow0 // ts

    def page_spec(i):
        return pl.BlockSpec((None, None, PAGE_SIZE, DA_WIDTH),
                            lambda b, s, pt: (pt[b * n_pages + s * pp + i], layer, 0, 0))

    in_specs = [
        pl.BlockSpec((None, ts, DA_WIDTH), lambda b, s, pt: (b, 0, 0)),
        pl.BlockSpec((None, ts, DA_WIDTH), lambda b, s, pt: (b, 0, 0)),
        pl.BlockSpec((ts, DA_WIDTH), lambda b, s, pt: (vrb + b, 2)),
        pl.BlockSpec((nrow, DA_WIDTH), lambda b, s, pt: (0, 0)),
        pl.BlockSpec((4, DA_HEAD_DIM), lambda b, s, pt: (0, 0)),
        pl.BlockSpec((1, DA_PAIR), lambda b, s, pt: (0, 0)),
    ] + [page_spec(i) for i in range(pp)] * 2
    grid_spec = pltpu.PrefetchScalarGridSpec(
        num_scalar_prefetch=1,
        grid=(nb, nsteps),
        in_specs=in_specs,
        out_specs=pl.BlockSpec((None, ts, DA_WIDTH), lambda b, s, pt: (b, 0, 0)),
        scratch_shapes=[pltpu.VMEM((nrow, 1), F32), pltpu.VMEM((nrow, 1), F32),
                        pltpu.VMEM((nrow, DA_WIDTH), F32), pltpu.VMEM((nrow, DA_WIDTH), BF)],
    )
    return pl.pallas_call(
        functools.partial(_attn_sample_kernel, pp=pp, nsteps=nsteps, ts=ts, lam_init=lam_init),
        grid_spec=grid_spec,
        out_shape=jax.ShapeDtypeStruct((nb, ts, DA_WIDTH), BF),
        compiler_params=_cparams("parallel", "arbitrary"),
        name="attn_sample",
    )(page_table.reshape(-1), q_s, k_new, z, sel, lamv, subln_g.reshape(1, DA_PAIR),
      *([ck] * pp), *([cv] * pp))


def _sgu_norm_v(sv_ref, ng_ref, g):
    x = jax.nn.gelu(sv_ref[:, g * SGU_GROUP_DIM:(g + 1) * SGU_GROUP_DIM])
    return _rms(x, -1) * ng_ref[g:g + 1, :]


def _sgu_prompt_kernel(u_ref, sv_ref, ng_ref, w_ref, bs_ref, y_ref, vr_ref, *, tm, c):
    last = pl.program_id(1) == pl.num_programs(1) - 1
    ri = lax.broadcasted_iota(jnp.int32, (c, c), 0)
    ci = lax.broadcasted_iota(jnp.int32, (c, c), 1)
    for g in range(SGU_GROUPS):
        sl = slice(g * SGU_GROUP_DIM, (g + 1) * SGU_GROUP_DIM)
        v = _sgu_norm_v(sv_ref, ng_ref, g)

        @pl.when(last)
        def _():
            vr_ref[:, sl] = v[tm - c:]

        vb = v.astype(BF)
        w = jnp.where(ri >= ci, w_ref[g], 0.0).astype(BF)
        for n in range(tm // c):
            rows = slice(n * c, (n + 1) * c)
            mixed = jnp.dot(w, vb[rows], preferred_element_type=F32) + bs_ref[g]
            y_ref[rows, sl] = (jax.nn.gelu(u_ref[rows, sl]) * mixed).astype(BF)


def _sgu_prompt(z, norm_g, w_s, b_s, *, nb, t):
    c = SGU_CHUNK
    tm = _pick_tile(t, 512, c)
    nt = t // tm
    bs = jnp.broadcast_to(b_s[:, :c, None], (SGU_GROUPS, c, SGU_GROUP_DIM))
    return pl.pallas_call(
        functools.partial(_sgu_prompt_kernel, tm=tm, c=c),
        grid=(nb, nt),
        in_specs=[
            pl.BlockSpec((tm, SGU_WIDTH), lambda b, i: (b * nt + i, 0)),
            pl.BlockSpec((tm, SGU_WIDTH), lambda b, i: (b * nt + i, 1)),
            pl.BlockSpec((SGU_GROUPS, SGU_GROUP_DIM), lambda b, i: (0, 0)),
            pl.BlockSpec((SGU_GROUPS, c, c), lambda b, i: (0, 0, 0)),
            pl.BlockSpec((SGU_GROUPS, c, SGU_GROUP_DIM), lambda b, i: (0, 0, 0)),
        ],
        out_specs=[
            pl.BlockSpec((tm, SGU_WIDTH), lambda b, i: (b * nt + i, 0)),
            pl.BlockSpec((None, c, SGU_WIDTH), lambda b, i: (b, 0, 0)),
        ],
        out_shape=[jax.ShapeDtypeStruct((nb * t, SGU_WIDTH), BF),
                   jax.ShapeDtypeStruct((nb, c, SGU_WIDTH), F32)],
        compiler_params=_cparams("parallel", "arbitrary"),
        name="sgu_prompt",
    )(z, z, norm_g, w_s[:, :c, :c], bs)


def _sgu_sample_kernel(u_ref, sv_ref, ng_ref, w_ref, bs_ref, y_ref, vr_ref, *, nb, ts):
    v = jnp.concatenate([_sgu_norm_v(sv_ref, ng_ref, g) for g in range(SGU_GROUPS)], axis=1)
    vr_ref[...] = v
    v3 = v.reshape(nb, ts, SGU_WIDTH)
    ii = lax.broadcasted_iota(jnp.int32, (ts, SGU_WIDTH), 0)
    mixed = jnp.broadcast_to(bs_ref[...][None], (nb, ts, SGU_WIDTH))
    for j in range(ts):
        wj = jnp.where(ii >= j, w_ref[j], 0.0)
        mixed = mixed + wj[None] * v3[:, j:j + 1, :]
    y = jax.nn.gelu(u_ref[...]).reshape(nb, ts, SGU_WIDTH) * mixed
    y_ref[...] = y.reshape(nb * ts, SGU_WIDTH).astype(BF)


def _sgu_sample(z, norm_g, w_s, b_s, *, nb, ts, row0):
    nrows = nb * ts
    rb = row0 // nrows
    w_exp = jnp.repeat(jnp.transpose(w_s[:, :ts, :ts], (2, 1, 0)), SGU_GROUP_DIM, axis=2)
    b_exp = jnp.repeat(jnp.transpose(b_s[:, :ts], (1, 0)), SGU_GROUP_DIM, axis=1)
    return pl.pallas_call(
        functools.partial(_sgu_sample_kernel, nb=nb, ts=ts),
        grid=(1,),
        in_specs=[
            pl.BlockSpec((nrows, SGU_WIDTH), lambda i: (rb, 0)),
            pl.BlockSpec((nrows, SGU_WIDTH), lambda i: (rb, 1)),
            pl.BlockSpec((SGU_GROUPS, SGU_GROUP_DIM), lambda i: (0, 0)),
            pl.BlockSpec((ts, ts, SGU_WIDTH), lambda i: (0, 0, 0)),
            pl.BlockSpec((ts, SGU_WIDTH), lambda i: (0, 0)),
        ],
        out_specs=[pl.BlockSpec((nrows, SGU_WIDTH), lambda i: (0, 0)),
                   pl.BlockSpec((nrows, SGU_WIDTH), lambda i: (0, 0))],
        out_shape=[jax.ShapeDtypeStruct((nrows, SGU_WIDTH), BF),
                   jax.ShapeDtypeStruct((nrows, SGU_WIDTH), F32)],
        compiler_params=_cparams("arbitrary"),
        name="sgu_sample",
    )(z, z, norm_g, w_exp, b_exp)


def _log_sigmoid(x):
    return jnp.minimum(x, 0.0) - jnp.log1p(jnp.exp(-jnp.abs(x)))


def _seg_scan(x, seg, pos, reverse):
    n = x.shape[0]
    r = pos & (seg - 1)
    t = 1
    while t < seg:
        if reverse:
            x = x + jnp.where(r < seg - t, pltpu.roll(x, n - t, 0), 0.0)
        else:
            x = x + jnp.where(r >= t, pltpu.roll(x, t, 0), 0.0)
        t *= 2
    return x


def _gla_decays(gk, c):
    pos = lax.broadcasted_iota(jnp.int32, gk.shape, 0)
    out = {1: (jnp.exp(gk), None)}
    s = 2
    while s <= c:
        pre = _seg_scan(gk, s, pos, False)
        suf = _seg_scan(gk, s, pos, True) - gk
        out[s] = (jnp.exp(pre), jnp.exp(suf))
        s *= 2
    return out


def _gla_intra(q, k, v, dec, c, sc):
    n = q.shape[0]
    ri = lax.broadcasted_iota(jnp.int32, (sc, sc), 0)
    ci = lax.broadcasted_iota(jnp.int32, (sc, sc), 1)
    same_chunk = _shr(ri, c) == _shr(ci, c)
    lane = lax.broadcasted_iota(jnp.int32, (sc, LANES), 1)
    levels = []
    levels.append((ri == ci, q.astype(BF), k.astype(BF)))
    s = 1
    while s < c:
        pre, _ = dec[s]
        suf = dec[s][1]
        ks = k if suf is None else k * suf
        mask = same_chunk & (_shr(ri, 2 * s) == _shr(ci, 2 * s)) & ((_shr(ri, s) & 1) == 1) \
            & ((_shr(ci, s) & 1) == 0)
        levels.append((mask, (q * pre).astype(BF), ks.astype(BF)))
        s *= 2
    outs = []
    for h in range(GLA_HEADS):
        pair = slice((h // 2) * LANES, (h // 2 + 1) * LANES)
        own = _shr(lane, GLA_K_DIM) == (h % 2)
        vh = v[:, h * GLA_V_DIM:(h + 1) * GLA_V_DIM].astype(BF)
        rows_out = []
        for b0 in range(0, n, sc):
            rows = slice(b0, b0 + sc)
            a = jnp.zeros((sc, sc), F32)
            for mask, qs, ks in levels:
                qm = jnp.where(own, qs[rows, pair], jnp.zeros((sc, LANES), BF))
                al = lax.dot_general(qm, ks[rows, pair], (((1,), (1,)), ((), ())),
                                     preferred_element_type=F32)
                a = a + jnp.where(mask, al, 0.0)
            rows_out.append(jnp.dot(a.astype(BF), vh[rows], preferred_element_type=F32))
        outs.append(jnp.concatenate(rows_out, axis=0) if len(rows_out) > 1 else rows_out[0])
    return outs


def _gla_gate(glr_ref, w2_ref, gb_ref):
    pre = jnp.dot(glr_ref[...].astype(BF), w2_ref[...], preferred_element_type=F32) + gb_ref[...]
    return _log_sigmoid(pre) / GLA_GATE_NORMALIZER


def _gla_finish(o_heads, g_ref, gn_ref, o_ref):
    for h in range(GLA_HEADS):
        sl = slice(h * GLA_V_DIM, (h + 1) * GLA_V_DIM)
        gate = g_ref[:, sl]
        o_ref[:, sl] = (_rms(o_heads[h], -1) * gn_ref[...] * (gate * jax.nn.sigmoid(gate))).astype(BF)


def _gla_prompt_kernel(q_ref, k_ref, v_ref, g_ref, glr_ref, w2_ref, gb_ref, gn_ref,
                       o_ref, s_out_ref, st_ref, *, tm, c):
    i = pl.program_id(1)

    @pl.when(i == 0)
    def _():
        st_ref[...] = jnp.zeros_like(st_ref)

    gk = _gla_gate(glr_ref, w2_ref, gb_ref)
    q = q_ref[...] * (GLA_K_DIM ** -0.5)
    k = k_ref[...]
    v = v_ref[...]
    dec = _gla_decays(gk, c)
    o_heads = _gla_intra(q, k, v, dec, c, min(tm, LANES))
    eb, esuf = dec[c]
    qd = (q * eb).astype(BF)
    kd = (k * esuf).astype(BF)
    lane = lax.broadcasted_iota(jnp.int32, (c, LANES), 1)
    for h in range(GLA_HEADS):
        pair = slice((h // 2) * LANES, (h // 2 + 1) * LANES)
        own = _shr(lane, GLA_K_DIM) == (h % 2)
        vh = v[:, h * GLA_V_DIM:(h + 1) * GLA_V_DIM].astype(BF)
        st = st_ref[h]
        inter = []
        for n in range(tm // c):
            rows = slice(n * c, (n + 1) * c)
            qm = jnp.where(own, qd[rows, pair], jnp.zeros((c, LANES), BF))
            inter.append(lax.dot_general(qm, st.astype(BF), (((1,), (1,)), ((), ())),
                                         preferred_element_type=F32))
            upd = lax.dot_general(vh[rows], kd[rows, pair], (((0,), (0,)), ((), ())),
                                  preferred_element_type=F32)
            st = eb[n * c + c - 1:n * c + c, pair] * st + upd
        st_ref[h] = st
        o_heads[h] = o_heads[h] + jnp.concatenate(inter, axis=0)
    _gla_finish(o_heads, g_ref, gn_ref, o_ref)

    @pl.when(i == pl.num_programs(1) - 1)
    def _():
        for h in range(GLA_HEADS):
            lo = (h % 2) * GLA_K_DIM
            s_out_ref[h] = st_ref[h].T[lo:lo + GLA_K_DIM, :]


def _gla_prompt(z, w2, gate_b, gla_norm, *, nb, t):
    c = GLA_CHUNK
    tm = _pick_tile(t, 256, LANES)
    nt = t // tm
    return pl.pallas_call(
        functools.partial(_gla_prompt_kernel, tm=tm, c=c),
        grid=(nb, nt),
        in_specs=[
            pl.BlockSpec((tm, GLA_K_WIDTH), lambda b, i: (b * nt + i, 4)),
            pl.BlockSpec((tm, GLA_K_WIDTH), lambda b, i: (b * nt + i, 5)),
            pl.BlockSpec((tm, GLA_V_WIDTH), lambda b, i: (b * nt + i, 3)),
            pl.BlockSpec((tm, GLA_V_WIDTH), lambda b, i: (b * nt + i, 4)),
            pl.BlockSpec((tm, GLA_RANK_PAD), lambda b, i: (b * nt + i, 20)),
            pl.BlockSpec((GLA_RANK_PAD, GLA_K_WIDTH), lambda b, i: (0, 0)),
            pl.BlockSpec((1, GLA_K_WIDTH), lambda b, i: (0, 0)),
            pl.BlockSpec((1, GLA_V_DIM), lambda b, i: (0, 0)),
        ],
        out_specs=[
            pl.BlockSpec((tm, GLA_V_WIDTH), lambda b, i: (b * nt + i, 0)),
            pl.BlockSpec((None, GLA_HEADS, GLA_K_DIM, GLA_V_DIM), lambda b, i: (b, 0, 0, 0)),
        ],
        out_shape=[jax.ShapeDtypeStruct((nb * t, GLA_V_WIDTH), BF),
                   jax.ShapeDtypeStruct((nb, GLA_HEADS, GLA_K_DIM, GLA_V_DIM), F32)],
        scratch_shapes=[pltpu.VMEM((GLA_HEADS, GLA_V_DIM, LANES), F32)],
        compiler_params=_cparams("parallel", "arbitrary"),
        name="gla_prompt",
    )(z, z, z, z, z, w2, gate_b.reshape(1, GLA_K_WIDTH), gla_norm.reshape(1, GLA_V_DIM))


def _gla_sample_kernel(q_ref, k_ref, v_ref, g_ref, glr_ref, w2_ref, gb_ref, gbc_ref, gn_ref, s0_ref,
                       o_ref, s_out_ref, *, ts):
    gk = _gla_gate(glr_ref, w2_ref, gb_ref)
    q = q_ref[...] * (GLA_K_DIM ** -0.5)
    k = k_ref[...]
    v = v_ref[...]
    dec = _gla_decays(gk, ts)
    o_heads = _gla_intra(q, k, v, dec, ts, ts)
    eb, esuf = dec[ts]
    qd = (q * eb).astype(BF)
    kd = (k * esuf).astype(BF)
    pre_t = lax.dot_general(w2_ref[...], glr_ref[...].astype(BF), (((0,), (1,)), ((), ())),
                            preferred_element_type=F32) + gbc_ref[...]
    dlast = jnp.exp(jnp.sum(_log_sigmoid(pre_t) / GLA_GATE_NORMALIZER, axis=1, keepdims=True))
    lane = lax.broadcasted_iota(jnp.int32, (ts, LANES), 1)
    for h in range(GLA_HEADS):
        pair = slice((h // 2) * LANES, (h // 2 + 1) * LANES)
        own = _shr(lane, GLA_K_DIM) == (h % 2)
        vh = v[:, h * GLA_V_DIM:(h + 1) * GLA_V_DIM].astype(BF)
        s_pair = s0_ref[pair, :]
        qm = jnp.where(own, qd[:, pair], jnp.zeros((ts, LANES), BF))
        o_heads[h] = o_heads[h] + jnp.dot(qm, s_pair.astype(BF), preferred_element_type=F32)
        upd = lax.dot_general(kd[:, pair], vh, (((0,), (0,)), ((), ())),
                              preferred_element_type=F32)
        hr = slice(h * GLA_K_DIM, (h + 1) * GLA_K_DIM)
        lo = (h % 2) * GLA_K_DIM
        s_out_ref[hr, :] = dlast[hr] * s0_ref[hr, :] + upd[lo:lo + GLA_K_DIM]
    _gla_finish(o_heads, g_ref, gn_ref, o_ref)


def _gla_sample(z, w2, gate_b, gla_norm, s0, *, nb, ts, row0):
    rb = row0 // ts
    s0r = s0.reshape(nb, GLA_HEADS * GLA_K_DIM, GLA_V_DIM)
    o, s_new = pl.pallas_call(
        functools.partial(_gla_sample_kernel, ts=ts),
        grid=(nb,),
        in_specs=[
            pl.BlockSpec((ts, GLA_K_WIDTH), lambda b: (rb + b, 4)),
            pl.BlockSpec((ts, GLA_K_WIDTH), lambda b: (rb + b, 5)),
            pl.BlockSpec((ts, GLA_V_WIDTH), lambda b: (rb + b, 3)),
            pl.BlockSpec((ts, GLA_V_WIDTH), lambda b: (rb + b, 4)),
            pl.BlockSpec((ts, GLA_RANK_PAD), lambda b: (rb + b, 20)),
            pl.BlockSpec((GLA_RANK_PAD, GLA_K_WIDTH), lambda b: (0, 0)),
            pl.BlockSpec((1, GLA_K_WIDTH), lambda b: (0, 0)),
            pl.BlockSpec((GLA_K_WIDTH, 1), lambda b: (0, 0)),
            pl.BlockSpec((1, GLA_V_DIM), lambda b: (0, 0)),
            pl.BlockSpec((None, GLA_HEADS * GLA_K_DIM, GLA_V_DIM), lambda b: (b, 0, 0)),
        ],
        out_specs=[
            pl.BlockSpec((ts, GLA_V_WIDTH), lambda b: (b, 0)),
            pl.BlockSpec((None, GLA_HEADS * GLA_K_DIM, GLA_V_DIM), lambda b: (b, 0, 0)),
        ],
        out_shape=[jax.ShapeDtypeStruct((nb * ts, GLA_V_WIDTH), BF),
                   jax.ShapeDtypeStruct((nb, GLA_HEADS * GLA_K_DIM, GLA_V_DIM), F32)],
        compiler_params=_cparams("parallel"),
        name="gla_sample",
    )(z, z, z, z, z, w2, gate_b.reshape(1, GLA_K_WIDTH), gate_b.reshape(GLA_K_WIDTH, 1),
      gla_norm.reshape(1, GLA_V_DIM), s0r)
    return o, s_new.reshape(nb, GLA_HEADS, GLA_K_DIM, GLA_V_DIM)


def kernel(x_prompt, x_sample, cache_k, cache_v, page_table, state_pool, state_gla, ffn1_norm, ffn1_w_gate, ffn1_w_up, ffn1_w_down, mix_norm, ffn2_norm, ffn2_w_gate, ffn2_w_up, ffn2_w_down, even_w_in, even_w_out, pool_w, pool_scale, diff_lambda, diff_subln, odd_w_in, odd_w_out, sgu_norm, sgu_w, sgu_b, gla_gate_w2, gla_gate_b, gla_norm, final_norm):
    nb, t, d = x_prompt.shape
    nbs, ts, _ = x_sample.shape
    depth = ffn1_norm.shape[0]
    n_p, n_s = nb * t, nbs * ts
    past_len = page_table.shape[1] * PAGE_SIZE
    assert ts < min(SGU_CHUNK, GLA_CHUNK) and ts & (ts - 1) == 0 and n_p % n_s == 0
    assert t % SGU_CHUNK == 0 and t >= POOL_PAD

    x = jnp.concatenate([x_prompt.reshape(n_p, d), x_sample.reshape(n_s, d)], axis=0)
    tq = _pick_tile(t, 512, LANES)
    cos_p, sin_p = _rope_tables(jnp.arange(t))
    cos_s, sin_s = _rope_tables(past_len + jnp.arange(ts))
    cos_s, sin_s = jnp.tile(cos_s, (nbs, 1)), jnp.tile(sin_s, (nbs, 1))

    k_p, v_p, pool_p, sgu_p, gla_p = [], [], [], [], []
    k_s, v_s, pool_s, sgu_s, gla_s = [], [], [], [], []
    for l in range(depth):
        i = l // 2
        x = _ffn(x, ffn1_norm[l], ffn1_w_gate[l].astype(BF), ffn1_w_up[l].astype(BF),
                 ffn1_w_down[l].astype(BF))
        if l % 2 == 0:
            lam_init = 0.8 - 0.6 * math.exp(-0.3 * l)
            w_in = even_w_in[i]
            w_in = jnp.concatenate([w_in[:, POOL_WIDTH:], w_in[:, :POOL_WIDTH]], axis=1).astype(BF)
            z = _inproj(x, mix_norm[l], w_in)
            pcol = 3 * DA_WIDTH // POOL_WIDTH
            w_bd = _block_diag(pool_w[i]).astype(BF)
            y_pool_p = _pool(z, z, w_bd, pool_scale[i], nb=nb, t=t, row0=0, col_blk=pcol,
                             prev_is_state=False, pos0=0)
            st_pad = jnp.pad(state_pool[i], ((0, 0), (POOL_PAD - POOL_HIST, 0), (0, 0)))
            y_pool_s = _pool(z, st_pad.reshape(nbs * POOL_PAD, POOL_WIDTH), w_bd, pool_scale[i],
                             nb=nbs, t=ts, row0=n_p, col_blk=pcol, prev_is_state=True, pos0=past_len)
            qt, kb, vt, k_rows = _prep_prompt(z, cos_p, sin_p, nb=nb, t=t, tk=tq)
            o_p = _attn_prompt(qt, kb, vt, diff_lambda[i], diff_subln[i], tq=tq, lam_init=lam_init)
            q_s, k_new = _rope_rows(z, cos_s, sin_s, row0=n_p, nrows=n_s)
            k_new = k_new.reshape(nbs, ts, DA_WIDTH)
            o_s = _attn_sample(q_s.reshape(nbs, ts, DA_WIDTH), k_new, z, cache_k, cache_v,
                               page_table, diff_lambda[i], diff_subln[i], layer=i, vrow0=n_p,
                               lam_init=lam_init)
            a1 = jnp.concatenate([y_pool_p, y_pool_s], axis=0)
            a2 = jnp.concatenate([o_p.reshape(n_p, DA_WIDTH), o_s.reshape(n_s, DA_WIDTH)], axis=0)
            w_out = even_w_out[i].astype(BF)
            x = _outproj(x, a1, a2, w_out[:POOL_WIDTH], w_out[POOL_WIDTH:])
            zp = z[:n_p].reshape(nb, t, -1)
            zs = z[n_p:].reshape(nbs, ts, -1)
            k_p.append(k_rows)
            v_p.append(zp[:, :, 2 * DA_WIDTH:3 * DA_WIDTH])
            pool_p.append(zp[:, t - POOL_HIST:, 3 * DA_WIDTH:])
            k_s.append(k_new)
            v_s.append(zs[:, :, 2 * DA_WIDTH:3 * DA_WIDTH])
            pool_s.append(jnp.concatenate([state_pool[i], zs[:, :, 3 * DA_WIDTH:]], axis=1)[:, -POOL_HIST:])
        else:
            w_in = jnp.pad(odd_w_in[i], ((0, 0), (0, GLA_RANK_PAD - GLA_GATE_RANK))).astype(BF)
            z = _inproj(x, mix_norm[l], w_in)
            w2 = jnp.pad(gla_gate_w2[i], ((0, GLA_RANK_PAD - GLA_GATE_RANK), (0, 0))).astype(BF)
            y_c_p, vr_p = _sgu_prompt(z, sgu_norm[i], sgu_w[i], sgu_b[i], nb=nb, t=t)
            y_c_s, vr_s = _sgu_sample(z, sgu_norm[i], sgu_w[i], sgu_b[i], nb=nbs, ts=ts, row0=n_p)
            og_p, s_p = _gla_prompt(z, w2, gla_gate_b[i], gla_norm[i], nb=nb, t=t)
            og_s, s_s = _gla_sample(z, w2, gla_gate_b[i], gla_norm[i], state_gla[i], nb=nbs, ts=ts,
                                    row0=n_p)
            a1 = jnp.concatenate([y_c_p, y_c_s], axis=0)
            a2 = jnp.concatenate([og_p, og_s], axis=0)
            w_out = odd_w_out[i].astype(BF)
            x = _outproj(x, a1, a2, w_out[:SGU_WIDTH], w_out[SGU_WIDTH:])
            sgu_p.append(vr_p)
            gla_p.append(s_p)
            sgu_s.append(vr_s.reshape(nbs, ts, SGU_WIDTH))
            gla_s.append(s_s)
        x = _ffn(x, ffn2_norm[l], ffn2_w_gate[l].astype(BF), ffn2_w_up[l].astype(BF),
                 ffn2_w_down[l].astype(BF))
    y = _final_norm(x, final_norm)

    def heads(a, nbat, tt):
        return a.reshape(nbat, -1, tt, DA_HEADS, 2, DA_HEAD_DIM)

    return (y[:n_p].reshape(nb, t, d), y[n_p:].reshape(nbs, ts, d),
            heads(jnp.stack(k_p, axis=1), nb, t),
            jnp.stack(v_p, axis=1).reshape(nb, -1, t, DA_HEADS, DA_PAIR),
            jnp.stack(pool_p, axis=0), jnp.stack(sgu_p, axis=0), jnp.stack(gla_p, axis=0),
            heads(jnp.stack(k_s, axis=1), nbs, ts),
            jnp.stack(v_s, axis=1).reshape(nbs, -1, ts, DA_HEADS, DA_PAIR),
            jnp.stack(pool_s, axis=0), jnp.stack(sgu_s, axis=0), jnp.stack(gla_s, axis=0))
```

```python
import functools
import math

import numpy as np
import jax
import jax.numpy as jnp
from jax import lax
from jax.experimental import pallas as pl
from jax.experimental.pallas import tpu as pltpu

F32 = jnp.float32
BF = jnp.bfloat16

RMS_EPS = 1e-6
ROPE_THETA = 10000.0
PAGE_SIZE = 128

POOL_WINDOWS = (2, 4, 8, 16)
POOL_GROUP_DIM = 64
POOL_WIDTH = 256
POOL_HIST = 15
POOL_PAD = 16

DA_HEADS = 6
DA_HEAD_DIM = 64
DA_PAIR = 2 * DA_HEAD_DIM
DA_WIDTH = DA_HEADS * DA_PAIR
DA_VT_ROWS = DA_PAIR + 16
LOG2E = 1.4426950408889634

SGU_GROUPS = 4
SGU_GROUP_DIM = 128
SGU_WIDTH = 512
SGU_CHUNK = 128

GLA_HEADS = 4
GLA_K_DIM = 64
GLA_V_DIM = 128
GLA_K_WIDTH = 256
GLA_V_WIDTH = 512
GLA_GATE_RANK = 16
GLA_GATE_NORMALIZER = 16.0
GLA_CHUNK = 64
GLA_RANK_PAD = 128

LANES = 128
VMEM_LIMIT = 56 * 1024 * 1024


def _cparams(*sem):
    return pltpu.CompilerParams(dimension_semantics=sem, vmem_limit_bytes=VMEM_LIMIT)


def _pick_tile(n, cap, mult=8):
    best = None
    for t in range(mult, min(n, cap) + 1, mult):
        if n % t == 0:
            best = t
    assert best is not None, (n, cap, mult)
    return best


def _shr(x, pow2):
    assert pow2 & (pow2 - 1) == 0
    return x >> (pow2.bit_length() - 1)


def _block_diag(w):
    g, a, b = w.shape
    out = jnp.zeros((g * a, g * b), w.dtype)
    for i in range(g):
        out = out.at[i * a:(i + 1) * a, i * b:(i + 1) * b].set(w[i])
    return out


ANY_SPEC = pl.BlockSpec(memory_space=pl.ANY)


def _rms(x, axis):
    return x * lax.rsqrt(jnp.mean(x * x, axis=axis, keepdims=True) + RMS_EPS)


def _ffn_kernel(x_ref, g_ref, wg_ref, wu_ref, wd_ref, o_ref, hn_ref, acc_ref, *, nf):
    f = pl.program_id(1)

    @pl.when(f == 0)
    def _():
        hn_ref[...] = (_rms(x_ref[...], -1) * g_ref[...]).astype(BF)
        acc_ref[...] = jnp.zeros_like(acc_ref)

    hn = hn_ref[...]
    a = jnp.dot(hn, wg_ref[...], preferred_element_type=F32)
    u = jnp.dot(hn, wu_ref[...], preferred_element_type=F32)
    h = (a * jax.nn.sigmoid(a) * u).astype(BF)
    acc_ref[...] += jnp.dot(h, wd_ref[...], preferred_element_type=F32)

    @pl.when(f == nf - 1)
    def _():
        o_ref[...] = x_ref[...] + 0.5 * acc_ref[...]


def _ffn(x, g, wg, wu, wd):
    n, d = x.shape
    ff = wg.shape[1]
    tm = _pick_tile(n, 1280)
    tf = _pick_tile(ff, 256, LANES)
    nf = ff // tf
    return pl.pallas_call(
        functools.partial(_ffn_kernel, nf=nf),
        grid=(n // tm, nf),
        in_specs=[
            pl.BlockSpec((tm, d), lambda m, f: (m, 0)),
            pl.BlockSpec((1, d), lambda m, f: (0, 0)),
            pl.BlockSpec((d, tf), lambda m, f: (0, f)),
            pl.BlockSpec((d, tf), lambda m, f: (0, f)),
            pl.BlockSpec((tf, d), lambda m, f: (f, 0)),
        ],
        out_specs=pl.BlockSpec((tm, d), lambda m, f: (m, 0)),
        out_shape=jax.ShapeDtypeStruct((n, d), F32),
        scratch_shapes=[pltpu.VMEM((tm, d), BF), pltpu.VMEM((tm, d), F32)],
        compiler_params=_cparams("parallel", "arbitrary"),
        name="ffn",
    )(x, g.reshape(1, d), wg, wu, wd)


def _inproj_kernel(x_ref, g_ref, w_ref, o_ref, hn_ref):
    @pl.when(pl.program_id(1) == 0)
    def _():
        hn_ref[...] = (_rms(x_ref[...], -1) * g_ref[...]).astype(BF)

    o_ref[...] = jnp.dot(hn_ref[...], w_ref[...], preferred_element_type=F32)


def _inproj(x, g, w):
    n, d = x.shape
    nout = w.shape[1]
    tm = _pick_tile(n, 1280)
    tn = _pick_tile(nout, 512, LANES)
    return pl.pallas_call(
        _inproj_kernel,
        grid=(n // tm, nout // tn),
        in_specs=[
            pl.BlockSpec((tm, d), lambda m, j: (m, 0)),
            pl.BlockSpec((1, d), lambda m, j: (0, 0)),
            pl.BlockSpec((d, tn), lambda m, j: (0, j)),
        ],
        out_specs=pl.BlockSpec((tm, tn), lambda m, j: (m, j)),
        out_shape=jax.ShapeDtypeStruct((n, nout), F32),
        scratch_shapes=[pltpu.VMEM((tm, d), BF)],
        compiler_params=_cparams("parallel", "arbitrary"),
        name="inproj",
    )(x, g.reshape(1, d), w)


def _outproj_kernel(x_ref, a1_ref, a2_ref, w1_ref, w2_ref, o_ref):
    y = jnp.dot(a1_ref[...], w1_ref[...], preferred_element_type=F32)
    y += jnp.dot(a2_ref[...], w2_ref[...], preferred_element_type=F32)
    o_ref[...] = x_ref[...] + y


def _outproj(x, a1, a2, w1, w2):
    n, d = x.shape
    k1, k2 = a1.shape[1], a2.shape[1]
    tm = _pick_tile(n, 1280)
    return pl.pallas_call(
        _outproj_kernel,
        grid=(n // tm,),
        in_specs=[
            pl.BlockSpec((tm, d), lambda m: (m, 0)),
            pl.BlockSpec((tm, k1), lambda m: (m, 0)),
            pl.BlockSpec((tm, k2), lambda m: (m, 0)),
            pl.BlockSpec((k1, d), lambda m: (0, 0)),
            pl.BlockSpec((k2, d), lambda m: (0, 0)),
        ],
        out_specs=pl.BlockSpec((tm, d), lambda m: (m, 0)),
        out_shape=jax.ShapeDtypeStruct((n, d), F32),
        compiler_params=_cparams("parallel"),
        name="outproj",
    )(x, a1, a2, w1, w2)


def _final_norm_kernel(x_ref, g_ref, o_ref):
    o_ref[...] = _rms(x_ref[...], -1) * g_ref[...]


def _final_norm(x, g, *, row0, nrows):
    d = x.shape[1]
    tm = _pick_tile(math.gcd(nrows, row0) if row0 else nrows, 1024)
    rb = row0 // tm
    return pl.pallas_call(
        _final_norm_kernel,
        grid=(nrows // tm,),
        in_specs=[pl.BlockSpec((tm, d), lambda m: (rb + m, 0)), pl.BlockSpec((1, d), lambda m: (0, 0))],
        out_specs=pl.BlockSpec((tm, d), lambda m: (m, 0)),
        out_shape=jax.ShapeDtypeStruct((nrows, d), F32),
        compiler_params=_cparams("parallel"),
        name="final_norm",
    )(x, g.reshape(1, d))


def _pool_kernel(pc_ref, pp_ref, w_ref, sc_ref, dst_ref, o_ref, *, tm, pos0, prev_at_first):
    del dst_ref
    i = pl.program_id(1)
    p = pc_ref[...]
    prev = pp_ref[...]
    if not prev_at_first:
        prev = jnp.where(i > 0, prev, 0.0)
    ext = jnp.concatenate([prev, p], axis=0)
    s2 = ext + pltpu.roll(ext, 1, 0)
    s4 = s2 + pltpu.roll(s2, 2, 0)
    s8 = s4 + pltpu.roll(s4, 4, 0)
    s16 = s8 + pltpu.roll(s8, 8, 0)
    grp = _shr(lax.broadcasted_iota(jnp.int32, (tm, POOL_WIDTH), 1), POOL_GROUP_DIM)
    pos = pos0 + i * tm + lax.broadcasted_iota(jnp.int32, (tm, POOL_WIDTH), 0)
    sums = (s2, s4, s8, s16)
    s = sums[3][POOL_PAD:]
    win = jnp.full((tm, POOL_WIDTH), POOL_WINDOWS[3], jnp.int32)
    for gi in (2, 1, 0):
        s = jnp.where(grp == gi, sums[gi][POOL_PAD:], s)
        win = jnp.where(grp == gi, POOL_WINDOWS[gi], win)
    cnt = jnp.minimum(pos + 1, win).astype(F32)
    diff = s / cnt - p
    y = jnp.dot(diff.astype(BF), w_ref[...], preferred_element_type=F32) * sc_ref[...]
    o_ref[...] = y.astype(BF)


def _pool(z, prev_src, w_bd, scale, dst, *, nb, t, row0, col_blk, prev_is_state, pos0):
    tm = _pick_tile(t, 512)
    nt = t // tm
    rb0 = row0 // tm
    if prev_is_state:
        assert nt == 1
        prev_spec = pl.BlockSpec((POOL_PAD, POOL_WIDTH), lambda b, i: (b, 0))
    else:
        r16 = tm // POOL_PAD
        base16 = row0 // POOL_PAD
        prev_spec = pl.BlockSpec(
            (POOL_PAD, POOL_WIDTH),
            lambda b, i: (jnp.maximum(base16 + (b * nt + i) * r16 - 1, 0), col_blk))
    return pl.pallas_call(
        functools.partial(_pool_kernel, tm=tm, pos0=pos0, prev_at_first=prev_is_state),
        grid=(nb, nt),
        in_specs=[
            pl.BlockSpec((tm, POOL_WIDTH), lambda b, i: (rb0 + b * nt + i, col_blk)),
            prev_spec,
            pl.BlockSpec((POOL_WIDTH, POOL_WIDTH), lambda b, i: (0, 0)),
            pl.BlockSpec((1, POOL_WIDTH), lambda b, i: (0, 0)),
            ANY_SPEC,
        ],
        out_specs=pl.BlockSpec((tm, POOL_WIDTH), lambda b, i: (rb0 + b * nt + i, 0)),
        out_shape=jax.ShapeDtypeStruct(dst.shape, dst.dtype),
        input_output_aliases={4: 0},
        compiler_params=_cparams("parallel", "arbitrary"),
        name="pool",
    )(z, prev_src, w_bd, scale.reshape(1, POOL_WIDTH), dst)


def _rope_tables(pos):
    half = DA_HEAD_DIM // 2
    inv = ROPE_THETA ** (-jnp.arange(half, dtype=F32) / half)
    ang = pos.astype(F32)[:, None] * inv[None, :]
    cos, sin = jnp.cos(ang), jnp.sin(ang)
    cos128 = jnp.concatenate([cos, cos, cos, cos], axis=1)
    sin128 = jnp.concatenate([-sin, sin, -sin, sin], axis=1)
    return cos128, sin128


def _rope128(x, cos, sin):
    lane = lax.broadcasted_iota(jnp.int32, x.shape, 1)
    first = (lane & (DA_HEAD_DIM - 1)) < (DA_HEAD_DIM // 2)
    partner = jnp.where(first, pltpu.roll(x, LANES - 32, 1), pltpu.roll(x, 32, 1))
    return x * cos + partner * sin


def _prep_prompt_kernel(zq_ref, zk_ref, zv_ref, cos_ref, sin_ref, kt_dst_ref, vh_dst_ref,
                        qt_ref, kb_ref, vt_ref, kt_ref, vh_ref):
    del kt_dst_ref, vh_dst_ref
    cos, sin = cos_ref[...], sin_ref[...]
    scale = DA_HEAD_DIM ** -0.5 * LOG2E
    ones = jnp.ones((DA_VT_ROWS - DA_PAIR, zq_ref.shape[0]), BF)
    for h in range(DA_HEADS):
        sl = slice(h * DA_PAIR, (h + 1) * DA_PAIR)
        q = _rope128(zq_ref[:, sl], cos, sin) * scale
        qt_ref[h] = q.T.astype(BF)
        k = _rope128(zk_ref[:, sl], cos, sin)
        kt_ref[h] = k.T
        kb_ref[h] = k.astype(BF)
        v = zv_ref[:, sl]
        vh_ref[h] = v
        vt_ref[h, :DA_PAIR] = v.T.astype(BF)
        vt_ref[h, DA_PAIR:] = ones


def _prep_prompt(z, cos, sin, kt_all, vh_all, *, layer, nb, t, tk):
    nt = t // tk
    return pl.pallas_call(
        _prep_prompt_kernel,
        grid=(nb, nt),
        in_specs=[
            pl.BlockSpec((tk, DA_WIDTH), lambda b, i: (b * nt + i, 0)),
            pl.BlockSpec((tk, DA_WIDTH), lambda b, i: (b * nt + i, 1)),
            pl.BlockSpec((tk, DA_WIDTH), lambda b, i: (b * nt + i, 2)),
            pl.BlockSpec((tk, LANES), lambda b, i: (i, 0)),
            pl.BlockSpec((tk, LANES), lambda b, i: (i, 0)),
            ANY_SPEC,
            ANY_SPEC,
        ],
        out_specs=[
            pl.BlockSpec((None, DA_HEADS, DA_PAIR, tk), lambda b, i: (b, 0, 0, i)),
            pl.BlockSpec((None, DA_HEADS, tk, DA_PAIR), lambda b, i: (b, 0, i, 0)),
            pl.BlockSpec((None, DA_HEADS, None, DA_VT_ROWS, tk), lambda b, i: (b, 0, i, 0, 0)),
            pl.BlockSpec((None, None, DA_HEADS, DA_PAIR, tk), lambda b, i: (b, layer, 0, 0, i)),
            pl.BlockSpec((None, None, DA_HEADS, tk, DA_PAIR), lambda b, i: (b, layer, 0, i, 0)),
        ],
        out_shape=[
            jax.ShapeDtypeStruct((nb, DA_HEADS, DA_PAIR, t), BF),
            jax.ShapeDtypeStruct((nb, DA_HEADS, t, DA_PAIR), BF),
            jax.ShapeDtypeStruct((nb, DA_HEADS, nt, DA_VT_ROWS, tk), BF),
            jax.ShapeDtypeStruct(kt_all.shape, kt_all.dtype),
            jax.ShapeDtypeStruct(vh_all.shape, vh_all.dtype),
        ],
        input_output_aliases={5: 3, 6: 4},
        compiler_params=_cparams("parallel", "parallel"),
        name="prep_prompt",
    )(z, z, z, cos, sin, kt_all, vh_all)


def _lambda_value(lamv_ref, lam_init):
    lv = lamv_ref[...]
    s01 = jnp.sum(lv[0:1] * lv[1:2], axis=1, keepdims=True)
    s23 = jnp.sum(lv[2:3] * lv[3:4], axis=1, keepdims=True)
    return jnp.exp(s01) - jnp.exp(s23) + lam_init


def _attn_prompt_kernel(qt_ref, k_ref, vt_ref, lamv_ref, g_ref, dst_ref, o_ref,
                        qx_ref, sa_ref, sb_ref, m_ref, acc_ref, *, tq, lam_init):
    del dst_ref
    qi = pl.program_id(2)
    qt = qt_ref[...]
    row = lax.broadcasted_iota(jnp.int32, qt.shape, 0)
    zero = jnp.zeros_like(qt)
    qx_ref[:, :tq] = jnp.where(row < DA_HEAD_DIM, qt, zero)
    qx_ref[:, tq:] = jnp.where(row >= DA_HEAD_DIM, qt, zero)

    def scores(j):
        kj = k_ref[pl.ds(pl.multiple_of(j * tq, tq), tq), :]
        return jnp.dot(kj, qx_ref[...], preferred_element_type=F32)

    def update(s, j):
        m_prev = m_ref[...]
        m_new = jnp.maximum(m_prev, jnp.max(s, axis=0, keepdims=True))
        p = jnp.exp2(s - m_new).astype(BF)
        acc_ref[...] = jnp.exp2(m_prev - m_new) * acc_ref[...] + jnp.dot(
            vt_ref[j], p, preferred_element_type=F32)
        m_ref[...] = m_new

    s = scores(qi)
    kpos = lax.broadcasted_iota(jnp.int32, s.shape, 0)
    qpos = lax.broadcasted_iota(jnp.int32, s.shape, 1) & (tq - 1)
    s = jnp.where(kpos <= qpos, s, -jnp.inf)
    m0 = jnp.max(s, axis=0, keepdims=True)
    m_ref[...] = m0
    acc_ref[...] = jnp.dot(vt_ref[qi], jnp.exp2(s - m0).astype(BF), preferred_element_type=F32)

    first = qi & 1

    @pl.when(first == 1)
    def _():
        update(scores(0), 0)

    npairs = qi >> 1

    @pl.when(npairs > 0)
    def _():
        sa_ref[...] = scores(first)

    def pair(pi, carry):
        j = first + 2 * pi
        sb_ref[...] = scores(j + 1)
        update(sa_ref[...], j)
        sa_ref[...] = scores(jnp.minimum(j + 2, qi - 1))
        update(sb_ref[...], j + 1)
        return carry

    lax.fori_loop(0, npairs, pair, 0)

    acc = acc_ref[...]
    on = acc[:DA_PAIR] / acc[DA_PAIR:DA_PAIR + 1]
    lam = _lambda_value(lamv_ref, lam_init)
    ot = on[:, :tq] - lam * on[:, tq:]
    ot = _rms(ot, 0) * g_ref[...] * (1.0 - lam_init)
    o_ref[...] = ot.T.astype(BF)


def _attn_prompt(qt, kb, vt, lamv, subln_g, dst, *, tq, lam_init):
    nb, nh, _, t = qt.shape
    nk = t // tq
    return pl.pallas_call(
        functools.partial(_attn_prompt_kernel, tq=tq, lam_init=lam_init),
        grid=(nb, nh, t // tq),
        in_specs=[
            pl.BlockSpec((None, None, DA_PAIR, tq), lambda b, h, i: (b, h, 0, i)),
            pl.BlockSpec((None, None, t, DA_PAIR), lambda b, h, i: (b, h, 0, 0)),
            pl.BlockSpec((None, None, nk, DA_VT_ROWS, tq), lambda b, h, i: (b, h, 0, 0, 0)),
            pl.BlockSpec((4, DA_HEAD_DIM), lambda b, h, i: (0, 0)),
            pl.BlockSpec((DA_PAIR, 1), lambda b, h, i: (0, 0)),
            ANY_SPEC,
        ],
        out_specs=pl.BlockSpec((tq, DA_PAIR), lambda b, h, i: (b * nk + i, h)),
        out_shape=jax.ShapeDtypeStruct(dst.shape, dst.dtype),
        scratch_shapes=[pltpu.VMEM((DA_PAIR, 2 * tq), BF),
                        pltpu.VMEM((tq, 2 * tq), F32), pltpu.VMEM((tq, 2 * tq), F32),
                        pltpu.VMEM((1, 2 * tq), F32), pltpu.VMEM((DA_VT_ROWS, 2 * tq), F32)],
        input_output_aliases={5: 0},
        compiler_params=_cparams("parallel", "parallel", "arbitrary"),
        name="attn_prompt",
    )(qt, kb, vt, lamv, subln_g.reshape(DA_PAIR, 1), dst)


def _rope_rows_kernel(zq_ref, zk_ref, cos_ref, sin_ref, q_ref, k_ref):
    cos, sin = cos_ref[...], sin_ref[...]
    scale = DA_HEAD_DIM ** -0.5
    for h in range(DA_HEADS):
        sl = slice(h * DA_PAIR, (h + 1) * DA_PAIR)
        q_ref[:, sl] = (_rope128(zq_ref[:, sl], cos, sin) * scale).astype(BF)
        k_ref[:, sl] = _rope128(zk_ref[:, sl], cos, sin)


def _rope_rows(z, cos, sin, *, row0, nrows):
    rb = row0 // nrows
    return pl.pallas_call(
        _rope_rows_kernel,
        grid=(1,),
        in_specs=[
            pl.BlockSpec((nrows, DA_WIDTH), lambda i: (rb, 0)),
            pl.BlockSpec((nrows, DA_WIDTH), lambda i: (rb, 1)),
            pl.BlockSpec((nrows, LANES), lambda i: (0, 0)),
            pl.BlockSpec((nrows, LANES), lambda i: (0, 0)),
        ],
        out_specs=[pl.BlockSpec((nrows, DA_WIDTH), lambda i: (0, 0)),
                   pl.BlockSpec((nrows, DA_WIDTH), lambda i: (0, 0))],
        out_shape=[jax.ShapeDtypeStruct((nrows, DA_WIDTH), BF),
                   jax.ShapeDtypeStruct((nrows, DA_WIDTH), F32)],
        compiler_params=_cparams("arbitrary"),
        name="rope_rows",
    )(z, z, cos, sin)


def _attn_sample_kernel(pt_ref, q_ref, kn_ref, vn_ref, sel_ref, lamv_ref, g_ref, *rest,
                        pp, nsteps, ts, lam_init):
    kpages, vpages = rest[:pp], rest[pp:2 * pp]
    _, o_ref, m_ref, l_ref, acc_ref, qbd_ref = rest[2 * pp:]
    s = pl.program_id(1)
    nrow = 2 * DA_HEADS * ts

    @pl.when(s == 0)
    def _():
        q = q_ref[...]
        qrep = jnp.concatenate([q] * (2 * DA_HEADS), axis=0)
        qbd_ref[...] = jnp.where(sel_ref[...] > 0, qrep, jnp.zeros_like(qrep))
        m_ref[...] = jnp.full(m_ref.shape, -jnp.inf, F32)
        l_ref[...] = jnp.zeros_like(l_ref)
        acc_ref[...] = jnp.zeros_like(acc_ref)

    def update(sc, vv, causal):
        if causal:
            tq = lax.broadcasted_iota(jnp.int32, sc.shape, 0) & (ts - 1)
            tk = lax.broadcasted_iota(jnp.int32, sc.shape, 1)
            sc = jnp.where(tk <= tq, sc, -jnp.inf)
        m_prev = m_ref[...]
        m_new = jnp.maximum(m_prev, jnp.max(sc, axis=1, keepdims=True))
        alpha = jnp.exp(m_prev - m_new)
        p = jnp.exp(sc - m_new)
        l_ref[...] = alpha * l_ref[...] + jnp.sum(p, axis=1, keepdims=True)
        acc_ref[...] = alpha * acc_ref[...] + jnp.dot(p.astype(BF), vv, preferred_element_type=F32)
        m_ref[...] = m_new

    if pp:
        kt = jnp.concatenate([r[...] for r in kpages], axis=1).astype(BF)
        vv = jnp.concatenate(
            [jnp.concatenate([r[h] for h in range(DA_HEADS)], axis=1) for r in vpages],
            axis=0).astype(BF)
        update(jnp.dot(qbd_ref[...], kt, preferred_element_type=F32), vv, False)

    @pl.when(s == nsteps - 1)
    def _():
        sc_new = lax.dot_general(qbd_ref[...], kn_ref[...].astype(BF), (((1,), (1,)), ((), ())),
                                 preferred_element_type=F32)
        update(sc_new, vn_ref[...].astype(BF), True)
        on = acc_ref[...] / l_ref[...]
        lam = _lambda_value(lamv_ref, lam_init)
        half = nrow // 2
        d = on[:half] - lam * on[half:]
        outs = []
        for h in range(DA_HEADS):
            blk = d[h * ts:(h + 1) * ts, h * DA_PAIR:(h + 1) * DA_PAIR]
            outs.append(_rms(blk, -1) * g_ref[...] * (1.0 - lam_init))
        o_ref[...] = jnp.concatenate(outs, axis=1).astype(BF)


def _attn_sample(q_s, k_new, z, cache_kt, cache_vh, page_table, lamv, subln_g, dst, *,
                 ts, layer, row0, lam_init):
    nb = q_s.shape[0] // ts
    n_pages = page_table.shape[1]
    pp = 0
    for cand in (8, 4, 2, 1):
        if n_pages and n_pages % cand == 0:
            pp = cand
            break
    nsteps = max(n_pages // pp, 1) if pp else 1
    nrow = 2 * DA_HEADS * ts
    r = np.arange(nrow)[:, None] // ts
    c = np.arange(DA_WIDTH)[None, :] // DA_HEAD_DIM
    sel = jnp.asarray(((r % DA_HEADS) * 2 + r // DA_HEADS == c).astype(np.float32))
    rb = row0 // ts

    def kpage_spec(i):
        return pl.BlockSpec((None, None, DA_WIDTH, PAGE_SIZE),
                            lambda b, s, pt: (pt[b * n_pages + s * pp + i], layer, 0, 0))

    def vpage_spec(i):
        return pl.BlockSpec((None, None, DA_HEADS, PAGE_SIZE, DA_PAIR),
                            lambda b, s, pt: (pt[b * n_pages + s * pp + i], layer, 0, 0, 0))

    in_specs = [
        pl.BlockSpec((ts, DA_WIDTH), lambda b, s, pt: (b, 0)),
        pl.BlockSpec((ts, DA_WIDTH), lambda b, s, pt: (b, 0)),
        pl.BlockSpec((ts, DA_WIDTH), lambda b, s, pt: (rb + b, 2)),
        pl.BlockSpec((nrow, DA_WIDTH), lambda b, s, pt: (0, 0)),
        pl.BlockSpec((4, DA_HEAD_DIM), lambda b, s, pt: (0, 0)),
        pl.BlockSpec((1, DA_PAIR), lambda b, s, pt: (0, 0)),
    ] + [kpage_spec(i) for i in range(pp)] + [vpage_spec(i) for i in range(pp)] + [ANY_SPEC]
    grid_spec = pltpu.PrefetchScalarGridSpec(
        num_scalar_prefetch=1,
        grid=(nb, nsteps),
        in_specs=in_specs,
        out_specs=pl.BlockSpec((ts, DA_WIDTH), lambda b, s, pt: (rb + b, 0)),
        scratch_shapes=[pltpu.VMEM((nrow, 1), F32), pltpu.VMEM((nrow, 1), F32),
                        pltpu.VMEM((nrow, DA_WIDTH), F32), pltpu.VMEM((nrow, DA_WIDTH), BF)],
    )
    return pl.pallas_call(
        functools.partial(_attn_sample_kernel, pp=pp, nsteps=nsteps, ts=ts, lam_init=lam_init),
        grid_spec=grid_spec,
        out_shape=jax.ShapeDtypeStruct(dst.shape, dst.dtype),
        input_output_aliases={7 + 2 * pp: 0},
        compiler_params=_cparams("parallel", "arbitrary"),
        name="attn_sample",
    )(page_table.reshape(-1), q_s, k_new, z, sel, lamv, subln_g.reshape(1, DA_PAIR),
      *([cache_kt] * pp), *([cache_vh] * pp), dst)


def _sgu_norm_v(sv_ref, ng_ref, g):
    x = jax.nn.gelu(sv_ref[:, g * SGU_GROUP_DIM:(g + 1) * SGU_GROUP_DIM])
    return _rms(x, -1) * ng_ref[g:g + 1, :]


def _sgu_prompt_kernel(u_ref, sv_ref, ng_ref, w_ref, bs_ref, dst_ref, y_ref, vr_ref, *, tm, c):
    del dst_ref
    last = pl.program_id(1) == pl.num_programs(1) - 1
    ri = lax.broadcasted_iota(jnp.int32, (c, c), 0)
    ci = lax.broadcasted_iota(jnp.int32, (c, c), 1)
    for g in range(SGU_GROUPS):
        sl = slice(g * SGU_GROUP_DIM, (g + 1) * SGU_GROUP_DIM)
        v = _sgu_norm_v(sv_ref, ng_ref, g)

        @pl.when(last)
        def _():
            vr_ref[:, sl] = v[tm - c:]

        vb = v.astype(BF)
        w = jnp.where(ri >= ci, w_ref[g], 0.0).astype(BF)
        for n in range(tm // c):
            rows = slice(n * c, (n + 1) * c)
            mixed = jnp.dot(w, vb[rows], preferred_element_type=F32) + bs_ref[g]
            y_ref[rows, sl] = (jax.nn.gelu(u_ref[rows, sl]) * mixed).astype(BF)


def _sgu_prompt(z, norm_g, w_s, b_s, dst, *, nb, t):
    c = SGU_CHUNK
    tm = _pick_tile(t, 512, c)
    nt = t // tm
    bs = jnp.broadcast_to(b_s[:, :c, None], (SGU_GROUPS, c, SGU_GROUP_DIM))
    return pl.pallas_call(
        functools.partial(_sgu_prompt_kernel, tm=tm, c=c),
        grid=(nb, nt),
        in_specs=[
            pl.BlockSpec((tm, SGU_WIDTH), lambda b, i: (b * nt + i, 0)),
            pl.BlockSpec((tm, SGU_WIDTH), lambda b, i: (b * nt + i, 1)),
            pl.BlockSpec((SGU_GROUPS, SGU_GROUP_DIM), lambda b, i: (0, 0)),
            pl.BlockSpec((SGU_GROUPS, c, c), lambda b, i: (0, 0, 0)),
            pl.BlockSpec((SGU_GROUPS, c, SGU_GROUP_DIM), lambda b, i: (0, 0, 0)),
            ANY_SPEC,
        ],
        out_specs=[
            pl.BlockSpec((tm, SGU_WIDTH), lambda b, i: (b * nt + i, 0)),
            pl.BlockSpec((None, c, SGU_WIDTH), lambda b, i: (b, 0, 0)),
        ],
        out_shape=[jax.ShapeDtypeStruct(dst.shape, dst.dtype),
                   jax.ShapeDtypeStruct((nb, c, SGU_WIDTH), F32)],
        input_output_aliases={5: 0},
        compiler_params=_cparams("parallel", "arbitrary"),
        name="sgu_prompt",
    )(z, z, norm_g, w_s[:, :c, :c], bs, dst)


def _sgu_sample_kernel(u_ref, sv_ref, ng_ref, w_ref, bs_ref, dst_ref, y_ref, vr_ref, *, nb, ts):
    del dst_ref
    v = jnp.concatenate([_sgu_norm_v(sv_ref, ng_ref, g) for g in range(SGU_GROUPS)], axis=1)
    vr_ref[...] = v
    v3 = v.reshape(nb, ts, SGU_WIDTH)
    ii = lax.broadcasted_iota(jnp.int32, (ts, SGU_WIDTH), 0)
    mixed = jnp.broadcast_to(bs_ref[...][None], (nb, ts, SGU_WIDTH))
    for j in range(ts):
        wj = jnp.where(ii >= j, w_ref[j], 0.0)
        mixed = mixed + wj[None] * v3[:, j:j + 1, :]
    y = jax.nn.gelu(u_ref[...]).reshape(nb, ts, SGU_WIDTH) * mixed
    y_ref[...] = y.reshape(nb * ts, SGU_WIDTH).astype(BF)


def _sgu_sample(z, norm_g, w_s, b_s, dst, *, nb, ts, row0):
    nrows = nb * ts
    rb = row0 // nrows
    w_exp = jnp.repeat(jnp.transpose(w_s[:, :ts, :ts], (2, 1, 0)), SGU_GROUP_DIM, axis=2)
    b_exp = jnp.repeat(jnp.transpose(b_s[:, :ts], (1, 0)), SGU_GROUP_DIM, axis=1)
    return pl.pallas_call(
        functools.partial(_sgu_sample_kernel, nb=nb, ts=ts),
        grid=(1,),
        in_specs=[
            pl.BlockSpec((nrows, SGU_WIDTH), lambda i: (rb, 0)),
            pl.BlockSpec((nrows, SGU_WIDTH), lambda i: (rb, 1)),
            pl.BlockSpec((SGU_GROUPS, SGU_GROUP_DIM), lambda i: (0, 0)),
            pl.BlockSpec((ts, ts, SGU_WIDTH), lambda i: (0, 0, 0)),
            pl.BlockSpec((ts, SGU_WIDTH), lambda i: (0, 0)),
            ANY_SPEC,
        ],
        out_specs=[pl.BlockSpec((nrows, SGU_WIDTH), lambda i: (rb, 0)),
                   pl.BlockSpec((nrows, SGU_WIDTH), lambda i: (0, 0))],
        out_shape=[jax.ShapeDtypeStruct(dst.shape, dst.dtype),
                   jax.ShapeDtypeStruct((nrows, SGU_WIDTH), F32)],
        input_output_aliases={5: 0},
        compiler_params=_cparams("arbitrary"),
        name="sgu_sample",
    )(z, z, norm_g, w_exp, b_exp, dst)


def _log_sigmoid(x):
    return jnp.minimum(x, 0.0) - jnp.log1p(jnp.exp(-jnp.abs(x)))


def _seg_scan(x, seg, pos, reverse):
    n = x.shape[0]
    r = pos & (seg - 1)
    t = 1
    while t < seg:
        if reverse:
            x = x + jnp.where(r < seg - t, pltpu.roll(x, n - t, 0), 0.0)
        else:
            x = x + jnp.where(r >= t, pltpu.roll(x, t, 0), 0.0)
        t *= 2
    return x


def _gla_decays(gk, c):
    pos = lax.broadcasted_iota(jnp.int32, gk.shape, 0)
    out = {1: (jnp.exp(gk), None)}
    s = 2
    while s <= c:
        pre = _seg_scan(gk, s, pos, False)
        suf = _seg_scan(gk, s, pos, True) - gk
        out[s] = (jnp.exp(pre), jnp.exp(suf))
        s *= 2
    return out


def _gla_intra(q, k, v, dec, c, sc):
    n = q.shape[0]
    ri = lax.broadcasted_iota(jnp.int32, (sc, sc), 0)
    ci = lax.broadcasted_iota(jnp.int32, (sc, sc), 1)
    same_chunk = _shr(ri, c) == _shr(ci, c)
    lane = lax.broadcasted_iota(jnp.int32, (sc, LANES), 1)
    levels = []
    levels.append((ri == ci, q.astype(BF), k.astype(BF)))
    s = 1
    while s < c:
        pre, _ = dec[s]
        suf = dec[s][1]
        ks = k if suf is None else k * suf
        mask = same_chunk & (_shr(ri, 2 * s) == _shr(ci, 2 * s)) & ((_shr(ri, s) & 1) == 1) \
            & ((_shr(ci, s) & 1) == 0)
        levels.append((mask, (q * pre).astype(BF), ks.astype(BF)))
        s *= 2
    outs = []
    for h in range(GLA_HEADS):
        pair = slice((h // 2) * LANES, (h // 2 + 1) * LANES)
        own = _shr(lane, GLA_K_DIM) == (h % 2)
        vh = v[:, h * GLA_V_DIM:(h + 1) * GLA_V_DIM].astype(BF)
        rows_out = []
        for b0 in range(0, n, sc):
            rows = slice(b0, b0 + sc)
            a = jnp.zeros((sc, sc), F32)
            for mask, qs, ks in levels:
                qm = jnp.where(own, qs[rows, pair], jnp.zeros((sc, LANES), BF))
                al = lax.dot_general(qm, ks[rows, pair], (((1,), (1,)), ((), ())),
                                     preferred_element_type=F32)
                a = a + jnp.where(mask, al, 0.0)
            rows_out.append(jnp.dot(a.astype(BF), vh[rows], preferred_element_type=F32))
        outs.append(jnp.concatenate(rows_out, axis=0) if len(rows_out) > 1 else rows_out[0])
    return outs


def _gla_gate(glr_ref, w2_ref, gb_ref):
    pre = jnp.dot(glr_ref[...].astype(BF), w2_ref[...], preferred_element_type=F32) + gb_ref[...]
    return _log_sigmoid(pre) / GLA_GATE_NORMALIZER


def _gla_finish(o_heads, g_ref, gn_ref, o_ref):
    for h in range(GLA_HEADS):
        sl = slice(h * GLA_V_DIM, (h + 1) * GLA_V_DIM)
        gate = g_ref[:, sl]
        o_ref[:, sl] = (_rms(o_heads[h], -1) * gn_ref[...] * (gate * jax.nn.sigmoid(gate))).astype(BF)


def _gla_prompt_kernel(q_ref, k_ref, v_ref, g_ref, glr_ref, w2_ref, gb_ref, gn_ref, dst_ref,
                       o_ref, s_out_ref, st_ref, *, tm, c):
    del dst_ref
    i = pl.program_id(1)

    @pl.when(i == 0)
    def _():
        st_ref[...] = jnp.zeros_like(st_ref)

    gk = _gla_gate(glr_ref, w2_ref, gb_ref)
    q = q_ref[...] * (GLA_K_DIM ** -0.5)
    k = k_ref[...]
    v = v_ref[...]
    dec = _gla_decays(gk, c)
    o_heads = _gla_intra(q, k, v, dec, c, min(tm, LANES))
    eb, esuf = dec[c]
    qd = (q * eb).astype(BF)
    kd = (k * esuf).astype(BF)
    lane = lax.broadcasted_iota(jnp.int32, (c, LANES), 1)
    for h in range(GLA_HEADS):
        pair = slice((h // 2) * LANES, (h // 2 + 1) * LANES)
        own = _shr(lane, GLA_K_DIM) == (h % 2)
        vh = v[:, h * GLA_V_DIM:(h + 1) * GLA_V_DIM].astype(BF)
        st = st_ref[h]
        inter = []
        for n in range(tm // c):
            rows = slice(n * c, (n + 1) * c)
            qm = jnp.where(own, qd[rows, pair], jnp.zeros((c, LANES), BF))
            inter.append(lax.dot_general(qm, st.astype(BF), (((1,), (1,)), ((), ())),
                                         preferred_element_type=F32))
            upd = lax.dot_general(vh[rows], kd[rows, pair], (((0,), (0,)), ((), ())),
                                  preferred_element_type=F32)
            st = eb[n * c + c - 1:n * c + c, pair] * st + upd
        st_ref[h] = st
        o_heads[h] = o_heads[h] + jnp.concatenate(inter, axis=0)
    _gla_finish(o_heads, g_ref, gn_ref, o_ref)

    @pl.when(i == pl.num_programs(1) - 1)
    def _():
        for h in range(GLA_HEADS):
            lo = (h % 2) * GLA_K_DIM
            s_out_ref[h] = st_ref[h].T[lo:lo + GLA_K_DIM, :]


def _gla_prompt(z, w2, gate_b, gla_norm, dst, *, nb, t):
    c = GLA_CHUNK
    tm = _pick_tile(t, 256, LANES)
    nt = t // tm
    return pl.pallas_call(
        functools.partial(_gla_prompt_kernel, tm=tm, c=c),
        grid=(nb, nt),
        in_specs=[
            pl.BlockSpec((tm, GLA_K_WIDTH), lambda b, i: (b * nt + i, 4)),
            pl.BlockSpec((tm, GLA_K_WIDTH), lambda b, i: (b * nt + i, 5)),
            pl.BlockSpec((tm, GLA_V_WIDTH), lambda b, i: (b * nt + i, 3)),
            pl.BlockSpec((tm, GLA_V_WIDTH), lambda b, i: (b * nt + i, 4)),
            pl.BlockSpec((tm, GLA_RANK_PAD), lambda b, i: (b * nt + i, 20)),
            pl.BlockSpec((GLA_RANK_PAD, GLA_K_WIDTH), lambda b, i: (0, 0)),
            pl.BlockSpec((1, GLA_K_WIDTH), lambda b, i: (0, 0)),
            pl.BlockSpec((1, GLA_V_DIM), lambda b, i: (0, 0)),
            ANY_SPEC,
        ],
        out_specs=[
            pl.BlockSpec((tm, GLA_V_WIDTH), lambda b, i: (b * nt + i, 0)),
            pl.BlockSpec((None, GLA_HEADS, GLA_K_DIM, GLA_V_DIM), lambda b, i: (b, 0, 0, 0)),
        ],
        out_shape=[jax.ShapeDtypeStruct(dst.shape, dst.dtype),
                   jax.ShapeDtypeStruct((nb, GLA_HEADS, GLA_K_DIM, GLA_V_DIM), F32)],
        scratch_shapes=[pltpu.VMEM((GLA_HEADS, GLA_V_DIM, LANES), F32)],
        input_output_aliases={8: 0},
        compiler_params=_cparams("parallel", "arbitrary"),
        name="gla_prompt",
    )(z, z, z, z, z, w2, gate_b.reshape(1, GLA_K_WIDTH), gla_norm.reshape(1, GLA_V_DIM), dst)


def _gla_sample_kernel(q_ref, k_ref, v_ref, g_ref, glr_ref, w2_ref, gb_ref, gbc_ref, gn_ref, s0_ref,
                       dst_ref, o_ref, s_out_ref, *, ts):
    del dst_ref
    gk = _gla_gate(glr_ref, w2_ref, gb_ref)
    q = q_ref[...] * (GLA_K_DIM ** -0.5)
    k = k_ref[...]
    v = v_ref[...]
    dec = _gla_decays(gk, ts)
    o_heads = _gla_intra(q, k, v, dec, ts, ts)
    eb, esuf = dec[ts]
    qd = (q * eb).astype(BF)
    kd = (k * esuf).astype(BF)
    pre_t = lax.dot_general(w2_ref[...], glr_ref[...].astype(BF), (((0,), (1,)), ((), ())),
                            preferred_element_type=F32) + gbc_ref[...]
    dlast = jnp.exp(jnp.sum(_log_sigmoid(pre_t) / GLA_GATE_NORMALIZER, axis=1, keepdims=True))
    lane = lax.broadcasted_iota(jnp.int32, (ts, LANES), 1)
    for h in range(GLA_HEADS):
        pair = slice((h // 2) * LANES, (h // 2 + 1) * LANES)
        own = _shr(lane, GLA_K_DIM) == (h % 2)
        vh = v[:, h * GLA_V_DIM:(h + 1) * GLA_V_DIM].astype(BF)
        s_pair = s0_ref[pair, :]
        qm = jnp.where(own, qd[:, pair], jnp.zeros((ts, LANES), BF))
        o_heads[h] = o_heads[h] + jnp.dot(qm, s_pair.astype(BF), preferred_element_type=F32)
        upd = lax.dot_general(kd[:, pair], vh, (((0,), (0,)), ((), ())),
                              preferred_element_type=F32)
        hr = slice(h * GLA_K_DIM, (h + 1) * GLA_K_DIM)
        lo = (h % 2) * GLA_K_DIM
        s_out_ref[hr, :] = dlast[hr] * s0_ref[hr, :] + upd[lo:lo + GLA_K_DIM]
    _gla_finish(o_heads, g_ref, gn_ref, o_ref)


def _gla_sample(z, w2, gate_b, gla_norm, s0, dst, *, nb, ts, row0):
    rb = row0 // ts
    s0r = s0.reshape(nb, GLA_HEADS * GLA_K_DIM, GLA_V_DIM)
    o, s_new = pl.pallas_call(
        functools.partial(_gla_sample_kernel, ts=ts),
        grid=(nb,),
        in_specs=[
            pl.BlockSpec((ts, GLA_K_WIDTH), lambda b: (rb + b, 4)),
            pl.BlockSpec((ts, GLA_K_WIDTH), lambda b: (rb + b, 5)),
            pl.BlockSpec((ts, GLA_V_WIDTH), lambda b: (rb + b, 3)),
            pl.BlockSpec((ts, GLA_V_WIDTH), lambda b: (rb + b, 4)),
            pl.BlockSpec((ts, GLA_RANK_PAD), lambda b: (rb + b, 20)),
            pl.BlockSpec((GLA_RANK_PAD, GLA_K_WIDTH), lambda b: (0, 0)),
            pl.BlockSpec((1, GLA_K_WIDTH), lambda b: (0, 0)),
            pl.BlockSpec((GLA_K_WIDTH, 1), lambda b: (0, 0)),
            pl.BlockSpec((1, GLA_V_DIM), lambda b: (0, 0)),
            pl.BlockSpec((None, GLA_HEADS * GLA_K_DIM, GLA_V_DIM), lambda b: (b, 0, 0)),
            ANY_SPEC,
        ],
        out_specs=[
            pl.BlockSpec((ts, GLA_V_WIDTH), lambda b: (rb + b, 0)),
            pl.BlockSpec((None, GLA_HEADS * GLA_K_DIM, GLA_V_DIM), lambda b: (b, 0, 0)),
        ],
        out_shape=[jax.ShapeDtypeStruct(dst.shape, dst.dtype),
                   jax.ShapeDtypeStruct((nb, GLA_HEADS * GLA_K_DIM, GLA_V_DIM), F32)],
        input_output_aliases={10: 0},
        compiler_params=_cparams("parallel"),
        name="gla_sample",
    )(z, z, z, z, z, w2, gate_b.reshape(1, GLA_K_WIDTH), gate_b.reshape(GLA_K_WIDTH, 1),
      gla_norm.reshape(1, GLA_V_DIM), s0r, dst)
    return o, s_new.reshape(nb, GLA_HEADS, GLA_K_DIM, GLA_V_DIM)


def kernel(x_prompt, x_sample, cache_k, cache_v, page_table, state_pool, state_gla, ffn1_norm, ffn1_w_gate, ffn1_w_up, ffn1_w_down, mix_norm, ffn2_norm, ffn2_w_gate, ffn2_w_up, ffn2_w_down, even_w_in, even_w_out, pool_w, pool_scale, diff_lambda, diff_subln, odd_w_in, odd_w_out, sgu_norm, sgu_w, sgu_b, gla_gate_w2, gla_gate_b, gla_norm, final_norm):
    nb, t, d = x_prompt.shape
    nbs, ts, _ = x_sample.shape
    depth = ffn1_norm.shape[0]
    n_p, n_s = nb * t, nbs * ts
    past_len = page_table.shape[1] * PAGE_SIZE
    assert ts < min(SGU_CHUNK, GLA_CHUNK) and ts & (ts - 1) == 0 and n_p % n_s == 0
    assert t % SGU_CHUNK == 0 and t >= POOL_PAD

    n = n_p + n_s
    x = jnp.concatenate([x_prompt.reshape(n_p, d), x_sample.reshape(n_s, d)], axis=0)
    tq = _pick_tile(t, 512, LANES)
    cos_p, sin_p = _rope_tables(jnp.arange(t))
    cos_s, sin_s = _rope_tables(past_len + jnp.arange(ts))
    cos_s, sin_s = jnp.tile(cos_s, (nbs, 1)), jnp.tile(sin_s, (nbs, 1))

    n_even = (depth + 1) // 2
    cache_kt = jnp.transpose(cache_k, (0, 1, 3, 4, 5, 2)).reshape(
        cache_k.shape[0], cache_k.shape[1], DA_WIDTH, PAGE_SIZE)
    cache_vh = jnp.transpose(cache_v, (0, 1, 3, 2, 4))
    kt_all = jnp.zeros((nb, n_even, DA_HEADS, DA_PAIR, t), F32)
    vh_all = jnp.zeros((nb, n_even, DA_HEADS, t, DA_PAIR), F32)
    pool_p, sgu_p, gla_p = [], [], []
    k_s, v_s, pool_s, sgu_s, gla_s = [], [], [], [], []
    for l in range(depth):
        i = l // 2
        x = _ffn(x, ffn1_norm[l], ffn1_w_gate[l].astype(BF), ffn1_w_up[l].astype(BF),
                 ffn1_w_down[l].astype(BF))
        if l % 2 == 0:
            lam_init = 0.8 - 0.6 * math.exp(-0.3 * l)
            w_in = even_w_in[i]
            w_in = jnp.concatenate([w_in[:, POOL_WIDTH:], w_in[:, :POOL_WIDTH]], axis=1).astype(BF)
            z = _inproj(x, mix_norm[l], w_in)
            pcol = 3 * DA_WIDTH // POOL_WIDTH
            w_bd = _block_diag(pool_w[i]).astype(BF)
            a1 = jnp.zeros((n, POOL_WIDTH), BF)
            a1 = _pool(z, z, w_bd, pool_scale[i], a1, nb=nb, t=t, row0=0, col_blk=pcol,
                       prev_is_state=False, pos0=0)
            st_pad = jnp.pad(state_pool[i], ((0, 0), (POOL_PAD - POOL_HIST, 0), (0, 0)))
            a1 = _pool(z, st_pad.reshape(nbs * POOL_PAD, POOL_WIDTH), w_bd, pool_scale[i], a1,
                       nb=nbs, t=ts, row0=n_p, col_blk=pcol, prev_is_state=True, pos0=past_len)
            qt, kb, vt, kt_all, vh_all = _prep_prompt(z, cos_p, sin_p, kt_all, vh_all, layer=i,
                                                      nb=nb, t=t, tk=tq)
            a2 = jnp.zeros((n, DA_WIDTH), BF)
            a2 = _attn_prompt(qt, kb, vt, diff_lambda[i], diff_subln[i], a2, tq=tq, lam_init=lam_init)
            q_s, k_new = _rope_rows(z, cos_s, sin_s, row0=n_p, nrows=n_s)
            a2 = _attn_sample(q_s, k_new, z, cache_kt, cache_vh, page_table, diff_lambda[i],
                              diff_subln[i], a2, ts=ts, layer=i, row0=n_p, lam_init=lam_init)
            w_out = even_w_out[i].astype(BF)
            x = _outproj(x, a1, a2, w_out[:POOL_WIDTH], w_out[POOL_WIDTH:])
            zs = z[n_p:].reshape(nbs, ts, -1)
            pool_p.append(jnp.stack([z[(b + 1) * t - POOL_HIST:(b + 1) * t, 3 * DA_WIDTH:]
                                     for b in range(nb)], axis=0))
            k_s.append(k_new.reshape(nbs, ts, DA_WIDTH))
            v_s.append(zs[:, :, 2 * DA_WIDTH:3 * DA_WIDTH])
            pool_s.append(jnp.concatenate([state_pool[i], zs[:, :, 3 * DA_WIDTH:]], axis=1)[:, -POOL_HIST:])
        else:
            w_in = jnp.pad(odd_w_in[i], ((0, 0), (0, GLA_RANK_PAD - GLA_GATE_RANK))).astype(BF)
            z = _inproj(x, mix_norm[l], w_in)
            w2 = jnp.pad(gla_gate_w2[i], ((0, GLA_RANK_PAD - GLA_GATE_RANK), (0, 0))).astype(BF)
            a1 = jnp.zeros((n, SGU_WIDTH), BF)
            a1, vr_p = _sgu_prompt(z, sgu_norm[i], sgu_w[i], sgu_b[i], a1, nb=nb, t=t)
            a1, vr_s = _sgu_sample(z, sgu_norm[i], sgu_w[i], sgu_b[i], a1, nb=nbs, ts=ts, row0=n_p)
            a2 = jnp.zeros((n, GLA_V_WIDTH), BF)
            a2, s_p = _gla_prompt(z, w2, gla_gate_b[i], gla_norm[i], a2, nb=nb, t=t)
            a2, s_s = _gla_sample(z, w2, gla_gate_b[i], gla_norm[i], state_gla[i], a2, nb=nbs, ts=ts,
                                  row0=n_p)
            w_out = odd_w_out[i].astype(BF)
            x = _outproj(x, a1, a2, w_out[:SGU_WIDTH], w_out[SGU_WIDTH:])
            sgu_p.append(vr_p)
            gla_p.append(s_p)
            sgu_s.append(vr_s.reshape(nbs, ts, SGU_WIDTH))
            gla_s.append(s_s)
        x = _ffn(x, ffn2_norm[l], ffn2_w_gate[l].astype(BF), ffn2_w_up[l].astype(BF),
                 ffn2_w_down[l].astype(BF))
    y_p = _final_norm(x, final_norm, row0=0, nrows=n_p)
    y_s = _final_norm(x, final_norm, row0=n_p, nrows=n_s)
    k_rows_p = jnp.transpose(kt_all.reshape(nb, n_even, DA_HEADS, 2, DA_HEAD_DIM, t), (0, 1, 5, 2, 3, 4))
    v_rows_p = jnp.transpose(vh_all, (0, 1, 3, 2, 4))

    return (y_p.reshape(nb, t, d), y_s.reshape(nbs, ts, d), k_rows_p, v_rows_p,
            jnp.stack(pool_p, axis=0), jnp.stack(sgu_p, axis=0), jnp.stack(gla_p, axis=0),
            jnp.stack(k_s, axis=1).reshape(nbs, -1, ts, DA_HEADS, 2, DA_HEAD_DIM),
            jnp.stack(v_s, axis=1).reshape(nbs, -1, ts, DA_HEADS, DA_PAIR),
            jnp.stack(pool_s, axis=0), jnp.stack(sgu_s, axis=0), jnp.stack(gla_s, axis=0))
```

```python
import functools
import math

import numpy as np
import jax
import jax.numpy as jnp
from jax import lax
from jax.experimental import pallas as pl
from jax.experimental.pallas import tpu as pltpu

F32 = jnp.float32
BF = jnp.bfloat16

RMS_EPS = 1e-6
ROPE_THETA = 10000.0
PAGE_SIZE = 128

POOL_WINDOWS = (2, 4, 8, 16)
POOL_GROUP_DIM = 64
POOL_WIDTH = 256
POOL_HIST = 15
POOL_PAD = 16

DA_HEADS = 6
DA_HEAD_DIM = 64
DA_PAIR = 2 * DA_HEAD_DIM
DA_WIDTH = DA_HEADS * DA_PAIR
DA_VT_ROWS = DA_PAIR + 16
LOG2E = 1.4426950408889634
DA_SAFE_LOG2 = 60.0

SGU_GROUPS = 4
SGU_GROUP_DIM = 128
SGU_WIDTH = 512
SGU_CHUNK = 128

GLA_HEADS = 4
GLA_K_DIM = 64
GLA_V_DIM = 128
GLA_K_WIDTH = 256
GLA_V_WIDTH = 512
GLA_GATE_RANK = 16
GLA_GATE_NORMALIZER = 16.0
GLA_CHUNK = 64
GLA_RANK_PAD = 128

LANES = 128
VMEM_LIMIT = 56 * 1024 * 1024


def _cparams(*sem):
    return pltpu.CompilerParams(dimension_semantics=sem, vmem_limit_bytes=VMEM_LIMIT)


def _pick_tile(n, cap, mult=8):
    best = None
    for t in range(mult, min(n, cap) + 1, mult):
        if n % t == 0:
            best = t
    assert best is not None, (n, cap, mult)
    return best


def _shr(x, pow2):
    assert pow2 & (pow2 - 1) == 0
    return x >> (pow2.bit_length() - 1)


def _block_diag(w):
    g, a, b = w.shape
    out = jnp.zeros((g * a, g * b), w.dtype)
    for i in range(g):
        out = out.at[i * a:(i + 1) * a, i * b:(i + 1) * b].set(w[i])
    return out


ANY_SPEC = pl.BlockSpec(memory_space=pl.ANY)


def _rms(x, axis):
    return x * lax.rsqrt(jnp.mean(x * x, axis=axis, keepdims=True) + RMS_EPS)


def _ffn_kernel(x_ref, g_ref, wg_ref, wu_ref, wd_ref, o_ref, hn_ref, acc_ref, *, nf):
    f = pl.program_id(1)

    @pl.when(f == 0)
    def _():
        hn_ref[...] = (_rms(x_ref[...], -1) * g_ref[...]).astype(BF)
        acc_ref[...] = jnp.zeros_like(acc_ref)

    hn = hn_ref[...]
    a = jnp.dot(hn, wg_ref[...].astype(BF), preferred_element_type=F32)
    u = jnp.dot(hn, wu_ref[...].astype(BF), preferred_element_type=F32)
    h = (a * jax.nn.sigmoid(a) * u).astype(BF)
    acc_ref[...] += jnp.dot(h, wd_ref[...].astype(BF), preferred_element_type=F32)

    @pl.when(f == nf - 1)
    def _():
        o_ref[...] = x_ref[...] + 0.5 * acc_ref[...]


def _ffn(x, g, wg, wu, wd):
    n, d = x.shape
    ff = wg.shape[1]
    tm = _pick_tile(n, 1280)
    tf = _pick_tile(ff, 256, LANES)
    nf = ff // tf
    return pl.pallas_call(
        functools.partial(_ffn_kernel, nf=nf),
        grid=(n // tm, nf),
        in_specs=[
            pl.BlockSpec((tm, d), lambda m, f: (m, 0)),
            pl.BlockSpec((1, d), lambda m, f: (0, 0)),
            pl.BlockSpec((d, tf), lambda m, f: (0, f)),
            pl.BlockSpec((d, tf), lambda m, f: (0, f)),
            pl.BlockSpec((tf, d), lambda m, f: (f, 0)),
        ],
        out_specs=pl.BlockSpec((tm, d), lambda m, f: (m, 0)),
        out_shape=jax.ShapeDtypeStruct((n, d), F32),
        scratch_shapes=[pltpu.VMEM((tm, d), BF), pltpu.VMEM((tm, d), F32)],
        compiler_params=_cparams("parallel", "arbitrary"),
        name="ffn",
    )(x, g.reshape(1, d), wg, wu, wd)


def _inproj_kernel(x_ref, g_ref, w_ref, o_ref, hn_ref):
    @pl.when(pl.program_id(1) == 0)
    def _():
        hn_ref[...] = (_rms(x_ref[...], -1) * g_ref[...]).astype(BF)

    o_ref[...] = jnp.dot(hn_ref[...], w_ref[...], preferred_element_type=F32)


def _inproj(x, g, w):
    n, d = x.shape
    nout = w.shape[1]
    tm = _pick_tile(n, 1280)
    tn = _pick_tile(nout, 512, LANES)
    return pl.pallas_call(
        _inproj_kernel,
        grid=(n // tm, nout // tn),
        in_specs=[
            pl.BlockSpec((tm, d), lambda m, j: (m, 0)),
            pl.BlockSpec((1, d), lambda m, j: (0, 0)),
            pl.BlockSpec((d, tn), lambda m, j: (0, j)),
        ],
        out_specs=pl.BlockSpec((tm, tn), lambda m, j: (m, j)),
        out_shape=jax.ShapeDtypeStruct((n, nout), F32),
        scratch_shapes=[pltpu.VMEM((tm, d), BF)],
        compiler_params=_cparams("parallel", "arbitrary"),
        name="inproj",
    )(x, g.reshape(1, d), w)


def _outproj_kernel(x_ref, a1_ref, a2_ref, w1_ref, w2_ref, o_ref):
    y = jnp.dot(a1_ref[...], w1_ref[...], preferred_element_type=F32)
    y += jnp.dot(a2_ref[...], w2_ref[...], preferred_element_type=F32)
    o_ref[...] = x_ref[...] + y


def _outproj(x, a1, a2, w1, w2):
    n, d = x.shape
    k1, k2 = a1.shape[1], a2.shape[1]
    tm = _pick_tile(n, 1280)
    return pl.pallas_call(
        _outproj_kernel,
        grid=(n // tm,),
        in_specs=[
            pl.BlockSpec((tm, d), lambda m: (m, 0)),
            pl.BlockSpec((tm, k1), lambda m: (m, 0)),
            pl.BlockSpec((tm, k2), lambda m: (m, 0)),
            pl.BlockSpec((k1, d), lambda m: (0, 0)),
            pl.BlockSpec((k2, d), lambda m: (0, 0)),
        ],
        out_specs=pl.BlockSpec((tm, d), lambda m: (m, 0)),
        out_shape=jax.ShapeDtypeStruct((n, d), F32),
        compiler_params=_cparams("parallel"),
        name="outproj",
    )(x, a1, a2, w1, w2)


def _final_norm_kernel(x_ref, g_ref, o_ref):
    o_ref[...] = _rms(x_ref[...], -1) * g_ref[...]


def _final_norm(x, g, *, row0, nrows):
    d = x.shape[1]
    tm = _pick_tile(math.gcd(nrows, row0) if row0 else nrows, 1024)
    rb = row0 // tm
    return pl.pallas_call(
        _final_norm_kernel,
        grid=(nrows // tm,),
        in_specs=[pl.BlockSpec((tm, d), lambda m: (rb + m, 0)), pl.BlockSpec((1, d), lambda m: (0, 0))],
        out_specs=pl.BlockSpec((tm, d), lambda m: (m, 0)),
        out_shape=jax.ShapeDtypeStruct((nrows, d), F32),
        compiler_params=_cparams("parallel"),
        name="final_norm",
    )(x, g.reshape(1, d))


def _pool_kernel(pc_ref, pp_ref, w_ref, sc_ref, dst_ref, o_ref, *, tm, pos0, prev_at_first):
    del dst_ref
    i = pl.program_id(1)
    p = pc_ref[...]
    prev = pp_ref[...]
    if not prev_at_first:
        prev = jnp.where(i > 0, prev, 0.0)
    ext = jnp.concatenate([prev, p], axis=0)
    s2 = ext + pltpu.roll(ext, 1, 0)
    s4 = s2 + pltpu.roll(s2, 2, 0)
    s8 = s4 + pltpu.roll(s4, 4, 0)
    s16 = s8 + pltpu.roll(s8, 8, 0)
    grp = _shr(lax.broadcasted_iota(jnp.int32, (tm, POOL_WIDTH), 1), POOL_GROUP_DIM)
    pos = pos0 + i * tm + lax.broadcasted_iota(jnp.int32, (tm, POOL_WIDTH), 0)
    sums = (s2, s4, s8, s16)
    s = sums[3][POOL_PAD:]
    win = jnp.full((tm, POOL_WIDTH), POOL_WINDOWS[3], jnp.int32)
    for gi in (2, 1, 0):
        s = jnp.where(grp == gi, sums[gi][POOL_PAD:], s)
        win = jnp.where(grp == gi, POOL_WINDOWS[gi], win)
    cnt = jnp.minimum(pos + 1, win).astype(F32)
    diff = s / cnt - p
    y = jnp.dot(diff.astype(BF), w_ref[...], preferred_element_type=F32) * sc_ref[...]
    o_ref[...] = y.astype(BF)


def _pool(z, prev_src, w_bd, scale, dst, *, nb, t, row0, col_blk, prev_is_state, pos0):
    tm = _pick_tile(t, 512)
    nt = t // tm
    rb0 = row0 // tm
    if prev_is_state:
        assert nt == 1
        prev_spec = pl.BlockSpec((POOL_PAD, POOL_WIDTH), lambda b, i: (b, 0))
    else:
        r16 = tm // POOL_PAD
        base16 = row0 // POOL_PAD
        prev_spec = pl.BlockSpec(
            (POOL_PAD, POOL_WIDTH),
            lambda b, i: (jnp.maximum(base16 + (b * nt + i) * r16 - 1, 0), col_blk))
    return pl.pallas_call(
        functools.partial(_pool_kernel, tm=tm, pos0=pos0, prev_at_first=prev_is_state),
        grid=(nb, nt),
        in_specs=[
            pl.BlockSpec((tm, POOL_WIDTH), lambda b, i: (rb0 + b * nt + i, col_blk)),
            prev_spec,
            pl.BlockSpec((POOL_WIDTH, POOL_WIDTH), lambda b, i: (0, 0)),
            pl.BlockSpec((1, POOL_WIDTH), lambda b, i: (0, 0)),
            ANY_SPEC,
        ],
        out_specs=pl.BlockSpec((tm, POOL_WIDTH), lambda b, i: (rb0 + b * nt + i, 0)),
        out_shape=jax.ShapeDtypeStruct(dst.shape, dst.dtype),
        input_output_aliases={4: 0},
        compiler_params=_cparams("parallel", "arbitrary"),
        name="pool",
    )(z, prev_src, w_bd, scale.reshape(1, POOL_WIDTH), dst)


def _rope_tables(pos):
    half = DA_HEAD_DIM // 2
    inv = ROPE_THETA ** (-jnp.arange(half, dtype=F32) / half)
    ang = pos.astype(F32)[:, None] * inv[None, :]
    cos, sin = jnp.cos(ang), jnp.sin(ang)
    cos128 = jnp.concatenate([cos, cos, cos, cos], axis=1)
    sin128 = jnp.concatenate([-sin, sin, -sin, sin], axis=1)
    return cos128, sin128


def _rope128(x, cos, sin):
    lane = lax.broadcasted_iota(jnp.int32, x.shape, 1)
    first = (lane & (DA_HEAD_DIM - 1)) < (DA_HEAD_DIM // 2)
    partner = jnp.where(first, pltpu.roll(x, LANES - 32, 1), pltpu.roll(x, 32, 1))
    return x * cos + partner * sin


def _prep_prompt_kernel(zq_ref, zk_ref, zv_ref, cos_ref, sin_ref, kt_dst_ref, vh_dst_ref,
                        qt_ref, kb_ref, vt_ref, kn_ref, kt_ref, vh_ref):
    del kt_dst_ref, vh_dst_ref
    cos, sin = cos_ref[...], sin_ref[...]
    scale = DA_HEAD_DIM ** -0.5 * LOG2E
    ones = jnp.ones((DA_VT_ROWS - DA_PAIR, zq_ref.shape[0]), BF)
    for h in range(DA_HEADS):
        sl = slice(h * DA_PAIR, (h + 1) * DA_PAIR)
        q = _rope128(zq_ref[:, sl], cos, sin) * scale
        qt_ref[h] = q.T.astype(BF)
        k = _rope128(zk_ref[:, sl], cos, sin)
        kt_ref[h] = k.T
        kb = k.astype(BF)
        kb_ref[h] = kb
        kf = kb.astype(F32)
        kn2 = jnp.max(jnp.sum(kf * kf, axis=1, keepdims=True), axis=0, keepdims=True)
        kn_ref[h] = jnp.broadcast_to(kn2, kn_ref.shape[1:])
        v = zv_ref[:, sl]
        vh_ref[h] = v
        vt_ref[h, :DA_PAIR] = v.T.astype(BF)
        vt_ref[h, DA_PAIR:] = ones


def _prep_prompt(z, cos, sin, kt_all, vh_all, *, layer, nb, t, tk):
    nt = t // tk
    return pl.pallas_call(
        _prep_prompt_kernel,
        grid=(nb, nt),
        in_specs=[
            pl.BlockSpec((tk, DA_WIDTH), lambda b, i: (b * nt + i, 0)),
            pl.BlockSpec((tk, DA_WIDTH), lambda b, i: (b * nt + i, 1)),
            pl.BlockSpec((tk, DA_WIDTH), lambda b, i: (b * nt + i, 2)),
            pl.BlockSpec((tk, LANES), lambda b, i: (i, 0)),
            pl.BlockSpec((tk, LANES), lambda b, i: (i, 0)),
            ANY_SPEC,
            ANY_SPEC,
        ],
        out_specs=[
            pl.BlockSpec((None, DA_HEADS, DA_PAIR, tk), lambda b, i: (b, 0, 0, i)),
            pl.BlockSpec((None, DA_HEADS, tk, DA_PAIR), lambda b, i: (b, 0, i, 0)),
            pl.BlockSpec((None, DA_HEADS, None, DA_VT_ROWS, tk), lambda b, i: (b, 0, i, 0, 0)),
            pl.BlockSpec((None, DA_HEADS, None, 8, LANES), lambda b, i: (b, 0, i, 0, 0)),
            pl.BlockSpec((None, None, DA_HEADS, DA_PAIR, tk), lambda b, i: (b, layer, 0, 0, i)),
            pl.BlockSpec((None, None, DA_HEADS, tk, DA_PAIR), lambda b, i: (b, layer, 0, i, 0)),
        ],
        out_shape=[
            jax.ShapeDtypeStruct((nb, DA_HEADS, DA_PAIR, t), BF),
            jax.ShapeDtypeStruct((nb, DA_HEADS, t, DA_PAIR), BF),
            jax.ShapeDtypeStruct((nb, DA_HEADS, nt, DA_VT_ROWS, tk), BF),
            jax.ShapeDtypeStruct((nb, DA_HEADS, nt, 8, LANES), F32),
            jax.ShapeDtypeStruct(kt_all.shape, kt_all.dtype),
            jax.ShapeDtypeStruct(vh_all.shape, vh_all.dtype),
        ],
        input_output_aliases={5: 4, 6: 5},
        compiler_params=_cparams("parallel", "parallel"),
        name="prep_prompt",
    )(z, z, z, cos, sin, kt_all, vh_all)


def _lambda_value(lamv_ref, lam_init):
    lv = lamv_ref[...]
    s01 = jnp.sum(lv[0:1] * lv[1:2], axis=1, keepdims=True)
    s23 = jnp.sum(lv[2:3] * lv[3:4], axis=1, keepdims=True)
    return jnp.exp(s01) - jnp.exp(s23) + lam_init


def _attn_prompt_kernel(qt_ref, k_ref, vt_ref, kn_ref, lamv_ref, g_ref, dst_ref, o_ref,
                        qx_ref, sa_ref, sb_ref, m_ref, acc_ref, *, tq, lam_init):
    del dst_ref
    qi = pl.program_id(2)
    qt = qt_ref[...]
    row = lax.broadcasted_iota(jnp.int32, qt.shape, 0)
    zero = jnp.zeros_like(qt)
    qx_ref[:, :tq] = jnp.where(row < DA_HEAD_DIM, qt, zero)
    qx_ref[:, tq:] = jnp.where(row >= DA_HEAD_DIM, qt, zero)
    qf = qt.astype(F32)
    qn2 = jnp.max(jnp.sum(qf * qf, axis=0, keepdims=True))
    bounded = qn2 * jnp.max(kn_ref[...]) < DA_SAFE_LOG2 ** 2

    def keys(j):
        return k_ref[pl.ds(pl.multiple_of(j * tq, tq), tq), :]

    def diagonal_mask(shape):
        kpos = lax.broadcasted_iota(jnp.int32, shape, 0)
        qpos = lax.broadcasted_iota(jnp.int32, shape, 1) & (tq - 1)
        return kpos <= qpos

    def plain_chunk(j, diagonal, first):
        p = jnp.exp2(jnp.dot(keys(j), qx_ref[...], preferred_element_type=F32))
        if diagonal:
            p = jnp.where(diagonal_mask(p.shape), p, 0.0)
        d = jnp.dot(vt_ref[j], p.astype(BF), preferred_element_type=F32)
        acc_ref[...] = d if first else acc_ref[...] + d

    @pl.when(bounded)
    def _():
        plain_chunk(qi, True, True)

        @pl.when((qi & 1) == 1)
        def _():
            plain_chunk(0, False, False)

        def pair(pi, carry):
            j = (qi & 1) + 2 * pi
            plain_chunk(j, False, False)
            plain_chunk(j + 1, False, False)
            return carry

        lax.fori_loop(0, qi >> 1, pair, 0)

    def scores(j):
        return jnp.dot(keys(j), qx_ref[...], preferred_element_type=F32)

    def update(s, j):
        m_prev = m_ref[...]
        m_new = jnp.maximum(m_prev, jnp.max(s, axis=0, keepdims=True))
        p = jnp.exp2(s - m_new).astype(BF)
        acc_ref[...] = jnp.exp2(m_prev - m_new) * acc_ref[...] + jnp.dot(
            vt_ref[j], p, preferred_element_type=F32)
        m_ref[...] = m_new

    @pl.when(jnp.logical_not(bounded))
    def _():
        s = scores(qi)
        s = jnp.where(diagonal_mask(s.shape), s, -jnp.inf)
        m0 = jnp.max(s, axis=0, keepdims=True)
        m_ref[...] = m0
        acc_ref[...] = jnp.dot(vt_ref[qi], jnp.exp2(s - m0).astype(BF), preferred_element_type=F32)
        first = qi & 1

        @pl.when(first == 1)
        def _():
            update(scores(0), 0)

        @pl.when(qi >= 2)
        def _():
            sa_ref[...] = scores(first)

        def pair(pi, carry):
            j = first + 2 * pi
            sb_ref[...] = scores(j + 1)
            update(sa_ref[...], j)
            sa_ref[...] = scores(jnp.minimum(j + 2, qi - 1))
            update(sb_ref[...], j + 1)
            return carry

        lax.fori_loop(0, qi >> 1, pair, 0)

    acc = acc_ref[...]
    on = acc[:DA_PAIR] / acc[DA_PAIR:DA_PAIR + 1]
    lam = _lambda_value(lamv_ref, lam_init)
    ot = on[:, :tq] - lam * on[:, tq:]
    ot = _rms(ot, 0) * g_ref[...] * (1.0 - lam_init)
    o_ref[...] = ot.T.astype(BF)


def _attn_prompt(qt, kb, vt, kn2, lamv, subln_g, dst, *, tq, lam_init):
    nb, nh, _, t = qt.shape
    nk = t // tq
    return pl.pallas_call(
        functools.partial(_attn_prompt_kernel, tq=tq, lam_init=lam_init),
        grid=(nb, nh, t // tq),
        in_specs=[
            pl.BlockSpec((None, None, DA_PAIR, tq), lambda b, h, i: (b, h, 0, i)),
            pl.BlockSpec((None, None, t, DA_PAIR), lambda b, h, i: (b, h, 0, 0)),
            pl.BlockSpec((None, None, nk, DA_VT_ROWS, tq), lambda b, h, i: (b, h, 0, 0, 0)),
            pl.BlockSpec((None, None, nk, 8, LANES), lambda b, h, i: (b, h, 0, 0, 0)),
            pl.BlockSpec((4, DA_HEAD_DIM), lambda b, h, i: (0, 0)),
            pl.BlockSpec((DA_PAIR, 1), lambda b, h, i: (0, 0)),
            ANY_SPEC,
        ],
        out_specs=pl.BlockSpec((tq, DA_PAIR), lambda b, h, i: (b * nk + i, h)),
        out_shape=jax.ShapeDtypeStruct(dst.shape, dst.dtype),
        scratch_shapes=[pltpu.VMEM((DA_PAIR, 2 * tq), BF),
                        pltpu.VMEM((tq, 2 * tq), F32), pltpu.VMEM((tq, 2 * tq), F32),
                        pltpu.VMEM((1, 2 * tq), F32), pltpu.VMEM((DA_VT_ROWS, 2 * tq), F32)],
        input_output_aliases={6: 0},
        compiler_params=_cparams("parallel", "parallel", "arbitrary"),
        name="attn_prompt",
    )(qt, kb, vt, kn2, lamv, subln_g.reshape(DA_PAIR, 1), dst)


def _rope_rows_kernel(zq_ref, zk_ref, cos_ref, sin_ref, q_ref, k_ref):
    cos, sin = cos_ref[...], sin_ref[...]
    scale = DA_HEAD_DIM ** -0.5
    for h in range(DA_HEADS):
        sl = slice(h * DA_PAIR, (h + 1) * DA_PAIR)
        q_ref[:, sl] = (_rope128(zq_ref[:, sl], cos, sin) * scale).astype(BF)
        k_ref[:, sl] = _rope128(zk_ref[:, sl], cos, sin)


def _rope_rows(z, cos, sin, *, row0, nrows):
    rb = row0 // nrows
    return pl.pallas_call(
        _rope_rows_kernel,
        grid=(1,),
        in_specs=[
            pl.BlockSpec((nrows, DA_WIDTH), lambda i: (rb, 0)),
            pl.BlockSpec((nrows, DA_WIDTH), lambda i: (rb, 1)),
            pl.BlockSpec((nrows, LANES), lambda i: (0, 0)),
            pl.BlockSpec((nrows, LANES), lambda i: (0, 0)),
        ],
        out_specs=[pl.BlockSpec((nrows, DA_WIDTH), lambda i: (0, 0)),
                   pl.BlockSpec((nrows, DA_WIDTH), lambda i: (0, 0))],
        out_shape=[jax.ShapeDtypeStruct((nrows, DA_WIDTH), BF),
                   jax.ShapeDtypeStruct((nrows, DA_WIDTH), F32)],
        compiler_params=_cparams("arbitrary"),
        name="rope_rows",
    )(z, z, cos, sin)


def _attn_sample_kernel(pt_ref, q_ref, kn_ref, vn_ref, sel_ref, lamv_ref, g_ref, *rest,
                        pp, nsteps, ts, lam_init):
    kpages, vpages = rest[:pp], rest[pp:2 * pp]
    _, o_ref, m_ref, l_ref, acc_ref, qbd_ref = rest[2 * pp:]
    s = pl.program_id(1)
    nrow = 2 * DA_HEADS * ts

    @pl.when(s == 0)
    def _():
        q = q_ref[...]
        qrep = jnp.concatenate([q] * (2 * DA_HEADS), axis=0)
        qbd_ref[...] = jnp.where(sel_ref[...] > 0, qrep, jnp.zeros_like(qrep))
        m_ref[...] = jnp.full(m_ref.shape, -jnp.inf, F32)
        l_ref[...] = jnp.zeros_like(l_ref)
        acc_ref[...] = jnp.zeros_like(acc_ref)

    def update(sc, vv, causal):
        if causal:
            tq = lax.broadcasted_iota(jnp.int32, sc.shape, 0) & (ts - 1)
            tk = lax.broadcasted_iota(jnp.int32, sc.shape, 1)
            sc = jnp.where(tk <= tq, sc, -jnp.inf)
        m_prev = m_ref[...]
        m_new = jnp.maximum(m_prev, jnp.max(sc, axis=1, keepdims=True))
        alpha = jnp.exp(m_prev - m_new)
        p = jnp.exp(sc - m_new)
        l_ref[...] = alpha * l_ref[...] + jnp.sum(p, axis=1, keepdims=True)
        acc_ref[...] = alpha * acc_ref[...] + jnp.dot(p.astype(BF), vv, preferred_element_type=F32)
        m_ref[...] = m_new

    if pp:
        kt = jnp.concatenate([r[...] for r in kpages], axis=1).astype(BF)
        vv = jnp.concatenate(
            [jnp.concatenate([r[h] for h in range(DA_HEADS)], axis=1) for r in vpages],
            axis=0).astype(BF)
        update(jnp.dot(qbd_ref[...], kt, preferred_element_type=F32), vv, False)

    @pl.when(s == nsteps - 1)
    def _():
        sc_new = lax.dot_general(qbd_ref[...], kn_ref[...].astype(BF), (((1,), (1,)), ((), ())),
                                 preferred_element_type=F32)
        update(sc_new, vn_ref[...].astype(BF), True)
        on = acc_ref[...] / l_ref[...]
        lam = _lambda_value(lamv_ref, lam_init)
        half = nrow // 2
        d = on[:half] - lam * on[half:]
        outs = []
        for h in range(DA_HEADS):
            blk = d[h * ts:(h + 1) * ts, h * DA_PAIR:(h + 1) * DA_PAIR]
            outs.append(_rms(blk, -1) * g_ref[...] * (1.0 - lam_init))
        o_ref[...] = jnp.concatenate(outs, axis=1).astype(BF)


def _attn_sample(q_s, k_new, z, cache_kt, cache_vh, page_table, lamv, subln_g, dst, *,
                 ts, layer, row0, lam_init):
    nb = q_s.shape[0] // ts
    n_pages = page_table.shape[1]
    pp = 0
    for cand in (8, 4, 2, 1):
        if n_pages and n_pages % cand == 0:
            pp = cand
            break
    nsteps = max(n_pages // pp, 1) if pp else 1
    nrow = 2 * DA_HEADS * ts
    r = np.arange(nrow)[:, None] // ts
    c = np.arange(DA_WIDTH)[None, :] // DA_HEAD_DIM
    sel = jnp.asarray(((r % DA_HEADS) * 2 + r // DA_HEADS == c).astype(np.float32))
    rb = row0 // ts

    def kpage_spec(i):
        return pl.BlockSpec((None, None, DA_WIDTH, PAGE_SIZE),
                            lambda b, s, pt: (pt[b * n_pages + s * pp + i], layer, 0, 0))

    def vpage_spec(i):
        return pl.BlockSpec((None, None, DA_HEADS, PAGE_SIZE, DA_PAIR),
                            lambda b, s, pt: (pt[b * n_pages + s * pp + i], layer, 0, 0, 0))

    in_specs = [
        pl.BlockSpec((ts, DA_WIDTH), lambda b, s, pt: (b, 0)),
        pl.BlockSpec((ts, DA_WIDTH), lambda b, s, pt: (b, 0)),
        pl.BlockSpec((ts, DA_WIDTH), lambda b, s, pt: (rb + b, 2)),
        pl.BlockSpec((nrow, DA_WIDTH), lambda b, s, pt: (0, 0)),
        pl.BlockSpec((4, DA_HEAD_DIM), lambda b, s, pt: (0, 0)),
        pl.BlockSpec((1, DA_PAIR), lambda b, s, pt: (0, 0)),
    ] + [kpage_spec(i) for i in range(pp)] + [vpage_spec(i) for i in range(pp)] + [ANY_SPEC]
    grid_spec = pltpu.PrefetchScalarGridSpec(
        num_scalar_prefetch=1,
        grid=(nb, nsteps),
        in_specs=in_specs,
        out_specs=pl.BlockSpec((ts, DA_WIDTH), lambda b, s, pt: (rb + b, 0)),
        scratch_shapes=[pltpu.VMEM((nrow, 1), F32), pltpu.VMEM((nrow, 1), F32),
                        pltpu.VMEM((nrow, DA_WIDTH), F32), pltpu.VMEM((nrow, DA_WIDTH), BF)],
    )
    return pl.pallas_call(
        functools.partial(_attn_sample_kernel, pp=pp, nsteps=nsteps, ts=ts, lam_init=lam_init),
        grid_spec=grid_spec,
        out_shape=jax.ShapeDtypeStruct(dst.shape, dst.dtype),
        input_output_aliases={7 + 2 * pp: 0},
        compiler_params=_cparams("parallel", "arbitrary"),
        name="attn_sample",
    )(page_table.reshape(-1), q_s, k_new, z, sel, lamv, subln_g.reshape(1, DA_PAIR),
      *([cache_kt] * pp), *([cache_vh] * pp), dst)


def _sgu_norm_v(sv_ref, ng_ref, g):
    x = jax.nn.gelu(sv_ref[:, g * SGU_GROUP_DIM:(g + 1) * SGU_GROUP_DIM])
    return _rms(x, -1) * ng_ref[g:g + 1, :]


def _sgu_prompt_kernel(u_ref, sv_ref, ng_ref, w_ref, bs_ref, dst_ref, y_ref, vr_ref, *, tm, c):
    del dst_ref
    last = pl.program_id(1) == pl.num_programs(1) - 1
    ri = lax.broadcasted_iota(jnp.int32, (c, c), 0)
    ci = lax.broadcasted_iota(jnp.int32, (c, c), 1)
    for g in range(SGU_GROUPS):
        sl = slice(g * SGU_GROUP_DIM, (g + 1) * SGU_GROUP_DIM)
        v = _sgu_norm_v(sv_ref, ng_ref, g)

        @pl.when(last)
        def _():
            vr_ref[:, sl] = v[tm - c:]

        vb = v.astype(BF)
        w = jnp.where(ri >= ci, w_ref[g], 0.0).astype(BF)
        for n in range(tm // c):
            rows = slice(n * c, (n + 1) * c)
            mixed = jnp.dot(w, vb[rows], preferred_element_type=F32) + bs_ref[g]
            y_ref[rows, sl] = (jax.nn.gelu(u_ref[rows, sl]) * mixed).astype(BF)


def _sgu_prompt(z, norm_g, w_s, b_s, dst, *, nb, t):
    c = SGU_CHUNK
    tm = _pick_tile(t, 512, c)
    nt = t // tm
    bs = jnp.broadcast_to(b_s[:, :c, None], (SGU_GROUPS, c, SGU_GROUP_DIM))
    return pl.pallas_call(
        functools.partial(_sgu_prompt_kernel, tm=tm, c=c),
        grid=(nb, nt),
        in_specs=[
            pl.BlockSpec((tm, SGU_WIDTH), lambda b, i: (b * nt + i, 0)),
            pl.BlockSpec((tm, SGU_WIDTH), lambda b, i: (b * nt + i, 1)),
            pl.BlockSpec((SGU_GROUPS, SGU_GROUP_DIM), lambda b, i: (0, 0)),
            pl.BlockSpec((SGU_GROUPS, c, c), lambda b, i: (0, 0, 0)),
            pl.BlockSpec((SGU_GROUPS, c, SGU_GROUP_DIM), lambda b, i: (0, 0, 0)),
            ANY_SPEC,
        ],
        out_specs=[
            pl.BlockSpec((tm, SGU_WIDTH), lambda b, i: (b * nt + i, 0)),
            pl.BlockSpec((None, c, SGU_WIDTH), lambda b, i: (b, 0, 0)),
        ],
        out_shape=[jax.ShapeDtypeStruct(dst.shape, dst.dtype),
                   jax.ShapeDtypeStruct((nb, c, SGU_WIDTH), F32)],
        input_output_aliases={5: 0},
        compiler_params=_cparams("parallel", "arbitrary"),
        name="sgu_prompt",
    )(z, z, norm_g, w_s[:, :c, :c], bs, dst)


def _sgu_sample_kernel(u_ref, sv_ref, ng_ref, w_ref, bs_ref, dst_ref, y_ref, vr_ref, *, nb, ts):
    del dst_ref
    v = jnp.concatenate([_sgu_norm_v(sv_ref, ng_ref, g) for g in range(SGU_GROUPS)], axis=1)
    vr_ref[...] = v
    v3 = v.reshape(nb, ts, SGU_WIDTH)
    ii = lax.broadcasted_iota(jnp.int32, (ts, SGU_WIDTH), 0)
    mixed = jnp.broadcast_to(bs_ref[...][None], (nb, ts, SGU_WIDTH))
    for j in range(ts):
        wj = jnp.where(ii >= j, w_ref[j], 0.0)
        mixed = mixed + wj[None] * v3[:, j:j + 1, :]
    y = jax.nn.gelu(u_ref[...]).reshape(nb, ts, SGU_WIDTH) * mixed
    y_ref[...] = y.reshape(nb * ts, SGU_WIDTH).astype(BF)


def _sgu_sample(z, norm_g, w_s, b_s, dst, *, nb, ts, row0):
    nrows = nb * ts
    rb = row0 // nrows
    w_exp = jnp.repeat(jnp.transpose(w_s[:, :ts, :ts], (2, 1, 0)), SGU_GROUP_DIM, axis=2)
    b_exp = jnp.repeat(jnp.transpose(b_s[:, :ts], (1, 0)), SGU_GROUP_DIM, axis=1)
    return pl.pallas_call(
        functools.partial(_sgu_sample_kernel, nb=nb, ts=ts),
        grid=(1,),
        in_specs=[
            pl.BlockSpec((nrows, SGU_WIDTH), lambda i: (rb, 0)),
            pl.BlockSpec((nrows, SGU_WIDTH), lambda i: (rb, 1)),
            pl.BlockSpec((SGU_GROUPS, SGU_GROUP_DIM), lambda i: (0, 0)),
            pl.BlockSpec((ts, ts, SGU_WIDTH), lambda i: (0, 0, 0)),
            pl.BlockSpec((ts, SGU_WIDTH), lambda i: (0, 0)),
            ANY_SPEC,
        ],
        out_specs=[pl.BlockSpec((nrows, SGU_WIDTH), lambda i: (rb, 0)),
                   pl.BlockSpec((nrows, SGU_WIDTH), lambda i: (0, 0))],
        out_shape=[jax.ShapeDtypeStruct(dst.shape, dst.dtype),
                   jax.ShapeDtypeStruct((nrows, SGU_WIDTH), F32)],
        input_output_aliases={5: 0},
        compiler_params=_cparams("arbitrary"),
        name="sgu_sample",
    )(z, z, norm_g, w_exp, b_exp, dst)


def _log_sigmoid(x):
    return jnp.minimum(x, 0.0) - jnp.log1p(jnp.exp(-jnp.abs(x)))


def _seg_scan(x, seg, pos, reverse):
    n = x.shape[0]
    r = pos & (seg - 1)
    t = 1
    while t < seg:
        if reverse:
            x = x + jnp.where(r < seg - t, pltpu.roll(x, n - t, 0), 0.0)
        else:
            x = x + jnp.where(r >= t, pltpu.roll(x, t, 0), 0.0)
        t *= 2
    return x


def _gla_decays(gk, c):
    pos = lax.broadcasted_iota(jnp.int32, gk.shape, 0)
    out = {1: (jnp.exp(gk), None)}
    s = 2
    while s <= c:
        pre = _seg_scan(gk, s, pos, False)
        suf = _seg_scan(gk, s, pos, True) - gk
        out[s] = (jnp.exp(pre), jnp.exp(suf))
        s *= 2
    return out


def _gla_intra(q, k, v, dec, c, sc):
    n = q.shape[0]
    ri = lax.broadcasted_iota(jnp.int32, (sc, sc), 0)
    ci = lax.broadcasted_iota(jnp.int32, (sc, sc), 1)
    same_chunk = _shr(ri, c) == _shr(ci, c)
    lane = lax.broadcasted_iota(jnp.int32, (sc, LANES), 1)
    levels = []
    levels.append((ri == ci, q.astype(BF), k.astype(BF)))
    s = 1
    while s < c:
        pre, _ = dec[s]
        suf = dec[s][1]
        ks = k if suf is None else k * suf
        mask = same_chunk & (_shr(ri, 2 * s) == _shr(ci, 2 * s)) & ((_shr(ri, s) & 1) == 1) \
            & ((_shr(ci, s) & 1) == 0)
        levels.append((mask, (q * pre).astype(BF), ks.astype(BF)))
        s *= 2
    outs = []
    for h in range(GLA_HEADS):
        pair = slice((h // 2) * LANES, (h // 2 + 1) * LANES)
        own = _shr(lane, GLA_K_DIM) == (h % 2)
        vh = v[:, h * GLA_V_DIM:(h + 1) * GLA_V_DIM].astype(BF)
        rows_out = []
        for b0 in range(0, n, sc):
            rows = slice(b0, b0 + sc)
            a = jnp.zeros((sc, sc), F32)
            for mask, qs, ks in levels:
                qm = jnp.where(own, qs[rows, pair], jnp.zeros((sc, LANES), BF))
                al = lax.dot_general(qm, ks[rows, pair], (((1,), (1,)), ((), ())),
                                     preferred_element_type=F32)
                a = a + jnp.where(mask, al, 0.0)
            rows_out.append(jnp.dot(a.astype(BF), vh[rows], preferred_element_type=F32))
        outs.append(jnp.concatenate(rows_out, axis=0) if len(rows_out) > 1 else rows_out[0])
    return outs


def _gla_gate(glr_ref, w2_ref, gb_ref):
    pre = jnp.dot(glr_ref[...].astype(BF), w2_ref[...], preferred_element_type=F32) + gb_ref[...]
    return _log_sigmoid(pre) / GLA_GATE_NORMALIZER


def _gla_finish(o_heads, g_ref, gn_ref, o_ref):
    for h in range(GLA_HEADS):
        sl = slice(h * GLA_V_DIM, (h + 1) * GLA_V_DIM)
        gate = g_ref[:, sl]
        o_ref[:, sl] = (_rms(o_heads[h], -1) * gn_ref[...] * (gate * jax.nn.sigmoid(gate))).astype(BF)


def _gla_prompt_kernel(q_ref, k_ref, v_ref, g_ref, glr_ref, w2_ref, gb_ref, gn_ref, dst_ref,
                       o_ref, s_out_ref, st_ref, *, tm, c):
    del dst_ref
    i = pl.program_id(1)

    @pl.when(i == 0)
    def _():
        st_ref[...] = jnp.zeros_like(st_ref)

    gk = _gla_gate(glr_ref, w2_ref, gb_ref)
    q = q_ref[...] * (GLA_K_DIM ** -0.5)
    k = k_ref[...]
    v = v_ref[...]
    dec = _gla_decays(gk, c)
    o_heads = _gla_intra(q, k, v, dec, c, min(tm, LANES))
    eb, esuf = dec[c]
    qd = (q * eb).astype(BF)
    kd = (k * esuf).astype(BF)
    lane = lax.broadcasted_iota(jnp.int32, (c, LANES), 1)
    for h in range(GLA_HEADS):
        pair = slice((h // 2) * LANES, (h // 2 + 1) * LANES)
        own = _shr(lane, GLA_K_DIM) == (h % 2)
        vh = v[:, h * GLA_V_DIM:(h + 1) * GLA_V_DIM].astype(BF)
        st = st_ref[h]
        inter = []
        for n in range(tm // c):
            rows = slice(n * c, (n + 1) * c)
            qm = jnp.where(own, qd[rows, pair], jnp.zeros((c, LANES), BF))
            inter.append(lax.dot_general(qm, st.astype(BF), (((1,), (1,)), ((), ())),
                                         preferred_element_type=F32))
            upd = lax.dot_general(vh[rows], kd[rows, pair], (((0,), (0,)), ((), ())),
                                  preferred_element_type=F32)
            st = eb[n * c + c - 1:n * c + c, pair] * st + upd
        st_ref[h] = st
        o_heads[h] = o_heads[h] + jnp.concatenate(inter, axis=0)
    _gla_finish(o_heads, g_ref, gn_ref, o_ref)

    @pl.when(i == pl.num_programs(1) - 1)
    def _():
        for h in range(GLA_HEADS):
            lo = (h % 2) * GLA_K_DIM
            s_out_ref[h] = st_ref[h].T[lo:lo + GLA_K_DIM, :]


def _gla_prompt(z, w2, gate_b, gla_norm, dst, *, nb, t):
    c = GLA_CHUNK
    tm = _pick_tile(t, 256, LANES)
    nt = t // tm
    return pl.pallas_call(
        functools.partial(_gla_prompt_kernel, tm=tm, c=c),
        grid=(nb, nt),
        in_specs=[
            pl.BlockSpec((tm, GLA_K_WIDTH), lambda b, i: (b * nt + i, 4)),
            pl.BlockSpec((tm, GLA_K_WIDTH), lambda b, i: (b * nt + i, 5)),
            pl.BlockSpec((tm, GLA_V_WIDTH), lambda b, i: (b * nt + i, 3)),
            pl.BlockSpec((tm, GLA_V_WIDTH), lambda b, i: (b * nt + i, 4)),
            pl.BlockSpec((tm, GLA_RANK_PAD), lambda b, i: (b * nt + i, 20)),
            pl.BlockSpec((GLA_RANK_PAD, GLA_K_WIDTH), lambda b, i: (0, 0)),
            pl.BlockSpec((1, GLA_K_WIDTH), lambda b, i: (0, 0)),
            pl.BlockSpec((1, GLA_V_DIM), lambda b, i: (0, 0)),
            ANY_SPEC,
        ],
        out_specs=[
            pl.BlockSpec((tm, GLA_V_WIDTH), lambda b, i: (b * nt + i, 0)),
            pl.BlockSpec((None, GLA_HEADS, GLA_K_DIM, GLA_V_DIM), lambda b, i: (b, 0, 0, 0)),
        ],
        out_shape=[jax.ShapeDtypeStruct(dst.shape, dst.dtype),
                   jax.ShapeDtypeStruct((nb, GLA_HEADS, GLA_K_DIM, GLA_V_DIM), F32)],
        scratch_shapes=[pltpu.VMEM((GLA_HEADS, GLA_V_DIM, LANES), F32)],
        input_output_aliases={8: 0},
        compiler_params=_cparams("parallel", "arbitrary"),
        name="gla_prompt",
    )(z, z, z, z, z, w2, gate_b.reshape(1, GLA_K_WIDTH), gla_norm.reshape(1, GLA_V_DIM), dst)


def _gla_sample_kernel(q_ref, k_ref, v_ref, g_ref, glr_ref, w2_ref, gb_ref, gbc_ref, gn_ref, s0_ref,
                       dst_ref, o_ref, s_out_ref, *, ts):
    del dst_ref
    gk = _gla_gate(glr_ref, w2_ref, gb_ref)
    q = q_ref[...] * (GLA_K_DIM ** -0.5)
    k = k_ref[...]
    v = v_ref[...]
    dec = _gla_decays(gk, ts)
    o_heads = _gla_intra(q, k, v, dec, ts, ts)
    eb, esuf = dec[ts]
    qd = (q * eb).astype(BF)
    kd = (k * esuf).astype(BF)
    pre_t = lax.dot_general(w2_ref[...], glr_ref[...].astype(BF), (((0,), (1,)), ((), ())),
                            preferred_element_type=F32) + gbc_ref[...]
    dlast = jnp.exp(jnp.sum(_log_sigmoid(pre_t) / GLA_GATE_NORMALIZER, axis=1, keepdims=True))
    lane = lax.broadcasted_iota(jnp.int32, (ts, LANES), 1)
    for h in range(GLA_HEADS):
        pair = slice((h // 2) * LANES, (h // 2 + 1) * LANES)
        own = _shr(lane, GLA_K_DIM) == (h % 2)
        vh = v[:, h * GLA_V_DIM:(h + 1) * GLA_V_DIM].astype(BF)
        s_pair = s0_ref[pair, :]
        qm = jnp.where(own, qd[:, pair], jnp.zeros((ts, LANES), BF))
        o_heads[h] = o_heads[h] + jnp.dot(qm, s_pair.astype(BF), preferred_element_type=F32)
        upd = lax.dot_general(kd[:, pair], vh, (((0,), (0,)), ((), ())),
                              preferred_element_type=F32)
        hr = slice(h * GLA_K_DIM, (h + 1) * GLA_K_DIM)
        lo = (h % 2) * GLA_K_DIM
        s_out_ref[hr, :] = dlast[hr] * s0_ref[hr, :] + upd[lo:lo + GLA_K_DIM]
    _gla_finish(o_heads, g_ref, gn_ref, o_ref)


def _gla_sample(z, w2, gate_b, gla_norm, s0, dst, *, nb, ts, row0):
    rb = row0 // ts
    s0r = s0.reshape(nb, GLA_HEADS * GLA_K_DIM, GLA_V_DIM)
    o, s_new = pl.pallas_call(
        functools.partial(_gla_sample_kernel, ts=ts),
        grid=(nb,),
        in_specs=[
            pl.BlockSpec((ts, GLA_K_WIDTH), lambda b: (rb + b, 4)),
            pl.BlockSpec((ts, GLA_K_WIDTH), lambda b: (rb + b, 5)),
            pl.BlockSpec((ts, GLA_V_WIDTH), lambda b: (rb + b, 3)),
            pl.BlockSpec((ts, GLA_V_WIDTH), lambda b: (rb + b, 4)),
            pl.BlockSpec((ts, GLA_RANK_PAD), lambda b: (rb + b, 20)),
            pl.BlockSpec((GLA_RANK_PAD, GLA_K_WIDTH), lambda b: (0, 0)),
            pl.BlockSpec((1, GLA_K_WIDTH), lambda b: (0, 0)),
            pl.BlockSpec((GLA_K_WIDTH, 1), lambda b: (0, 0)),
            pl.BlockSpec((1, GLA_V_DIM), lambda b: (0, 0)),
            pl.BlockSpec((None, GLA_HEADS * GLA_K_DIM, GLA_V_DIM), lambda b: (b, 0, 0)),
            ANY_SPEC,
        ],
        out_specs=[
            pl.BlockSpec((ts, GLA_V_WIDTH), lambda b: (rb + b, 0)),
            pl.BlockSpec((None, GLA_HEADS * GLA_K_DIM, GLA_V_DIM), lambda b: (b, 0, 0)),
        ],
        out_shape=[jax.ShapeDtypeStruct(dst.shape, dst.dtype),
                   jax.ShapeDtypeStruct((nb, GLA_HEADS * GLA_K_DIM, GLA_V_DIM), F32)],
        input_output_aliases={10: 0},
        compiler_params=_cparams("parallel"),
        name="gla_sample",
    )(z, z, z, z, z, w2, gate_b.reshape(1, GLA_K_WIDTH), gate_b.reshape(GLA_K_WIDTH, 1),
      gla_norm.reshape(1, GLA_V_DIM), s0r, dst)
    return o, s_new.reshape(nb, GLA_HEADS, GLA_K_DIM, GLA_V_DIM)


def kernel(x_prompt, x_sample, cache_k, cache_v, page_table, state_pool, state_gla, ffn1_norm, ffn1_w_gate, ffn1_w_up, ffn1_w_down, mix_norm, ffn2_norm, ffn2_w_gate, ffn2_w_up, ffn2_w_down, even_w_in, even_w_out, pool_w, pool_scale, diff_lambda, diff_subln, odd_w_in, odd_w_out, sgu_norm, sgu_w, sgu_b, gla_gate_w2, gla_gate_b, gla_norm, final_norm):
    nb, t, d = x_prompt.shape
    nbs, ts, _ = x_sample.shape
    depth = ffn1_norm.shape[0]
    n_p, n_s = nb * t, nbs * ts
    past_len = page_table.shape[1] * PAGE_SIZE
    assert ts < min(SGU_CHUNK, GLA_CHUNK) and ts & (ts - 1) == 0 and n_p % n_s == 0
    assert t % SGU_CHUNK == 0 and t >= POOL_PAD

    n = n_p + n_s
    x = jnp.concatenate([x_prompt.reshape(n_p, d), x_sample.reshape(n_s, d)], axis=0)
    tq = _pick_tile(t, 512, LANES)
    cos_p, sin_p = _rope_tables(jnp.arange(t))
    cos_s, sin_s = _rope_tables(past_len + jnp.arange(ts))
    cos_s, sin_s = jnp.tile(cos_s, (nbs, 1)), jnp.tile(sin_s, (nbs, 1))

    n_even = (depth + 1) // 2
    cache_kt = jnp.transpose(cache_k, (0, 1, 3, 4, 5, 2)).reshape(
        cache_k.shape[0], cache_k.shape[1], DA_WIDTH, PAGE_SIZE)
    cache_vh = jnp.transpose(cache_v, (0, 1, 3, 2, 4))
    kt_all = jnp.zeros((nb, n_even, DA_HEADS, DA_PAIR, t), F32)
    vh_all = jnp.zeros((nb, n_even, DA_HEADS, t, DA_PAIR), F32)
    pool_p, sgu_p, gla_p = [], [], []
    k_s, v_s, pool_s, sgu_s, gla_s = [], [], [], [], []
    for l in range(depth):
        i = l // 2
        x = _ffn(x, ffn1_norm[l], ffn1_w_gate[l], ffn1_w_up[l], ffn1_w_down[l])
        if l % 2 == 0:
            lam_init = 0.8 - 0.6 * math.exp(-0.3 * l)
            w_in = even_w_in[i]
            w_in = jnp.concatenate([w_in[:, POOL_WIDTH:], w_in[:, :POOL_WIDTH]], axis=1).astype(BF)
            z = _inproj(x, mix_norm[l], w_in)
            pcol = 3 * DA_WIDTH // POOL_WIDTH
            w_bd = _block_diag(pool_w[i]).astype(BF)
            a1 = jnp.zeros((n, POOL_WIDTH), BF)
            a1 = _pool(z, z, w_bd, pool_scale[i], a1, nb=nb, t=t, row0=0, col_blk=pcol,
                       prev_is_state=False, pos0=0)
            st_pad = jnp.pad(state_pool[i], ((0, 0), (POOL_PAD - POOL_HIST, 0), (0, 0)))
            a1 = _pool(z, st_pad.reshape(nbs * POOL_PAD, POOL_WIDTH), w_bd, pool_scale[i], a1,
                       nb=nbs, t=ts, row0=n_p, col_blk=pcol, prev_is_state=True, pos0=past_len)
            qt, kb, vt, kn2, kt_all, vh_all = _prep_prompt(z, cos_p, sin_p, kt_all, vh_all, layer=i,
                                                           nb=nb, t=t, tk=tq)
            a2 = jnp.zeros((n, DA_WIDTH), BF)
            a2 = _attn_prompt(qt, kb, vt, kn2, diff_lambda[i], diff_subln[i], a2, tq=tq,
                              lam_init=lam_init)
            q_s, k_new = _rope_rows(z, cos_s, sin_s, row0=n_p, nrows=n_s)
            a2 = _attn_sample(q_s, k_new, z, cache_kt, cache_vh, page_table, diff_lambda[i],
                              diff_subln[i], a2, ts=ts, layer=i, row0=n_p, lam_init=lam_init)
            w_out = even_w_out[i].astype(BF)
            x = _outproj(x, a1, a2, w_out[:POOL_WIDTH], w_out[POOL_WIDTH:])
            zs = z[n_p:].reshape(nbs, ts, -1)
            pool_p.append(jnp.stack([z[(b + 1) * t - POOL_HIST:(b + 1) * t, 3 * DA_WIDTH:]
                                     for b in range(nb)], axis=0))
            k_s.append(k_new.reshape(nbs, ts, DA_WIDTH))
            v_s.append(zs[:, :, 2 * DA_WIDTH:3 * DA_WIDTH])
            pool_s.append(jnp.concatenate([state_pool[i], zs[:, :, 3 * DA_WIDTH:]], axis=1)[:, -POOL_HIST:])
        else:
            w_in = jnp.pad(odd_w_in[i], ((0, 0), (0, GLA_RANK_PAD - GLA_GATE_RANK))).astype(BF)
            z = _inproj(x, mix_norm[l], w_in)
            w2 = jnp.pad(gla_gate_w2[i], ((0, GLA_RANK_PAD - GLA_GATE_RANK), (0, 0))).astype(BF)
            a1 = jnp.zeros((n, SGU_WIDTH), BF)
            a1, vr_p = _sgu_prompt(z, sgu_norm[i], sgu_w[i], sgu_b[i], a1, nb=nb, t=t)
            a1, vr_s = _sgu_sample(z, sgu_norm[i], sgu_w[i], sgu_b[i], a1, nb=nbs, ts=ts, row0=n_p)
            a2 = jnp.zeros((n, GLA_V_WIDTH), BF)
            a2, s_p = _gla_prompt(z, w2, gla_gate_b[i], gla_norm[i], a2, nb=nb, t=t)
            a2, s_s = _gla_sample(z, w2, gla_gate_b[i], gla_norm[i], state_gla[i], a2, nb=nbs, ts=ts,
                                  row0=n_p)
            w_out = odd_w_out[i].astype(BF)
            x = _outproj(x, a1, a2, w_out[:SGU_WIDTH], w_out[SGU_WIDTH:])
            sgu_p.append(vr_p)
            gla_p.append(s_p)
            sgu_s.append(vr_s.reshape(nbs, ts, SGU_WIDTH))
            gla_s.append(s_s)
        x = _ffn(x, ffn2_norm[l], ffn2_w_gate[l], ffn2_w_up[l], ffn2_w_down[l])
    y_p = _final_norm(x, final_norm, row0=0, nrows=n_p)
    y_s = _final_norm(x, final_norm, row0=n_p, nrows=n_s)
    k_rows_p = jnp.transpose(kt_all.reshape(nb, n_even, DA_HEADS, 2, DA_HEAD_DIM, t), (0, 1, 5, 2, 3, 4))
    v_rows_p = jnp.transpose(vh_all, (0, 1, 3, 2, 4))

    return (y_p.reshape(nb, t, d), y_s.reshape(nbs, ts, d), k_rows_p, v_rows_p,
            jnp.stack(pool_p, axis=0), jnp.stack(sgu_p, axis=0), jnp.stack(gla_p, axis=0),
            jnp.stack(k_s, axis=1).reshape(nbs, -1, ts, DA_HEADS, 2, DA_HEAD_DIM),
            jnp.stack(v_s, axis=1).reshape(nbs, -1, ts, DA_HEADS, DA_PAIR),
            jnp.stack(pool_s, axis=0), jnp.stack(sgu_s, axis=0), jnp.stack(gla_s, axis=0))
```

```python
import functools
import math

import numpy as np
import jax
import jax.numpy as jnp
from jax import lax
from jax.experimental import pallas as pl
from jax.experimental.pallas import tpu as pltpu

F32 = jnp.float32
BF = jnp.bfloat16

RMS_EPS = 1e-6
ROPE_THETA = 10000.0
PAGE_SIZE = 128

POOL_WINDOWS = (2, 4, 8, 16)
POOL_GROUP_DIM = 64
POOL_WIDTH = 256
POOL_HIST = 15
POOL_PAD = 16

DA_HEADS = 6
DA_HEAD_DIM = 64
DA_PAIR = 2 * DA_HEAD_DIM
DA_WIDTH = DA_HEADS * DA_PAIR
DA_VT_ROWS = DA_PAIR + 16
LOG2E = 1.4426950408889634
DA_SAFE_LOG2 = 60.0

SGU_GROUPS = 4
SGU_GROUP_DIM = 128
SGU_WIDTH = 512
SGU_CHUNK = 128

GLA_HEADS = 4
GLA_K_DIM = 64
GLA_V_DIM = 128
GLA_K_WIDTH = 256
GLA_V_WIDTH = 512
GLA_GATE_RANK = 16
GLA_GATE_NORMALIZER = 16.0
GLA_CHUNK = 64
GLA_RANK_PAD = 128

LANES = 128
VMEM_LIMIT = 56 * 1024 * 1024


def _cparams(*sem):
    return pltpu.CompilerParams(dimension_semantics=sem, vmem_limit_bytes=VMEM_LIMIT)


def _pick_tile(n, cap, mult=8):
    best = None
    for t in range(mult, min(n, cap) + 1, mult):
        if n % t == 0:
            best = t
    assert best is not None, (n, cap, mult)
    return best


def _shr(x, pow2):
    assert pow2 & (pow2 - 1) == 0
    return x >> (pow2.bit_length() - 1)


def _block_diag(w):
    g, a, b = w.shape
    out = jnp.zeros((g * a, g * b), w.dtype)
    for i in range(g):
        out = out.at[i * a:(i + 1) * a, i * b:(i + 1) * b].set(w[i])
    return out


ANY_SPEC = pl.BlockSpec(memory_space=pl.ANY)


def _rms(x, axis):
    return x * lax.rsqrt(jnp.mean(x * x, axis=axis, keepdims=True) + RMS_EPS)


def _ffn_kernel(x_ref, g_ref, wg_ref, wu_ref, wd_ref, o_ref, hn_ref, acc_ref, *, nf):
    f = pl.program_id(1)

    @pl.when(f == 0)
    def _():
        hn_ref[...] = (_rms(x_ref[...], -1) * g_ref[...]).astype(BF)
        acc_ref[...] = jnp.zeros_like(acc_ref)

    hn = hn_ref[...]
    a = jnp.dot(hn, wg_ref[...].astype(BF), preferred_element_type=F32)
    u = jnp.dot(hn, wu_ref[...].astype(BF), preferred_element_type=F32)
    h = (a * jax.nn.sigmoid(a) * u).astype(BF)
    acc_ref[...] += jnp.dot(h, wd_ref[...].astype(BF), preferred_element_type=F32)

    @pl.when(f == nf - 1)
    def _():
        o_ref[...] = x_ref[...] + 0.5 * acc_ref[...]


def _ffn(x, g, wg, wu, wd, *, layer):
    n, d = x.shape
    ff = wg.shape[2]
    tm = _pick_tile(n, 1280)
    tf = _pick_tile(ff, 256, LANES)
    nf = ff // tf
    return pl.pallas_call(
        functools.partial(_ffn_kernel, nf=nf),
        grid=(n // tm, nf),
        in_specs=[
            pl.BlockSpec((tm, d), lambda m, f: (m, 0)),
            pl.BlockSpec((1, d), lambda m, f: (0, 0)),
            pl.BlockSpec((None, d, tf), lambda m, f: (layer, 0, f)),
            pl.BlockSpec((None, d, tf), lambda m, f: (layer, 0, f)),
            pl.BlockSpec((None, tf, d), lambda m, f: (layer, f, 0)),
        ],
        out_specs=pl.BlockSpec((tm, d), lambda m, f: (m, 0)),
        out_shape=jax.ShapeDtypeStruct((n, d), F32),
        scratch_shapes=[pltpu.VMEM((tm, d), BF), pltpu.VMEM((tm, d), F32)],
        compiler_params=_cparams("parallel", "arbitrary"),
        name="ffn",
    )(x, g.reshape(1, d), wg, wu, wd)


def _inproj_kernel(x_ref, g_ref, w_ref, o_ref):
    hn = (_rms(x_ref[...], -1) * g_ref[...]).astype(BF)
    o_ref[...] = jnp.dot(hn, w_ref[...], preferred_element_type=F32)


def _inproj(x, g, w):
    n, d = x.shape
    nout = w.shape[1]
    tm = _pick_tile(n, 640)
    return pl.pallas_call(
        _inproj_kernel,
        grid=(n // tm,),
        in_specs=[
            pl.BlockSpec((tm, d), lambda m: (m, 0)),
            pl.BlockSpec((1, d), lambda m: (0, 0)),
            pl.BlockSpec((d, nout), lambda m: (0, 0)),
        ],
        out_specs=pl.BlockSpec((tm, nout), lambda m: (m, 0)),
        out_shape=jax.ShapeDtypeStruct((n, nout), F32),
        compiler_params=_cparams("parallel"),
        name="inproj",
    )(x, g.reshape(1, d), w)


def _outproj_kernel(x_ref, a1_ref, a2_ref, w1_ref, w2_ref, o_ref):
    y = jnp.dot(a1_ref[...], w1_ref[...], preferred_element_type=F32)
    y += jnp.dot(a2_ref[...], w2_ref[...], preferred_element_type=F32)
    o_ref[...] = x_ref[...] + y


def _outproj(x, a1, a2, w1, w2):
    n, d = x.shape
    k1, k2 = a1.shape[1], a2.shape[1]
    tm = _pick_tile(n, 1280)
    return pl.pallas_call(
        _outproj_kernel,
        grid=(n // tm,),
        in_specs=[
            pl.BlockSpec((tm, d), lambda m: (m, 0)),
            pl.BlockSpec((tm, k1), lambda m: (m, 0)),
            pl.BlockSpec((tm, k2), lambda m: (m, 0)),
            pl.BlockSpec((k1, d), lambda m: (0, 0)),
            pl.BlockSpec((k2, d), lambda m: (0, 0)),
        ],
        out_specs=pl.BlockSpec((tm, d), lambda m: (m, 0)),
        out_shape=jax.ShapeDtypeStruct((n, d), F32),
        compiler_params=_cparams("parallel"),
        name="outproj",
    )(x, a1, a2, w1, w2)


def _final_norm_kernel(x_ref, g_ref, o_ref):
    o_ref[...] = _rms(x_ref[...], -1) * g_ref[...]


def _final_norm(x, g, *, row0, nrows):
    d = x.shape[1]
    tm = _pick_tile(math.gcd(nrows, row0) if row0 else nrows, 1024)
    rb = row0 // tm
    return pl.pallas_call(
        _final_norm_kernel,
        grid=(nrows // tm,),
        in_specs=[pl.BlockSpec((tm, d), lambda m: (rb + m, 0)), pl.BlockSpec((1, d), lambda m: (0, 0))],
        out_specs=pl.BlockSpec((tm, d), lambda m: (m, 0)),
        out_shape=jax.ShapeDtypeStruct((nrows, d), F32),
        compiler_params=_cparams("parallel"),
        name="final_norm",
    )(x, g.reshape(1, d))


def _pool_kernel(pc_ref, pp_ref, w_ref, sc_ref, dst_ref, o_ref, *, tm, pos0, prev_at_first):
    del dst_ref
    i = pl.program_id(1)
    p = pc_ref[...]
    prev = pp_ref[...]
    if not prev_at_first:
        prev = jnp.where(i > 0, prev, 0.0)
    ext = jnp.concatenate([prev, p], axis=0)
    s2 = ext + pltpu.roll(ext, 1, 0)
    s4 = s2 + pltpu.roll(s2, 2, 0)
    s8 = s4 + pltpu.roll(s4, 4, 0)
    s16 = s8 + pltpu.roll(s8, 8, 0)
    grp = _shr(lax.broadcasted_iota(jnp.int32, (tm, POOL_WIDTH), 1), POOL_GROUP_DIM)
    pos = pos0 + i * tm + lax.broadcasted_iota(jnp.int32, (tm, POOL_WIDTH), 0)
    sums = (s2, s4, s8, s16)
    s = sums[3][POOL_PAD:]
    win = jnp.full((tm, POOL_WIDTH), POOL_WINDOWS[3], jnp.int32)
    for gi in (2, 1, 0):
        s = jnp.where(grp == gi, sums[gi][POOL_PAD:], s)
        win = jnp.where(grp == gi, POOL_WINDOWS[gi], win)
    cnt = jnp.minimum(pos + 1, win).astype(F32)
    diff = s / cnt - p
    y = jnp.dot(diff.astype(BF), w_ref[...], preferred_element_type=F32) * sc_ref[...]
    o_ref[...] = y.astype(BF)


def _pool(z, prev_src, w_bd, scale, dst, *, nb, t, row0, col_blk, prev_is_state, pos0):
    tm = _pick_tile(t, 512)
    nt = t // tm
    rb0 = row0 // tm
    if prev_is_state:
        assert nt == 1
        prev_spec = pl.BlockSpec((POOL_PAD, POOL_WIDTH), lambda b, i: (b, 0))
    else:
        r16 = tm // POOL_PAD
        base16 = row0 // POOL_PAD
        prev_spec = pl.BlockSpec(
            (POOL_PAD, POOL_WIDTH),
            lambda b, i: (jnp.maximum(base16 + (b * nt + i) * r16 - 1, 0), col_blk))
    return pl.pallas_call(
        functools.partial(_pool_kernel, tm=tm, pos0=pos0, prev_at_first=prev_is_state),
        grid=(nb, nt),
        in_specs=[
            pl.BlockSpec((tm, POOL_WIDTH), lambda b, i: (rb0 + b * nt + i, col_blk)),
            prev_spec,
            pl.BlockSpec((POOL_WIDTH, POOL_WIDTH), lambda b, i: (0, 0)),
            pl.BlockSpec((1, POOL_WIDTH), lambda b, i: (0, 0)),
            ANY_SPEC,
        ],
        out_specs=pl.BlockSpec((tm, POOL_WIDTH), lambda b, i: (rb0 + b * nt + i, 0)),
        out_shape=jax.ShapeDtypeStruct(dst.shape, dst.dtype),
        input_output_aliases={4: 0},
        compiler_params=_cparams("parallel", "arbitrary"),
        name="pool",
    )(z, prev_src, w_bd, scale.reshape(1, POOL_WIDTH), dst)


def _rope_tables(pos):
    half = DA_HEAD_DIM // 2
    inv = ROPE_THETA ** (-jnp.arange(half, dtype=F32) / half)
    ang = pos.astype(F32)[:, None] * inv[None, :]
    cos, sin = jnp.cos(ang), jnp.sin(ang)
    cos128 = jnp.concatenate([cos, cos, cos, cos], axis=1)
    sin128 = jnp.concatenate([-sin, sin, -sin, sin], axis=1)
    return cos128, sin128


def _rope128(x, cos, sin):
    lane = lax.broadcasted_iota(jnp.int32, x.shape, 1)
    first = (lane & (DA_HEAD_DIM - 1)) < (DA_HEAD_DIM // 2)
    partner = jnp.where(first, pltpu.roll(x, LANES - 32, 1), pltpu.roll(x, 32, 1))
    return x * cos + partner * sin


def _prep_prompt_kernel(zq_ref, zk_ref, zv_ref, cos_ref, sin_ref, kt_dst_ref, vh_dst_ref,
                        qt_ref, kb_ref, vt_ref, kn_ref, kt_ref, vh_ref):
    del kt_dst_ref, vh_dst_ref
    cos, sin = cos_ref[...], sin_ref[...]
    scale = DA_HEAD_DIM ** -0.5 * LOG2E
    ones = jnp.ones((DA_VT_ROWS - DA_PAIR, zq_ref.shape[0]), BF)
    for h in range(DA_HEADS):
        sl = slice(h * DA_PAIR, (h + 1) * DA_PAIR)
        q = _rope128(zq_ref[:, sl], cos, sin) * scale
        qt_ref[h] = q.T.astype(BF)
        k = _rope128(zk_ref[:, sl], cos, sin)
        kt_ref[h] = k.T
        kb = k.astype(BF)
        kb_ref[h] = kb
        kf = kb.astype(F32)
        kn2 = jnp.max(jnp.sum(kf * kf, axis=1, keepdims=True), axis=0, keepdims=True)
        kn_ref[h] = jnp.broadcast_to(kn2, kn_ref.shape[1:])
        v = zv_ref[:, sl]
        vh_ref[h] = v
        vt_ref[h, :DA_PAIR] = v.T.astype(BF)
        vt_ref[h, DA_PAIR:] = ones


def _prep_prompt(z, cos, sin, kt_all, vh_all, *, layer, nb, t, tk):
    nt = t // tk
    return pl.pallas_call(
        _prep_prompt_kernel,
        grid=(nb, nt),
        in_specs=[
            pl.BlockSpec((tk, DA_WIDTH), lambda b, i: (b * nt + i, 0)),
            pl.BlockSpec((tk, DA_WIDTH), lambda b, i: (b * nt + i, 1)),
            pl.BlockSpec((tk, DA_WIDTH), lambda b, i: (b * nt + i, 2)),
            pl.BlockSpec((tk, LANES), lambda b, i: (i, 0)),
            pl.BlockSpec((tk, LANES), lambda b, i: (i, 0)),
            ANY_SPEC,
            ANY_SPEC,
        ],
        out_specs=[
            pl.BlockSpec((None, DA_HEADS, DA_PAIR, tk), lambda b, i: (b, 0, 0, i)),
            pl.BlockSpec((None, DA_HEADS, tk, DA_PAIR), lambda b, i: (b, 0, i, 0)),
            pl.BlockSpec((None, DA_HEADS, None, DA_VT_ROWS, tk), lambda b, i: (b, 0, i, 0, 0)),
            pl.BlockSpec((None, DA_HEADS, None, 8, LANES), lambda b, i: (b, 0, i, 0, 0)),
            pl.BlockSpec((None, None, DA_HEADS, DA_PAIR, tk), lambda b, i: (b, layer, 0, 0, i)),
            pl.BlockSpec((None, None, DA_HEADS, tk, DA_PAIR), lambda b, i: (b, layer, 0, i, 0)),
        ],
        out_shape=[
            jax.ShapeDtypeStruct((nb, DA_HEADS, DA_PAIR, t), BF),
            jax.ShapeDtypeStruct((nb, DA_HEADS, t, DA_PAIR), BF),
            jax.ShapeDtypeStruct((nb, DA_HEADS, nt, DA_VT_ROWS, tk), BF),
            jax.ShapeDtypeStruct((nb, DA_HEADS, nt, 8, LANES), F32),
            jax.ShapeDtypeStruct(kt_all.shape, kt_all.dtype),
            jax.ShapeDtypeStruct(vh_all.shape, vh_all.dtype),
        ],
        input_output_aliases={5: 4, 6: 5},
        compiler_params=_cparams("parallel", "parallel"),
        name="prep_prompt",
    )(z, z, z, cos, sin, kt_all, vh_all)


def _lambda_value(lamv_ref, lam_init):
    lv = lamv_ref[...]
    s01 = jnp.sum(lv[0:1] * lv[1:2], axis=1, keepdims=True)
    s23 = jnp.sum(lv[2:3] * lv[3:4], axis=1, keepdims=True)
    return jnp.exp(s01) - jnp.exp(s23) + lam_init


def _attn_prompt_kernel(qt_ref, k_ref, vt_ref, kn_ref, lamv_ref, g_ref, dst_ref, o_ref,
                        qx_ref, sa_ref, sb_ref, m_ref, acc_ref, *, tq, lam_init):
    del dst_ref
    qi = pl.program_id(2)
    qt = qt_ref[...]
    row = lax.broadcasted_iota(jnp.int32, qt.shape, 0)
    zero = jnp.zeros_like(qt)
    qx_ref[:, :tq] = jnp.where(row < DA_HEAD_DIM, qt, zero)
    qx_ref[:, tq:] = jnp.where(row >= DA_HEAD_DIM, qt, zero)
    qf = qt.astype(F32)
    qn2 = jnp.max(jnp.sum(qf * qf, axis=0, keepdims=True))
    bounded = qn2 * jnp.max(kn_ref[...]) < DA_SAFE_LOG2 ** 2

    def keys(j):
        return k_ref[pl.ds(pl.multiple_of(j * tq, tq), tq), :]

    def diagonal_mask(shape):
        kpos = lax.broadcasted_iota(jnp.int32, shape, 0)
        qpos = lax.broadcasted_iota(jnp.int32, shape, 1) & (tq - 1)
        return kpos <= qpos

    def plain_chunk(j, diagonal, first):
        p = jnp.exp2(jnp.dot(keys(j), qx_ref[...], preferred_element_type=F32))
        if diagonal:
            p = jnp.where(diagonal_mask(p.shape), p, 0.0)
        d = jnp.dot(vt_ref[j], p.astype(BF), preferred_element_type=F32)
        acc_ref[...] = d if first else acc_ref[...] + d

    @pl.when(bounded)
    def _():
        plain_chunk(qi, True, True)

        @pl.when((qi & 1) == 1)
        def _():
            plain_chunk(0, False, False)

        def pair(pi, carry):
            j = (qi & 1) + 2 * pi
            plain_chunk(j, False, False)
            plain_chunk(j + 1, False, False)
            return carry

        lax.fori_loop(0, qi >> 1, pair, 0)

    def scores(j):
        return jnp.dot(keys(j), qx_ref[...], preferred_element_type=F32)

    def update(s, j):
        m_prev = m_ref[...]
        m_new = jnp.maximum(m_prev, jnp.max(s, axis=0, keepdims=True))
        p = jnp.exp2(s - m_new).astype(BF)
        acc_ref[...] = jnp.exp2(m_prev - m_new) * acc_ref[...] + jnp.dot(
            vt_ref[j], p, preferred_element_type=F32)
        m_ref[...] = m_new

    @pl.when(jnp.logical_not(bounded))
    def _():
        s = scores(qi)
        s = jnp.where(diagonal_mask(s.shape), s, -jnp.inf)
        m0 = jnp.max(s, axis=0, keepdims=True)
        m_ref[...] = m0
        acc_ref[...] = jnp.dot(vt_ref[qi], jnp.exp2(s - m0).astype(BF), preferred_element_type=F32)
        first = qi & 1

        @pl.when(first == 1)
        def _():
            update(scores(0), 0)

        @pl.when(qi >= 2)
        def _():
            sa_ref[...] = scores(first)

        def pair(pi, carry):
            j = first + 2 * pi
            sb_ref[...] = scores(j + 1)
            update(sa_ref[...], j)
            sa_ref[...] = scores(jnp.minimum(j + 2, qi - 1))
            update(sb_ref[...], j + 1)
            return carry

        lax.fori_loop(0, qi >> 1, pair, 0)

    acc = acc_ref[...]
    on = acc[:DA_PAIR] / acc[DA_PAIR:DA_PAIR + 1]
    lam = _lambda_value(lamv_ref, lam_init)
    ot = on[:, :tq] - lam * on[:, tq:]
    ot = _rms(ot, 0) * g_ref[...] * (1.0 - lam_init)
    o_ref[...] = ot.T.astype(BF)


def _attn_prompt(qt, kb, vt, kn2, lamv, subln_g, dst, *, tq, lam_init):
    nb, nh, _, t = qt.shape
    nk = t // tq
    return pl.pallas_call(
        functools.partial(_attn_prompt_kernel, tq=tq, lam_init=lam_init),
        grid=(nb, nh, t // tq),
        in_specs=[
            pl.BlockSpec((None, None, DA_PAIR, tq), lambda b, h, i: (b, h, 0, i)),
            pl.BlockSpec((None, None, t, DA_PAIR), lambda b, h, i: (b, h, 0, 0)),
            pl.BlockSpec((None, None, nk, DA_VT_ROWS, tq), lambda b, h, i: (b, h, 0, 0, 0)),
            pl.BlockSpec((None, None, nk, 8, LANES), lambda b, h, i: (b, h, 0, 0, 0)),
            pl.BlockSpec((4, DA_HEAD_DIM), lambda b, h, i: (0, 0)),
            pl.BlockSpec((DA_PAIR, 1), lambda b, h, i: (0, 0)),
            ANY_SPEC,
        ],
        out_specs=pl.BlockSpec((tq, DA_PAIR), lambda b, h, i: (b * nk + i, h)),
        out_shape=jax.ShapeDtypeStruct(dst.shape, dst.dtype),
        scratch_shapes=[pltpu.VMEM((DA_PAIR, 2 * tq), BF),
                        pltpu.VMEM((tq, 2 * tq), F32), pltpu.VMEM((tq, 2 * tq), F32),
                        pltpu.VMEM((1, 2 * tq), F32), pltpu.VMEM((DA_VT_ROWS, 2 * tq), F32)],
        input_output_aliases={6: 0},
        compiler_params=_cparams("parallel", "parallel", "arbitrary"),
        name="attn_prompt",
    )(qt, kb, vt, kn2, lamv, subln_g.reshape(DA_PAIR, 1), dst)


def _rope_rows_kernel(zq_ref, zk_ref, cos_ref, sin_ref, q_ref, k_ref):
    cos, sin = cos_ref[...], sin_ref[...]
    scale = DA_HEAD_DIM ** -0.5
    for h in range(DA_HEADS):
        sl = slice(h * DA_PAIR, (h + 1) * DA_PAIR)
        q_ref[:, sl] = (_rope128(zq_ref[:, sl], cos, sin) * scale).astype(BF)
        k_ref[:, sl] = _rope128(zk_ref[:, sl], cos, sin)


def _rope_rows(z, cos, sin, *, row0, nrows):
    rb = row0 // nrows
    return pl.pallas_call(
        _rope_rows_kernel,
        grid=(1,),
        in_specs=[
            pl.BlockSpec((nrows, DA_WIDTH), lambda i: (rb, 0)),
            pl.BlockSpec((nrows, DA_WIDTH), lambda i: (rb, 1)),
            pl.BlockSpec((nrows, LANES), lambda i: (0, 0)),
            pl.BlockSpec((nrows, LANES), lambda i: (0, 0)),
        ],
        out_specs=[pl.BlockSpec((nrows, DA_WIDTH), lambda i: (0, 0)),
                   pl.BlockSpec((nrows, DA_WIDTH), lambda i: (0, 0))],
        out_shape=[jax.ShapeDtypeStruct((nrows, DA_WIDTH), BF),
                   jax.ShapeDtypeStruct((nrows, DA_WIDTH), F32)],
        compiler_params=_cparams("arbitrary"),
        name="rope_rows",
    )(z, z, cos, sin)


def _attn_sample_kernel(pt_ref, q_ref, kn_ref, vn_ref, sel_ref, lamv_ref, g_ref, *rest,
                        pp, nsteps, ts, lam_init):
    kpages, vpages = rest[:pp], rest[pp:2 * pp]
    _, o_ref, m_ref, l_ref, acc_ref, qbd_ref = rest[2 * pp:]
    s = pl.program_id(1)
    nrow = 2 * DA_HEADS * ts

    @pl.when(s == 0)
    def _():
        q = q_ref[...]
        qrep = jnp.concatenate([q] * (2 * DA_HEADS), axis=0)
        qbd_ref[...] = jnp.where(sel_ref[...] > 0, qrep, jnp.zeros_like(qrep))
        m_ref[...] = jnp.full(m_ref.shape, -jnp.inf, F32)
        l_ref[...] = jnp.zeros_like(l_ref)
        acc_ref[...] = jnp.zeros_like(acc_ref)

    def update(sc, vv, causal):
        if causal:
            tq = lax.broadcasted_iota(jnp.int32, sc.shape, 0) & (ts - 1)
            tk = lax.broadcasted_iota(jnp.int32, sc.shape, 1)
            sc = jnp.where(tk <= tq, sc, -jnp.inf)
        m_prev = m_ref[...]
        m_new = jnp.maximum(m_prev, jnp.max(sc, axis=1, keepdims=True))
        alpha = jnp.exp(m_prev - m_new)
        p = jnp.exp(sc - m_new)
        l_ref[...] = alpha * l_ref[...] + jnp.sum(p, axis=1, keepdims=True)
        acc_ref[...] = alpha * acc_ref[...] + jnp.dot(p.astype(BF), vv, preferred_element_type=F32)
        m_ref[...] = m_new

    if pp:
        kt = jnp.concatenate([r[...] for r in kpages], axis=1).astype(BF)
        vv = jnp.concatenate(
            [jnp.concatenate([r[h] for h in range(DA_HEADS)], axis=1) for r in vpages],
            axis=0).astype(BF)
        update(jnp.dot(qbd_ref[...], kt, preferred_element_type=F32), vv, False)

    @pl.when(s == nsteps - 1)
    def _():
        sc_new = lax.dot_general(qbd_ref[...], kn_ref[...].astype(BF), (((1,), (1,)), ((), ())),
                                 preferred_element_type=F32)
        update(sc_new, vn_ref[...].astype(BF), True)
        on = acc_ref[...] / l_ref[...]
        lam = _lambda_value(lamv_ref, lam_init)
        half = nrow // 2
        d = on[:half] - lam * on[half:]
        outs = []
        for h in range(DA_HEADS):
            blk = d[h * ts:(h + 1) * ts, h * DA_PAIR:(h + 1) * DA_PAIR]
            outs.append(_rms(blk, -1) * g_ref[...] * (1.0 - lam_init))
        o_ref[...] = jnp.concatenate(outs, axis=1).astype(BF)


def _attn_sample(q_s, k_new, z, cache_kt, cache_vh, page_table, lamv, subln_g, dst, *,
                 ts, layer, row0, lam_init):
    nb = q_s.shape[0] // ts
    n_pages = page_table.shape[1]
    pp = 0
    for cand in (16, 8, 4, 2, 1):
        if n_pages and n_pages % cand == 0:
            pp = cand
            break
    nsteps = max(n_pages // pp, 1) if pp else 1
    nrow = 2 * DA_HEADS * ts
    r = np.arange(nrow)[:, None] // ts
    c = np.arange(DA_WIDTH)[None, :] // DA_HEAD_DIM
    sel = jnp.asarray(((r % DA_HEADS) * 2 + r // DA_HEADS == c).astype(np.float32))
    rb = row0 // ts

    def kpage_spec(i):
        return pl.BlockSpec((None, None, DA_WIDTH, PAGE_SIZE),
                            lambda b, s, pt: (pt[b * n_pages + s * pp + i], layer, 0, 0))

    def vpage_spec(i):
        return pl.BlockSpec((None, None, DA_HEADS, PAGE_SIZE, DA_PAIR),
                            lambda b, s, pt: (pt[b * n_pages + s * pp + i], layer, 0, 0, 0))

    in_specs = [
        pl.BlockSpec((ts, DA_WIDTH), lambda b, s, pt: (b, 0)),
        pl.BlockSpec((ts, DA_WIDTH), lambda b, s, pt: (b, 0)),
        pl.BlockSpec((ts, DA_WIDTH), lambda b, s, pt: (rb + b, 2)),
        pl.BlockSpec((nrow, DA_WIDTH), lambda b, s, pt: (0, 0)),
        pl.BlockSpec((4, DA_HEAD_DIM), lambda b, s, pt: (0, 0)),
        pl.BlockSpec((1, DA_PAIR), lambda b, s, pt: (0, 0)),
    ] + [kpage_spec(i) for i in range(pp)] + [vpage_spec(i) for i in range(pp)] + [ANY_SPEC]
    grid_spec = pltpu.PrefetchScalarGridSpec(
        num_scalar_prefetch=1,
        grid=(nb, nsteps),
        in_specs=in_specs,
        out_specs=pl.BlockSpec((ts, DA_WIDTH), lambda b, s, pt: (rb + b, 0)),
        scratch_shapes=[pltpu.VMEM((nrow, 1), F32), pltpu.VMEM((nrow, 1), F32),
                        pltpu.VMEM((nrow, DA_WIDTH), F32), pltpu.VMEM((nrow, DA_WIDTH), BF)],
    )
    return pl.pallas_call(
        functools.partial(_attn_sample_kernel, pp=pp, nsteps=nsteps, ts=ts, lam_init=lam_init),
        grid_spec=grid_spec,
        out_shape=jax.ShapeDtypeStruct(dst.shape, dst.dtype),
        input_output_aliases={7 + 2 * pp: 0},
        compiler_params=_cparams("parallel", "arbitrary"),
        name="attn_sample",
    )(page_table.reshape(-1), q_s, k_new, z, sel, lamv, subln_g.reshape(1, DA_PAIR),
      *([cache_kt] * pp), *([cache_vh] * pp), dst)


def _sgu_norm_v(sv_ref, ng_ref, g):
    x = jax.nn.gelu(sv_ref[:, g * SGU_GROUP_DIM:(g + 1) * SGU_GROUP_DIM])
    return _rms(x, -1) * ng_ref[g:g + 1, :]


def _sgu_prompt_kernel(u_ref, sv_ref, ng_ref, w_ref, bs_ref, dst_ref, y_ref, vr_ref, *, tm, c):
    del dst_ref
    last = pl.program_id(1) == pl.num_programs(1) - 1
    ri = lax.broadcasted_iota(jnp.int32, (c, c), 0)
    ci = lax.broadcasted_iota(jnp.int32, (c, c), 1)
    for g in range(SGU_GROUPS):
        sl = slice(g * SGU_GROUP_DIM, (g + 1) * SGU_GROUP_DIM)
        v = _sgu_norm_v(sv_ref, ng_ref, g)

        @pl.when(last)
        def _():
            vr_ref[:, sl] = v[tm - c:]

        vb = v.astype(BF)
        w = jnp.where(ri >= ci, w_ref[g], 0.0).astype(BF)
        for n in range(tm // c):
            rows = slice(n * c, (n + 1) * c)
            mixed = jnp.dot(w, vb[rows], preferred_element_type=F32) + bs_ref[g]
            y_ref[rows, sl] = (jax.nn.gelu(u_ref[rows, sl]) * mixed).astype(BF)


def _sgu_prompt(z, norm_g, w_s, b_s, dst, *, nb, t):
    c = SGU_CHUNK
    tm = _pick_tile(t, 512, c)
    nt = t // tm
    bs = jnp.broadcast_to(b_s[:, :c, None], (SGU_GROUPS, c, SGU_GROUP_DIM))
    return pl.pallas_call(
        functools.partial(_sgu_prompt_kernel, tm=tm, c=c),
        grid=(nb, nt),
        in_specs=[
            pl.BlockSpec((tm, SGU_WIDTH), lambda b, i: (b * nt + i, 0)),
            pl.BlockSpec((tm, SGU_WIDTH), lambda b, i: (b * nt + i, 1)),
            pl.BlockSpec((SGU_GROUPS, SGU_GROUP_DIM), lambda b, i: (0, 0)),
            pl.BlockSpec((SGU_GROUPS, c, c), lambda b, i: (0, 0, 0)),
            pl.BlockSpec((SGU_GROUPS, c, SGU_GROUP_DIM), lambda b, i: (0, 0, 0)),
            ANY_SPEC,
        ],
        out_specs=[
            pl.BlockSpec((tm, SGU_WIDTH), lambda b, i: (b * nt + i, 0)),
            pl.BlockSpec((None, c, SGU_WIDTH), lambda b, i: (b, 0, 0)),
        ],
        out_shape=[jax.ShapeDtypeStruct(dst.shape, dst.dtype),
                   jax.ShapeDtypeStruct((nb, c, SGU_WIDTH), F32)],
        input_output_aliases={5: 0},
        compiler_params=_cparams("parallel", "arbitrary"),
        name="sgu_prompt",
    )(z, z, norm_g, w_s[:, :c, :c], bs, dst)


def _sgu_sample_kernel(u_ref, sv_ref, ng_ref, w_ref, bs_ref, dst_ref, y_ref, vr_ref, *, nb, ts):
    del dst_ref
    v = jnp.concatenate([_sgu_norm_v(sv_ref, ng_ref, g) for g in range(SGU_GROUPS)], axis=1)
    vr_ref[...] = v
    v3 = v.reshape(nb, ts, SGU_WIDTH)
    ii = lax.broadcasted_iota(jnp.int32, (ts, SGU_WIDTH), 0)
    mixed = jnp.broadcast_to(bs_ref[...][None], (nb, ts, SGU_WIDTH))
    for j in range(ts):
        wj = jnp.where(ii >= j, w_ref[j], 0.0)
        mixed = mixed + wj[None] * v3[:, j:j + 1, :]
    y = jax.nn.gelu(u_ref[...]).reshape(nb, ts, SGU_WIDTH) * mixed
    y_ref[...] = y.reshape(nb * ts, SGU_WIDTH).astype(BF)


def _sgu_sample(z, norm_g, w_s, b_s, dst, *, nb, ts, row0):
    nrows = nb * ts
    rb = row0 // nrows
    w_exp = jnp.repeat(jnp.transpose(w_s[:, :ts, :ts], (2, 1, 0)), SGU_GROUP_DIM, axis=2)
    b_exp = jnp.repeat(jnp.transpose(b_s[:, :ts], (1, 0)), SGU_GROUP_DIM, axis=1)
    return pl.pallas_call(
        functools.partial(_sgu_sample_kernel, nb=nb, ts=ts),
        grid=(1,),
        in_specs=[
            pl.BlockSpec((nrows, SGU_WIDTH), lambda i: (rb, 0)),
            pl.BlockSpec((nrows, SGU_WIDTH), lambda i: (rb, 1)),
            pl.BlockSpec((SGU_GROUPS, SGU_GROUP_DIM), lambda i: (0, 0)),
            pl.BlockSpec((ts, ts, SGU_WIDTH), lambda i: (0, 0, 0)),
            pl.BlockSpec((ts, SGU_WIDTH), lambda i: (0, 0)),
            ANY_SPEC,
        ],
        out_specs=[pl.BlockSpec((nrows, SGU_WIDTH), lambda i: (rb, 0)),
                   pl.BlockSpec((nrows, SGU_WIDTH), lambda i: (0, 0))],
        out_shape=[jax.ShapeDtypeStruct(dst.shape, dst.dtype),
                   jax.ShapeDtypeStruct((nrows, SGU_WIDTH), F32)],
        input_output_aliases={5: 0},
        compiler_params=_cparams("arbitrary"),
        name="sgu_sample",
    )(z, z, norm_g, w_exp, b_exp, dst)


def _log_sigmoid(x):
    return jnp.minimum(x, 0.0) - jnp.log1p(jnp.exp(-jnp.abs(x)))


def _seg_scan(x, seg, pos, reverse):
    n = x.shape[0]
    r = pos & (seg - 1)
    t = 1
    while t < seg:
        if reverse:
            x = x + jnp.where(r < seg - t, pltpu.roll(x, n - t, 0), 0.0)
        else:
            x = x + jnp.where(r >= t, pltpu.roll(x, t, 0), 0.0)
        t *= 2
    return x


def _gla_decays(gk, c):
    n, w = gk.shape
    pos = lax.broadcasted_iota(jnp.int32, gk.shape, 0)
    out = {1: (jnp.exp(gk), None)}
    s = 2
    while s <= c:
        if s <= 8 or n % s:
            pre = _seg_scan(gk, s, pos, False)
            suf = _seg_scan(gk, s, pos, True) - gk
        else:
            h = s // 2
            p4 = pre.reshape(n // s, 2, h, w)
            s4 = suf.reshape(n // s, 2, h, w)
            tot = p4[:, :, h - 1:h, :]
            pre = jnp.concatenate([p4[:, 0:1], p4[:, 1:2] + tot[:, 0:1]], axis=1).reshape(n, w)
            suf = jnp.concatenate([s4[:, 0:1] + tot[:, 1:2], s4[:, 1:2]], axis=1).reshape(n, w)
        out[s] = (jnp.exp(pre), jnp.exp(suf))
        s *= 2
    return out


def _gla_intra(q, k, v, dec, c, sc):
    n = q.shape[0]
    ri = lax.broadcasted_iota(jnp.int32, (sc, sc), 0)
    ci = lax.broadcasted_iota(jnp.int32, (sc, sc), 1)
    same_chunk = _shr(ri, c) == _shr(ci, c)
    lane = lax.broadcasted_iota(jnp.int32, (sc, LANES), 1)
    levels = []
    levels.append((ri == ci, q.astype(BF), k.astype(BF)))
    s = 1
    while s < c:
        pre, _ = dec[s]
        suf = dec[s][1]
        ks = k if suf is None else k * suf
        mask = same_chunk & (_shr(ri, 2 * s) == _shr(ci, 2 * s)) & ((_shr(ri, s) & 1) == 1) \
            & ((_shr(ci, s) & 1) == 0)
        levels.append((mask, (q * pre).astype(BF), ks.astype(BF)))
        s *= 2
    outs = []
    for h in range(GLA_HEADS):
        pair = slice((h // 2) * LANES, (h // 2 + 1) * LANES)
        own = _shr(lane, GLA_K_DIM) == (h % 2)
        vh = v[:, h * GLA_V_DIM:(h + 1) * GLA_V_DIM].astype(BF)
        rows_out = []
        for b0 in range(0, n, sc):
            rows = slice(b0, b0 + sc)
            a = jnp.zeros((sc, sc), F32)
            for mask, qs, ks in levels:
                qm = jnp.where(own, qs[rows, pair], jnp.zeros((sc, LANES), BF))
                al = lax.dot_general(qm, ks[rows, pair], (((1,), (1,)), ((), ())),
                                     preferred_element_type=F32)
                a = a + jnp.where(mask, al, 0.0)
            rows_out.append(jnp.dot(a.astype(BF), vh[rows], preferred_element_type=F32))
        outs.append(jnp.concatenate(rows_out, axis=0) if len(rows_out) > 1 else rows_out[0])
    return outs


def _gla_gate(glr_ref, w2_ref, gb_ref):
    pre = jnp.dot(glr_ref[...].astype(BF), w2_ref[...], preferred_element_type=F32) + gb_ref[...]
    return _log_sigmoid(pre) / GLA_GATE_NORMALIZER


def _gla_finish(o_heads, g_ref, gn_ref, o_ref):
    for h in range(GLA_HEADS):
        sl = slice(h * GLA_V_DIM, (h + 1) * GLA_V_DIM)
        gate = g_ref[:, sl]
        o_ref[:, sl] = (_rms(o_heads[h], -1) * gn_ref[...] * (gate * jax.nn.sigmoid(gate))).astype(BF)


def _gla_prompt_kernel(q_ref, k_ref, v_ref, g_ref, glr_ref, w2_ref, gb_ref, gn_ref, dst_ref,
                       o_ref, s_out_ref, st_ref, *, tm, c):
    del dst_ref
    i = pl.program_id(1)

    @pl.when(i == 0)
    def _():
        st_ref[...] = jnp.zeros_like(st_ref)

    gk = _gla_gate(glr_ref, w2_ref, gb_ref)
    q = q_ref[...] * (GLA_K_DIM ** -0.5)
    k = k_ref[...]
    v = v_ref[...]
    dec = _gla_decays(gk, c)
    o_heads = _gla_intra(q, k, v, dec, c, min(tm, LANES))
    eb, esuf = dec[c]
    qd = (q * eb).astype(BF)
    kd = (k * esuf).astype(BF)
    lane = lax.broadcasted_iota(jnp.int32, (c, LANES), 1)
    for h in range(GLA_HEADS):
        pair = slice((h // 2) * LANES, (h // 2 + 1) * LANES)
        own = _shr(lane, GLA_K_DIM) == (h % 2)
        vh = v[:, h * GLA_V_DIM:(h + 1) * GLA_V_DIM].astype(BF)
        st = st_ref[h]
        inter = []
        for n in range(tm // c):
            rows = slice(n * c, (n + 1) * c)
            qm = jnp.where(own, qd[rows, pair], jnp.zeros((c, LANES), BF))
            inter.append(lax.dot_general(qm, st.astype(BF), (((1,), (1,)), ((), ())),
                                         preferred_element_type=F32))
            upd = lax.dot_general(vh[rows], kd[rows, pair], (((0,), (0,)), ((), ())),
                                  preferred_element_type=F32)
            st = eb[n * c + c - 1:n * c + c, pair] * st + upd
        st_ref[h] = st
        o_heads[h] = o_heads[h] + jnp.concatenate(inter, axis=0)
    _gla_finish(o_heads, g_ref, gn_ref, o_ref)

    @pl.when(i == pl.num_programs(1) - 1)
    def _():
        for h in range(GLA_HEADS):
            lo = (h % 2) * GLA_K_DIM
            s_out_ref[h] = st_ref[h].T[lo:lo + GLA_K_DIM, :]


def _gla_prompt(z, w2, gate_b, gla_norm, dst, *, nb, t):
    c = GLA_CHUNK
    tm = _pick_tile(t, 256, LANES)
    nt = t // tm
    return pl.pallas_call(
        functools.partial(_gla_prompt_kernel, tm=tm, c=c),
        grid=(nb, nt),
        in_specs=[
            pl.BlockSpec((tm, GLA_K_WIDTH), lambda b, i: (b * nt + i, 4)),
            pl.BlockSpec((tm, GLA_K_WIDTH), lambda b, i: (b * nt + i, 5)),
            pl.BlockSpec((tm, GLA_V_WIDTH), lambda b, i: (b * nt + i, 3)),
            pl.BlockSpec((tm, GLA_V_WIDTH), lambda b, i: (b * nt + i, 4)),
            pl.BlockSpec((tm, GLA_RANK_PAD), lambda b, i: (b * nt + i, 20)),
            pl.BlockSpec((GLA_RANK_PAD, GLA_K_WIDTH), lambda b, i: (0, 0)),
            pl.BlockSpec((1, GLA_K_WIDTH), lambda b, i: (0, 0)),
            pl.BlockSpec((1, GLA_V_DIM), lambda b, i: (0, 0)),
            ANY_SPEC,
        ],
        out_specs=[
            pl.BlockSpec((tm, GLA_V_WIDTH), lambda b, i: (b * nt + i, 0)),
            pl.BlockSpec((None, GLA_HEADS, GLA_K_DIM, GLA_V_DIM), lambda b, i: (b, 0, 0, 0)),
        ],
        out_shape=[jax.ShapeDtypeStruct(dst.shape, dst.dtype),
                   jax.ShapeDtypeStruct((nb, GLA_HEADS, GLA_K_DIM, GLA_V_DIM), F32)],
        scratch_shapes=[pltpu.VMEM((GLA_HEADS, GLA_V_DIM, LANES), F32)],
        input_output_aliases={8: 0},
        compiler_params=_cparams("parallel", "arbitrary"),
        name="gla_prompt",
    )(z, z, z, z, z, w2, gate_b.reshape(1, GLA_K_WIDTH), gla_norm.reshape(1, GLA_V_DIM), dst)


def _gla_sample_kernel(q_ref, k_ref, v_ref, g_ref, glr_ref, w2_ref, gb_ref, gbc_ref, gn_ref, s0_ref,
                       dst_ref, o_ref, s_out_ref, *, ts):
    del dst_ref
    gk = _gla_gate(glr_ref, w2_ref, gb_ref)
    q = q_ref[...] * (GLA_K_DIM ** -0.5)
    k = k_ref[...]
    v = v_ref[...]
    dec = _gla_decays(gk, ts)
    o_heads = _gla_intra(q, k, v, dec, ts, ts)
    eb, esuf = dec[ts]
    qd = (q * eb).astype(BF)
    kd = (k * esuf).astype(BF)
    pre_t = lax.dot_general(w2_ref[...], glr_ref[...].astype(BF), (((0,), (1,)), ((), ())),
                            preferred_element_type=F32) + gbc_ref[...]
    dlast = jnp.exp(jnp.sum(_log_sigmoid(pre_t) / GLA_GATE_NORMALIZER, axis=1, keepdims=True))
    lane = lax.broadcasted_iota(jnp.int32, (ts, LANES), 1)
    for h in range(GLA_HEADS):
        pair = slice((h // 2) * LANES, (h // 2 + 1) * LANES)
        own = _shr(lane, GLA_K_DIM) == (h % 2)
        vh = v[:, h * GLA_V_DIM:(h + 1) * GLA_V_DIM].astype(BF)
        s_pair = s0_ref[pair, :]
        qm = jnp.where(own, qd[:, pair], jnp.zeros((ts, LANES), BF))
        o_heads[h] = o_heads[h] + jnp.dot(qm, s_pair.astype(BF), preferred_element_type=F32)
        upd = lax.dot_general(kd[:, pair], vh, (((0,), (0,)), ((), ())),
                              preferred_element_type=F32)
        hr = slice(h * GLA_K_DIM, (h + 1) * GLA_K_DIM)
        lo = (h % 2) * GLA_K_DIM
        s_out_ref[hr, :] = dlast[hr] * s0_ref[hr, :] + upd[lo:lo + GLA_K_DIM]
    _gla_finish(o_heads, g_ref, gn_ref, o_ref)


def _gla_sample(z, w2, gate_b, gla_norm, s0, dst, *, nb, ts, row0):
    rb = row0 // ts
    s0r = s0.reshape(nb, GLA_HEADS * GLA_K_DIM, GLA_V_DIM)
    o, s_new = pl.pallas_call(
        functools.partial(_gla_sample_kernel, ts=ts),
        grid=(nb,),
        in_specs=[
            pl.BlockSpec((ts, GLA_K_WIDTH), lambda b: (rb + b, 4)),
            pl.BlockSpec((ts, GLA_K_WIDTH), lambda b: (rb + b, 5)),
            pl.BlockSpec((ts, GLA_V_WIDTH), lambda b: (rb + b, 3)),
            pl.BlockSpec((ts, GLA_V_WIDTH), lambda b: (rb + b, 4)),
            pl.BlockSpec((ts, GLA_RANK_PAD), lambda b: (rb + b, 20)),
            pl.BlockSpec((GLA_RANK_PAD, GLA_K_WIDTH), lambda b: (0, 0)),
            pl.BlockSpec((1, GLA_K_WIDTH), lambda b: (0, 0)),
            pl.BlockSpec((GLA_K_WIDTH, 1), lambda b: (0, 0)),
            pl.BlockSpec((1, GLA_V_DIM), lambda b: (0, 0)),
            pl.BlockSpec((None, GLA_HEADS * GLA_K_DIM, GLA_V_DIM), lambda b: (b, 0, 0)),
            ANY_SPEC,
        ],
        out_specs=[
            pl.BlockSpec((ts, GLA_V_WIDTH), lambda b: (rb + b, 0)),
            pl.BlockSpec((None, GLA_HEADS * GLA_K_DIM, GLA_V_DIM), lambda b: (b, 0, 0)),
        ],
        out_shape=[jax.ShapeDtypeStruct(dst.shape, dst.dtype),
                   jax.ShapeDtypeStruct((nb, GLA_HEADS * GLA_K_DIM, GLA_V_DIM), F32)],
        input_output_aliases={10: 0},
        compiler_params=_cparams("parallel"),
        name="gla_sample",
    )(z, z, z, z, z, w2, gate_b.reshape(1, GLA_K_WIDTH), gate_b.reshape(GLA_K_WIDTH, 1),
      gla_norm.reshape(1, GLA_V_DIM), s0r, dst)
    return o, s_new.reshape(nb, GLA_HEADS, GLA_K_DIM, GLA_V_DIM)


def kernel(x_prompt, x_sample, cache_k, cache_v, page_table, state_pool, state_gla, ffn1_norm, ffn1_w_gate, ffn1_w_up, ffn1_w_down, mix_norm, ffn2_norm, ffn2_w_gate, ffn2_w_up, ffn2_w_down, even_w_in, even_w_out, pool_w, pool_scale, diff_lambda, diff_subln, odd_w_in, odd_w_out, sgu_norm, sgu_w, sgu_b, gla_gate_w2, gla_gate_b, gla_norm, final_norm):
    nb, t, d = x_prompt.shape
    nbs, ts, _ = x_sample.shape
    depth = ffn1_norm.shape[0]
    n_p, n_s = nb * t, nbs * ts
    past_len = page_table.shape[1] * PAGE_SIZE
    assert ts < min(SGU_CHUNK, GLA_CHUNK) and ts & (ts - 1) == 0 and n_p % n_s == 0
    assert t % SGU_CHUNK == 0 and t >= POOL_PAD

    n = n_p + n_s
    x = jnp.concatenate([x_prompt.reshape(n_p, d), x_sample.reshape(n_s, d)], axis=0)
    tq = _pick_tile(t, 512, LANES)
    cos_p, sin_p = _rope_tables(jnp.arange(t))
    cos_s, sin_s = _rope_tables(past_len + jnp.arange(ts))
    cos_s, sin_s = jnp.tile(cos_s, (nbs, 1)), jnp.tile(sin_s, (nbs, 1))

    n_even = (depth + 1) // 2
    cache_kt = jnp.transpose(cache_k, (0, 1, 3, 4, 5, 2)).reshape(
        cache_k.shape[0], cache_k.shape[1], DA_WIDTH, PAGE_SIZE)
    cache_vh = jnp.transpose(cache_v, (0, 1, 3, 2, 4))
    kt_all = jnp.zeros((nb, n_even, DA_HEADS, DA_PAIR, t), F32)
    vh_all = jnp.zeros((nb, n_even, DA_HEADS, t, DA_PAIR), F32)
    pool_p, sgu_p, gla_p = [], [], []
    k_s, v_s, pool_s, sgu_s, gla_s = [], [], [], [], []
    for l in range(depth):
        i = l // 2
        x = _ffn(x, ffn1_norm[l], ffn1_w_gate, ffn1_w_up, ffn1_w_down, layer=l)
        if l % 2 == 0:
            lam_init = 0.8 - 0.6 * math.exp(-0.3 * l)
            w_in = even_w_in[i]
            w_in = jnp.concatenate([w_in[:, POOL_WIDTH:], w_in[:, :POOL_WIDTH]], axis=1).astype(BF)
            z = _inproj(x, mix_norm[l], w_in)
            pcol = 3 * DA_WIDTH // POOL_WIDTH
            w_bd = _block_diag(pool_w[i]).astype(BF)
            a1 = jnp.zeros((n, POOL_WIDTH), BF)
            a1 = _pool(z, z, w_bd, pool_scale[i], a1, nb=nb, t=t, row0=0, col_blk=pcol,
                       prev_is_state=False, pos0=0)
            st_pad = jnp.pad(state_pool[i], ((0, 0), (POOL_PAD - POOL_HIST, 0), (0, 0)))
            a1 = _pool(z, st_pad.reshape(nbs * POOL_PAD, POOL_WIDTH), w_bd, pool_scale[i], a1,
                       nb=nbs, t=ts, row0=n_p, col_blk=pcol, prev_is_state=True, pos0=past_len)
            qt, kb, vt, kn2, kt_all, vh_all = _prep_prompt(z, cos_p, sin_p, kt_all, vh_all, layer=i,
                                                           nb=nb, t=t, tk=tq)
            a2 = jnp.zeros((n, DA_WIDTH), BF)
            a2 = _attn_prompt(qt, kb, vt, kn2, diff_lambda[i], diff_subln[i], a2, tq=tq,
                              lam_init=lam_init)
            q_s, k_new = _rope_rows(z, cos_s, sin_s, row0=n_p, nrows=n_s)
            a2 = _attn_sample(q_s, k_new, z, cache_kt, cache_vh, page_table, diff_lambda[i],
                              diff_subln[i], a2, ts=ts, layer=i, row0=n_p, lam_init=lam_init)
            w_out = even_w_out[i].astype(BF)
            x = _outproj(x, a1, a2, w_out[:POOL_WIDTH], w_out[POOL_WIDTH:])
            zs = z[n_p:].reshape(nbs, ts, -1)
            pool_p.append(jnp.stack([z[(b + 1) * t - POOL_HIST:(b + 1) * t, 3 * DA_WIDTH:]
                                     for b in range(nb)], axis=0))
            k_s.append(k_new.reshape(nbs, ts, DA_WIDTH))
            v_s.append(zs[:, :, 2 * DA_WIDTH:3 * DA_WIDTH])
            pool_s.append(jnp.concatenate([state_pool[i], zs[:, :, 3 * DA_WIDTH:]], axis=1)[:, -POOL_HIST:])
        else:
            w_in = jnp.pad(odd_w_in[i], ((0, 0), (0, GLA_RANK_PAD - GLA_GATE_RANK))).astype(BF)
            z = _inproj(x, mix_norm[l], w_in)
            w2 = jnp.pad(gla_gate_w2[i], ((0, GLA_RANK_PAD - GLA_GATE_RANK), (0, 0))).astype(BF)
            a1 = jnp.zeros((n, SGU_WIDTH), BF)
            a1, vr_p = _sgu_prompt(z, sgu_norm[i], sgu_w[i], sgu_b[i], a1, nb=nb, t=t)
            a1, vr_s = _sgu_sample(z, sgu_norm[i], sgu_w[i], sgu_b[i], a1, nb=nbs, ts=ts, row0=n_p)
            a2 = jnp.zeros((n, GLA_V_WIDTH), BF)
            a2, s_p = _gla_prompt(z, w2, gla_gate_b[i], gla_norm[i], a2, nb=nb, t=t)
            a2, s_s = _gla_sample(z, w2, gla_gate_b[i], gla_norm[i], state_gla[i], a2, nb=nbs, ts=ts,
                                  row0=n_p)
            w_out = odd_w_out[i].astype(BF)
            x = _outproj(x, a1, a2, w_out[:SGU_WIDTH], w_out[SGU_WIDTH:])
            sgu_p.append(vr_p)
            gla_p.append(s_p)
            sgu_s.append(vr_s.reshape(nbs, ts, SGU_WIDTH))
            gla_s.append(s_s)
        x = _ffn(x, ffn2_norm[l], ffn2_w_gate, ffn2_w_up, ffn2_w_down, layer=l)
    y_p = _final_norm(x, final_norm, row0=0, nrows=n_p)
    y_s = _final_norm(x, final_norm, row0=n_p, nrows=n_s)
    k_rows_p = jnp.transpose(kt_all.reshape(nb, n_even, DA_HEADS, 2, DA_HEAD_DIM, t), (0, 1, 5, 2, 3, 4))
    v_rows_p = jnp.transpose(vh_all, (0, 1, 3, 2, 4))

    return (y_p.reshape(nb, t, d), y_s.reshape(nbs, ts, d), k_rows_p, v_rows_p,
            jnp.stack(pool_p, axis=0), jnp.stack(sgu_p, axis=0), jnp.stack(gla_p, axis=0),
            jnp.stack(k_s, axis=1).reshape(nbs, -1, ts, DA_HEADS, 2, DA_HEAD_DIM),
            jnp.stack(v_s, axis=1).reshape(nbs, -1, ts, DA_HEADS, DA_PAIR),
            jnp.stack(pool_s, axis=0), jnp.stack(sgu_s, axis=0), jnp.stack(gla_s, axis=0))
```

```python
import functools
import math

import numpy as np
import jax
import jax.numpy as jnp
from jax import lax
from jax.experimental import pallas as pl
from jax.experimental.pallas import tpu as pltpu

F32 = jnp.float32
BF = jnp.bfloat16

RMS_EPS = 1e-6
ROPE_THETA = 10000.0
PAGE_SIZE = 128

POOL_WINDOWS = (2, 4, 8, 16)
POOL_GROUP_DIM = 64
POOL_WIDTH = 256
POOL_HIST = 15
POOL_PAD = 16

DA_HEADS = 6
DA_HEAD_DIM = 64
DA_PAIR = 2 * DA_HEAD_DIM
DA_WIDTH = DA_HEADS * DA_PAIR
DA_VT_ROWS = DA_PAIR + 16
LOG2E = 1.4426950408889634
DA_SAFE_LOG2 = 60.0

SGU_GROUPS = 4
SGU_GROUP_DIM = 128
SGU_WIDTH = 512
SGU_CHUNK = 128

GLA_HEADS = 4
GLA_K_DIM = 64
GLA_V_DIM = 128
GLA_K_WIDTH = 256
GLA_V_WIDTH = 512
GLA_GATE_RANK = 16
GLA_GATE_NORMALIZER = 16.0
GLA_CHUNK = 64
GLA_RANK_PAD = 128

LANES = 128
VMEM_LIMIT = 56 * 1024 * 1024


def _cparams(*sem):
    return pltpu.CompilerParams(dimension_semantics=sem, vmem_limit_bytes=VMEM_LIMIT)


def _pick_tile(n, cap, mult=8):
    best = None
    for t in range(mult, min(n, cap) + 1, mult):
        if n % t == 0:
            best = t
    assert best is not None, (n, cap, mult)
    return best


def _shr(x, pow2):
    assert pow2 & (pow2 - 1) == 0
    return x >> (pow2.bit_length() - 1)


def _block_diag(w):
    g, a, b = w.shape
    out = jnp.zeros((g * a, g * b), w.dtype)
    for i in range(g):
        out = out.at[i * a:(i + 1) * a, i * b:(i + 1) * b].set(w[i])
    return out


ANY_SPEC = pl.BlockSpec(memory_space=pl.ANY)


def _rms(x, axis):
    return x * lax.rsqrt(jnp.mean(x * x, axis=axis, keepdims=True) + RMS_EPS)


def _ffn_kernel(*refs, nf, mixed):
    if mixed:
        x_ref, a1_ref, a2_ref, w1_ref, w2_ref, g_ref, wg_ref, wu_ref, wd_ref, o_ref, hn_ref, acc_ref = refs
    else:
        x_ref, g_ref, wg_ref, wu_ref, wd_ref, o_ref, hn_ref, acc_ref = refs
    f = pl.program_id(1)

    @pl.when(f == 0)
    def _():
        x = x_ref[...]
        if mixed:
            x = x + jnp.dot(a1_ref[...], w1_ref[...], preferred_element_type=F32)
            x = x + jnp.dot(a2_ref[...], w2_ref[...], preferred_element_type=F32)
            o_ref[...] = x
        hn_ref[...] = (_rms(x, -1) * g_ref[...]).astype(BF)
        acc_ref[...] = jnp.zeros_like(acc_ref)

    hn = hn_ref[...]
    a = jnp.dot(hn, wg_ref[...].astype(BF), preferred_element_type=F32)
    u = jnp.dot(hn, wu_ref[...].astype(BF), preferred_element_type=F32)
    h = (a * jax.nn.sigmoid(a) * u).astype(BF)
    acc_ref[...] += jnp.dot(h, wd_ref[...].astype(BF), preferred_element_type=F32)

    @pl.when(f == nf - 1)
    def _():
        o_ref[...] = (o_ref[...] if mixed else x_ref[...]) + 0.5 * acc_ref[...]


def _ffn(x, g, wg, wu, wd, *, layer, mix=None):
    n, d = x.shape
    ff = wg.shape[2]
    tm = _pick_tile(n, 1280)
    tf = _pick_tile(ff, 256, LANES)
    nf = ff // tf
    mix_specs, mix_args = [], []
    if mix is not None:
        a1, a2, w1, w2 = mix
        mix_specs = [
            pl.BlockSpec((tm, a1.shape[1]), lambda m, f: (m, 0)),
            pl.BlockSpec((tm, a2.shape[1]), lambda m, f: (m, 0)),
            pl.BlockSpec(w1.shape, lambda m, f: (0, 0)),
            pl.BlockSpec(w2.shape, lambda m, f: (0, 0)),
        ]
        mix_args = [a1, a2, w1, w2]
    return pl.pallas_call(
        functools.partial(_ffn_kernel, nf=nf, mixed=mix is not None),
        grid=(n // tm, nf),
        in_specs=[pl.BlockSpec((tm, d), lambda m, f: (m, 0))] + mix_specs + [
            pl.BlockSpec((1, d), lambda m, f: (0, 0)),
            pl.BlockSpec((None, d, tf), lambda m, f: (layer, 0, f)),
            pl.BlockSpec((None, d, tf), lambda m, f: (layer, 0, f)),
            pl.BlockSpec((None, tf, d), lambda m, f: (layer, f, 0)),
        ],
        out_specs=pl.BlockSpec((tm, d), lambda m, f: (m, 0)),
        out_shape=jax.ShapeDtypeStruct((n, d), F32),
        scratch_shapes=[pltpu.VMEM((tm, d), BF), pltpu.VMEM((tm, d), F32)],
        compiler_params=_cparams("parallel", "arbitrary"),
        name="ffn",
    )(x, *mix_args, g.reshape(1, d), wg, wu, wd)


def _inproj_kernel(x_ref, g_ref, w_ref, o_ref):
    hn = (_rms(x_ref[...], -1) * g_ref[...]).astype(BF)
    o_ref[...] = jnp.dot(hn, w_ref[...], preferred_element_type=F32)


def _inproj(x, g, w):
    n, d = x.shape
    nout = w.shape[1]
    tm = _pick_tile(n, 640)
    return pl.pallas_call(
        _inproj_kernel,
        grid=(n // tm,),
        in_specs=[
            pl.BlockSpec((tm, d), lambda m: (m, 0)),
            pl.BlockSpec((1, d), lambda m: (0, 0)),
            pl.BlockSpec((d, nout), lambda m: (0, 0)),
        ],
        out_specs=pl.BlockSpec((tm, nout), lambda m: (m, 0)),
        out_shape=jax.ShapeDtypeStruct((n, nout), F32),
        compiler_params=_cparams("parallel"),
        name="inproj",
    )(x, g.reshape(1, d), w)


def _final_norm_kernel(x_ref, g_ref, o_ref):
    o_ref[...] = _rms(x_ref[...], -1) * g_ref[...]


def _final_norm(x, g, *, row0, nrows):
    d = x.shape[1]
    tm = _pick_tile(math.gcd(nrows, row0) if row0 else nrows, 1024)
    rb = row0 // tm
    return pl.pallas_call(
        _final_norm_kernel,
        grid=(nrows // tm,),
        in_specs=[pl.BlockSpec((tm, d), lambda m: (rb + m, 0)), pl.BlockSpec((1, d), lambda m: (0, 0))],
        out_specs=pl.BlockSpec((tm, d), lambda m: (m, 0)),
        out_shape=jax.ShapeDtypeStruct((nrows, d), F32),
        compiler_params=_cparams("parallel"),
        name="final_norm",
    )(x, g.reshape(1, d))


def _pool_kernel(pc_ref, pp_ref, w_ref, sc_ref, dst_ref, o_ref, *, tm, pos0, prev_at_first):
    del dst_ref
    i = pl.program_id(1)
    p = pc_ref[...]
    prev = pp_ref[...]
    if not prev_at_first:
        prev = jnp.where(i > 0, prev, 0.0)
    ext = jnp.concatenate([prev, p], axis=0)
    s2 = ext + pltpu.roll(ext, 1, 0)
    s4 = s2 + pltpu.roll(s2, 2, 0)
    s8 = s4 + pltpu.roll(s4, 4, 0)
    s16 = s8 + pltpu.roll(s8, 8, 0)
    grp = _shr(lax.broadcasted_iota(jnp.int32, (tm, POOL_WIDTH), 1), POOL_GROUP_DIM)
    pos = pos0 + i * tm + lax.broadcasted_iota(jnp.int32, (tm, POOL_WIDTH), 0)
    sums = (s2, s4, s8, s16)
    s = sums[3][POOL_PAD:]
    win = jnp.full((tm, POOL_WIDTH), POOL_WINDOWS[3], jnp.int32)
    for gi in (2, 1, 0):
        s = jnp.where(grp == gi, sums[gi][POOL_PAD:], s)
        win = jnp.where(grp == gi, POOL_WINDOWS[gi], win)
    cnt = jnp.minimum(pos + 1, win).astype(F32)
    diff = s / cnt - p
    y = jnp.dot(diff.astype(BF), w_ref[...], preferred_element_type=F32) * sc_ref[...]
    o_ref[...] = y.astype(BF)


def _pool(z, prev_src, w_bd, scale, dst, *, nb, t, row0, col_blk, prev_is_state, pos0):
    tm = _pick_tile(t, 512)
    nt = t // tm
    rb0 = row0 // tm
    if prev_is_state:
        assert nt == 1
        prev_spec = pl.BlockSpec((POOL_PAD, POOL_WIDTH), lambda b, i: (b, 0))
    else:
        r16 = tm // POOL_PAD
        base16 = row0 // POOL_PAD
        prev_spec = pl.BlockSpec(
            (POOL_PAD, POOL_WIDTH),
            lambda b, i: (jnp.maximum(base16 + (b * nt + i) * r16 - 1, 0), col_blk))
    return pl.pallas_call(
        functools.partial(_pool_kernel, tm=tm, pos0=pos0, prev_at_first=prev_is_state),
        grid=(nb, nt),
        in_specs=[
            pl.BlockSpec((tm, POOL_WIDTH), lambda b, i: (rb0 + b * nt + i, col_blk)),
            prev_spec,
            pl.BlockSpec((POOL_WIDTH, POOL_WIDTH), lambda b, i: (0, 0)),
            pl.BlockSpec((1, POOL_WIDTH), lambda b, i: (0, 0)),
            ANY_SPEC,
        ],
        out_specs=pl.BlockSpec((tm, POOL_WIDTH), lambda b, i: (rb0 + b * nt + i, 0)),
        out_shape=jax.ShapeDtypeStruct(dst.shape, dst.dtype),
        input_output_aliases={4: 0},
        compiler_params=_cparams("parallel", "arbitrary"),
        name="pool",
    )(z, prev_src, w_bd, scale.reshape(1, POOL_WIDTH), dst)


def _rope_tables(pos):
    half = DA_HEAD_DIM // 2
    inv = ROPE_THETA ** (-jnp.arange(half, dtype=F32) / half)
    ang = pos.astype(F32)[:, None] * inv[None, :]
    cos, sin = jnp.cos(ang), jnp.sin(ang)
    cos128 = jnp.concatenate([cos, cos, cos, cos], axis=1)
    sin128 = jnp.concatenate([-sin, sin, -sin, sin], axis=1)
    return cos128, sin128


def _rope128(x, cos, sin):
    lane = lax.broadcasted_iota(jnp.int32, x.shape, 1)
    first = (lane & (DA_HEAD_DIM - 1)) < (DA_HEAD_DIM // 2)
    partner = jnp.where(first, pltpu.roll(x, LANES - 32, 1), pltpu.roll(x, 32, 1))
    return x * cos + partner * sin


def _prep_prompt_kernel(zq_ref, zk_ref, zv_ref, cos_ref, sin_ref, kt_dst_ref, vh_dst_ref,
                        qt_ref, kb_ref, vt_ref, kn_ref, kt_ref, vh_ref):
    del kt_dst_ref, vh_dst_ref
    cos, sin = cos_ref[...], sin_ref[...]
    scale = DA_HEAD_DIM ** -0.5 * LOG2E
    ones = jnp.ones((DA_VT_ROWS - DA_PAIR, zq_ref.shape[0]), BF)
    for h in range(DA_HEADS):
        sl = slice(h * DA_PAIR, (h + 1) * DA_PAIR)
        q = _rope128(zq_ref[:, sl], cos, sin) * scale
        qt_ref[h] = q.T.astype(BF)
        k = _rope128(zk_ref[:, sl], cos, sin)
        kt_ref[h] = k.T
        kb = k.astype(BF)
        kb_ref[h] = kb
        kf = kb.astype(F32)
        kn2 = jnp.max(jnp.sum(kf * kf, axis=1, keepdims=True), axis=0, keepdims=True)
        kn_ref[h] = jnp.broadcast_to(kn2, kn_ref.shape[1:])
        v = zv_ref[:, sl]
        vh_ref[h] = v
        vt_ref[h, :DA_PAIR] = v.T.astype(BF)
        vt_ref[h, DA_PAIR:] = ones


def _prep_prompt(z, cos, sin, kt_all, vh_all, *, layer, nb, t, tk):
    nt = t // tk
    return pl.pallas_call(
        _prep_prompt_kernel,
        grid=(nb, nt),
        in_specs=[
            pl.BlockSpec((tk, DA_WIDTH), lambda b, i: (b * nt + i, 0)),
            pl.BlockSpec((tk, DA_WIDTH), lambda b, i: (b * nt + i, 1)),
            pl.BlockSpec((tk, DA_WIDTH), lambda b, i: (b * nt + i, 2)),
            pl.BlockSpec((tk, LANES), lambda b, i: (i, 0)),
            pl.BlockSpec((tk, LANES), lambda b, i: (i, 0)),
            ANY_SPEC,
            ANY_SPEC,
        ],
        out_specs=[
            pl.BlockSpec((None, DA_HEADS, DA_PAIR, tk), lambda b, i: (b, 0, 0, i)),
            pl.BlockSpec((None, DA_HEADS, tk, DA_PAIR), lambda b, i: (b, 0, i, 0)),
            pl.BlockSpec((None, DA_HEADS, None, DA_VT_ROWS, tk), lambda b, i: (b, 0, i, 0, 0)),
            pl.BlockSpec((None, DA_HEADS, None, 8, LANES), lambda b, i: (b, 0, i, 0, 0)),
            pl.BlockSpec((None, None, DA_HEADS, DA_PAIR, tk), lambda b, i: (b, layer, 0, 0, i)),
            pl.BlockSpec((None, None, DA_HEADS, tk, DA_PAIR), lambda b, i: (b, layer, 0, i, 0)),
        ],
        out_shape=[
            jax.ShapeDtypeStruct((nb, DA_HEADS, DA_PAIR, t), BF),
            jax.ShapeDtypeStruct((nb, DA_HEADS, t, DA_PAIR), BF),
            jax.ShapeDtypeStruct((nb, DA_HEADS, nt, DA_VT_ROWS, tk), BF),
            jax.ShapeDtypeStruct((nb, DA_HEADS, nt, 8, LANES), F32),
            jax.ShapeDtypeStruct(kt_all.shape, kt_all.dtype),
            jax.ShapeDtypeStruct(vh_all.shape, vh_all.dtype),
        ],
        input_output_aliases={5: 4, 6: 5},
        compiler_params=_cparams("parallel", "parallel"),
        name="prep_prompt",
    )(z, z, z, cos, sin, kt_all, vh_all)


def _lambda_value(lamv_ref, lam_init):
    lv = lamv_ref[...]
    s01 = jnp.sum(lv[0:1] * lv[1:2], axis=1, keepdims=True)
    s23 = jnp.sum(lv[2:3] * lv[3:4], axis=1, keepdims=True)
    return jnp.exp(s01) - jnp.exp(s23) + lam_init


def _attn_prompt_kernel(qt_ref, k_ref, vt_ref, kn_ref, lamv_ref, g_ref, dst_ref, o_ref,
                        qx_ref, sa_ref, sb_ref, m_ref, acc_ref, *, tq, tk, lam_init):
    del dst_ref
    ndiag = tq // tk
    qi = pl.program_id(2)
    nfull = qi * ndiag
    qt = qt_ref[...]
    row = lax.broadcasted_iota(jnp.int32, qt.shape, 0)
    zero = jnp.zeros_like(qt)
    qx_ref[:, :tq] = jnp.where(row < DA_HEAD_DIM, qt, zero)
    qx_ref[:, tq:] = jnp.where(row >= DA_HEAD_DIM, qt, zero)
    qf = qt.astype(F32)
    qn2 = jnp.max(jnp.sum(qf * qf, axis=0, keepdims=True))
    bounded = qn2 * jnp.max(kn_ref[...]) < DA_SAFE_LOG2 ** 2

    def keys(j):
        return k_ref[pl.ds(pl.multiple_of(j * tk, tk), tk), :]

    def diagonal_mask(shape, c, col0):
        kpos = c * tk + lax.broadcasted_iota(jnp.int32, shape, 0)
        qpos = (col0 + lax.broadcasted_iota(jnp.int32, shape, 1)) & (tq - 1)
        return kpos <= qpos

    def plain_chunk(j):
        p = jnp.exp2(jnp.dot(keys(j), qx_ref[...], preferred_element_type=F32))
        acc_ref[...] += jnp.dot(vt_ref[j], p.astype(BF), preferred_element_type=F32)

    def plain_diagonal(c):
        kj, vj = keys(nfull + c), vt_ref[nfull + c]
        for half in range(2):
            cols = slice(half * tq + c * tk, (half + 1) * tq)
            p = jnp.exp2(jnp.dot(kj, qx_ref[:, cols], preferred_element_type=F32))
            p = jnp.where(diagonal_mask(p.shape, c, c * tk), p, 0.0)
            d = jnp.dot(vj, p.astype(BF), preferred_element_type=F32)
            acc_ref[:, cols] = d if c == 0 else acc_ref[:, cols] + d

    @pl.when(bounded)
    def _():
        for c in range(ndiag):
            plain_diagonal(c)

        @pl.when((nfull & 1) == 1)
        def _():
            plain_chunk(0)

        def pair(pi, carry):
            j = (nfull & 1) + 2 * pi
            plain_chunk(j)
            plain_chunk(j + 1)
            return carry

        lax.fori_loop(0, nfull >> 1, pair, 0)

    def scores(j):
        return jnp.dot(keys(j), qx_ref[...], preferred_element_type=F32)

    def update(s, j):
        m_prev = m_ref[...]
        m_new = jnp.maximum(m_prev, jnp.max(s, axis=0, keepdims=True))
        p = jnp.exp2(s - m_new).astype(BF)
        acc_ref[...] = jnp.exp2(m_prev - m_new) * acc_ref[...] + jnp.dot(
            vt_ref[j], p, preferred_element_type=F32)
        m_ref[...] = m_new

    @pl.when(jnp.logical_not(bounded))
    def _():
        s = scores(nfull)
        s = jnp.where(diagonal_mask(s.shape, 0, 0), s, -jnp.inf)
        m0 = jnp.max(s, axis=0, keepdims=True)
        m_ref[...] = m0
        acc_ref[...] = jnp.dot(vt_ref[nfull], jnp.exp2(s - m0).astype(BF),
                               preferred_element_type=F32)
        for c in range(1, ndiag):
            s = scores(nfull + c)
            update(jnp.where(diagonal_mask(s.shape, c, 0), s, -jnp.inf), nfull + c)
        first = nfull & 1

        @pl.when(first == 1)
        def _():
            update(scores(0), 0)

        @pl.when(nfull >= 2)
        def _():
            sa_ref[...] = scores(first)

        def pair(pi, carry):
            j = first + 2 * pi
            sb_ref[...] = scores(j + 1)
            update(sa_ref[...], j)
            sa_ref[...] = scores(jnp.minimum(j + 2, nfull - 1))
            update(sb_ref[...], j + 1)
            return carry

        lax.fori_loop(0, nfull >> 1, pair, 0)

    acc = acc_ref[...]
    on = acc[:DA_PAIR] / acc[DA_PAIR:DA_PAIR + 1]
    lam = _lambda_value(lamv_ref, lam_init)
    ot = on[:, :tq] - lam * on[:, tq:]
    ot = _rms(ot, 0) * g_ref[...] * (1.0 - lam_init)
    o_ref[...] = ot.T.astype(BF)


def _attn_prompt(qt, kb, vt, kn2, lamv, subln_g, dst, *, tq, lam_init):
    nb, nh, _, t = qt.shape
    nk, tk = vt.shape[2], vt.shape[4]
    nq = t // tq
    return pl.pallas_call(
        functools.partial(_attn_prompt_kernel, tq=tq, tk=tk, lam_init=lam_init),
        grid=(nb, nh, nq),
        in_specs=[
            pl.BlockSpec((None, None, DA_PAIR, tq), lambda b, h, i: (b, h, 0, i)),
            pl.BlockSpec((None, None, t, DA_PAIR), lambda b, h, i: (b, h, 0, 0)),
            pl.BlockSpec((None, None, nk, DA_VT_ROWS, tk), lambda b, h, i: (b, h, 0, 0, 0)),
            pl.BlockSpec((None, None, nk, 8, LANES), lambda b, h, i: (b, h, 0, 0, 0)),
            pl.BlockSpec((4, DA_HEAD_DIM), lambda b, h, i: (0, 0)),
            pl.BlockSpec((DA_PAIR, 1), lambda b, h, i: (0, 0)),
            ANY_SPEC,
        ],
        out_specs=pl.BlockSpec((tq, DA_PAIR), lambda b, h, i: (b * nq + i, h)),
        out_shape=jax.ShapeDtypeStruct(dst.shape, dst.dtype),
        scratch_shapes=[pltpu.VMEM((DA_PAIR, 2 * tq), BF),
                        pltpu.VMEM((tk, 2 * tq), F32), pltpu.VMEM((tk, 2 * tq), F32),
                        pltpu.VMEM((1, 2 * tq), F32), pltpu.VMEM((DA_VT_ROWS, 2 * tq), F32)],
        input_output_aliases={6: 0},
        compiler_params=_cparams("parallel", "parallel", "arbitrary"),
        name="attn_prompt",
    )(qt, kb, vt, kn2, lamv, subln_g.reshape(DA_PAIR, 1), dst)


def _rope_rows_kernel(zq_ref, zk_ref, cos_ref, sin_ref, q_ref, k_ref):
    cos, sin = cos_ref[...], sin_ref[...]
    scale = DA_HEAD_DIM ** -0.5
    for h in range(DA_HEADS):
        sl = slice(h * DA_PAIR, (h + 1) * DA_PAIR)
        q_ref[:, sl] = (_rope128(zq_ref[:, sl], cos, sin) * scale).astype(BF)
        k_ref[:, sl] = _rope128(zk_ref[:, sl], cos, sin)


def _rope_rows(z, cos, sin, *, row0, nrows):
    rb = row0 // nrows
    return pl.pallas_call(
        _rope_rows_kernel,
        grid=(1,),
        in_specs=[
            pl.BlockSpec((nrows, DA_WIDTH), lambda i: (rb, 0)),
            pl.BlockSpec((nrows, DA_WIDTH), lambda i: (rb, 1)),
            pl.BlockSpec((nrows, LANES), lambda i: (0, 0)),
            pl.BlockSpec((nrows, LANES), lambda i: (0, 0)),
        ],
        out_specs=[pl.BlockSpec((nrows, DA_WIDTH), lambda i: (0, 0)),
                   pl.BlockSpec((nrows, DA_WIDTH), lambda i: (0, 0))],
        out_shape=[jax.ShapeDtypeStruct((nrows, DA_WIDTH), BF),
                   jax.ShapeDtypeStruct((nrows, DA_WIDTH), F32)],
        compiler_params=_cparams("arbitrary"),
        name="rope_rows",
    )(z, z, cos, sin)


def _attn_sample_kernel(pt_ref, q_ref, kn_ref, vn_ref, sel_ref, lamv_ref, g_ref, *rest,
                        pp, nsteps, ts, lam_init):
    kpages, vpages = rest[:pp], rest[pp:2 * pp]
    _, o_ref, m_ref, l_ref, acc_ref, qbd_ref = rest[2 * pp:]
    s = pl.program_id(1)
    nrow = 2 * DA_HEADS * ts

    @pl.when(s == 0)
    def _():
        q = q_ref[...]
        qrep = jnp.concatenate([q] * (2 * DA_HEADS), axis=0)
        qbd_ref[...] = jnp.where(sel_ref[...] > 0, qrep, jnp.zeros_like(qrep))
        m_ref[...] = jnp.full(m_ref.shape, -jnp.inf, F32)
        l_ref[...] = jnp.zeros_like(l_ref)
        acc_ref[...] = jnp.zeros_like(acc_ref)

    def update(sc, vv, causal):
        if causal:
            tq = lax.broadcasted_iota(jnp.int32, sc.shape, 0) & (ts - 1)
            tk = lax.broadcasted_iota(jnp.int32, sc.shape, 1)
            sc = jnp.where(tk <= tq, sc, -jnp.inf)
        m_prev = m_ref[...]
        m_new = jnp.maximum(m_prev, jnp.max(sc, axis=1, keepdims=True))
        alpha = jnp.exp(m_prev - m_new)
        p = jnp.exp(sc - m_new)
        l_ref[...] = alpha * l_ref[...] + jnp.sum(p, axis=1, keepdims=True)
        acc_ref[...] = alpha * acc_ref[...] + jnp.dot(p.astype(BF), vv, preferred_element_type=F32)
        m_ref[...] = m_new

    if pp:
        kt = jnp.concatenate([r[...] for r in kpages], axis=1).astype(BF)
        vv = jnp.concatenate(
            [jnp.concatenate([r[h] for h in range(DA_HEADS)], axis=1) for r in vpages],
            axis=0).astype(BF)
        update(jnp.dot(qbd_ref[...], kt, preferred_element_type=F32), vv, False)

    @pl.when(s == nsteps - 1)
    def _():
        sc_new = lax.dot_general(qbd_ref[...], kn_ref[...].astype(BF), (((1,), (1,)), ((), ())),
                                 preferred_element_type=F32)
        update(sc_new, vn_ref[...].astype(BF), True)
        on = acc_ref[...] / l_ref[...]
        lam = _lambda_value(lamv_ref, lam_init)
        half = nrow // 2
        d = on[:half] - lam * on[half:]
        outs = []
        for h in range(DA_HEADS):
            blk = d[h * ts:(h + 1) * ts, h * DA_PAIR:(h + 1) * DA_PAIR]
            outs.append(_rms(blk, -1) * g_ref[...] * (1.0 - lam_init))
        o_ref[...] = jnp.concatenate(outs, axis=1).astype(BF)


def _attn_sample(q_s, k_new, z, cache_kt, cache_vh, page_table, lamv, subln_g, dst, *,
                 ts, layer, row0, lam_init):
    nb = q_s.shape[0] // ts
    n_pages = page_table.shape[1]
    pp = 0
    for cand in (16, 8, 4, 2, 1):
        if n_pages and n_pages % cand == 0:
            pp = cand
            break
    nsteps = max(n_pages // pp, 1) if pp else 1
    nrow = 2 * DA_HEADS * ts
    r = np.arange(nrow)[:, None] // ts
    c = np.arange(DA_WIDTH)[None, :] // DA_HEAD_DIM
    sel = jnp.asarray(((r % DA_HEADS) * 2 + r // DA_HEADS == c).astype(np.float32))
    rb = row0 // ts

    def kpage_spec(i):
        return pl.BlockSpec((None, None, DA_WIDTH, PAGE_SIZE),
                            lambda b, s, pt: (pt[b * n_pages + s * pp + i], layer, 0, 0))

    def vpage_spec(i):
        return pl.BlockSpec((None, None, DA_HEADS, PAGE_SIZE, DA_PAIR),
                            lambda b, s, pt: (pt[b * n_pages + s * pp + i], layer, 0, 0, 0))

    in_specs = [
        pl.BlockSpec((ts, DA_WIDTH), lambda b, s, pt: (b, 0)),
        pl.BlockSpec((ts, DA_WIDTH), lambda b, s, pt: (b, 0)),
        pl.BlockSpec((ts, DA_WIDTH), lambda b, s, pt: (rb + b, 2)),
        pl.BlockSpec((nrow, DA_WIDTH), lambda b, s, pt: (0, 0)),
        pl.BlockSpec((4, DA_HEAD_DIM), lambda b, s, pt: (0, 0)),
        pl.BlockSpec((1, DA_PAIR), lambda b, s, pt: (0, 0)),
    ] + [kpage_spec(i) for i in range(pp)] + [vpage_spec(i) for i in range(pp)] + [ANY_SPEC]
    grid_spec = pltpu.PrefetchScalarGridSpec(
        num_scalar_prefetch=1,
        grid=(nb, nsteps),
        in_specs=in_specs,
        out_specs=pl.BlockSpec((ts, DA_WIDTH), lambda b, s, pt: (rb + b, 0)),
        scratch_shapes=[pltpu.VMEM((nrow, 1), F32), pltpu.VMEM((nrow, 1), F32),
                        pltpu.VMEM((nrow, DA_WIDTH), F32), pltpu.VMEM((nrow, DA_WIDTH), BF)],
    )
    return pl.pallas_call(
        functools.partial(_attn_sample_kernel, pp=pp, nsteps=nsteps, ts=ts, lam_init=lam_init),
        grid_spec=grid_spec,
        out_shape=jax.ShapeDtypeStruct(dst.shape, dst.dtype),
        input_output_aliases={7 + 2 * pp: 0},
        compiler_params=_cparams("parallel", "arbitrary"),
        name="attn_sample",
    )(page_table.reshape(-1), q_s, k_new, z, sel, lamv, subln_g.reshape(1, DA_PAIR),
      *([cache_kt] * pp), *([cache_vh] * pp), dst)


def _sgu_norm_v(sv_ref, ng_ref, g):
    x = jax.nn.gelu(sv_ref[:, g * SGU_GROUP_DIM:(g + 1) * SGU_GROUP_DIM])
    return _rms(x, -1) * ng_ref[g:g + 1, :]


def _sgu_prompt_kernel(u_ref, sv_ref, ng_ref, w_ref, bs_ref, dst_ref, y_ref, vr_ref, *, tm, c):
    del dst_ref
    last = pl.program_id(1) == pl.num_programs(1) - 1
    ri = lax.broadcasted_iota(jnp.int32, (c, c), 0)
    ci = lax.broadcasted_iota(jnp.int32, (c, c), 1)
    for g in range(SGU_GROUPS):
        sl = slice(g * SGU_GROUP_DIM, (g + 1) * SGU_GROUP_DIM)
        v = _sgu_norm_v(sv_ref, ng_ref, g)

        @pl.when(last)
        def _():
            vr_ref[:, sl] = v[tm - c:]

        vb = v.astype(BF)
        w = jnp.where(ri >= ci, w_ref[g], 0.0).astype(BF)
        for n in range(tm // c):
            rows = slice(n * c, (n + 1) * c)
            mixed = jnp.dot(w, vb[rows], preferred_element_type=F32) + bs_ref[g]
            y_ref[rows, sl] = (jax.nn.gelu(u_ref[rows, sl]) * mixed).astype(BF)


def _sgu_prompt(z, norm_g, w_s, b_s, dst, *, nb, t):
    c = SGU_CHUNK
    tm = _pick_tile(t, 512, c)
    nt = t // tm
    bs = jnp.broadcast_to(b_s[:, :c, None], (SGU_GROUPS, c, SGU_GROUP_DIM))
    return pl.pallas_call(
        functools.partial(_sgu_prompt_kernel, tm=tm, c=c),
        grid=(nb, nt),
        in_specs=[
            pl.BlockSpec((tm, SGU_WIDTH), lambda b, i: (b * nt + i, 0)),
            pl.BlockSpec((tm, SGU_WIDTH), lambda b, i: (b * nt + i, 1)),
            pl.BlockSpec((SGU_GROUPS, SGU_GROUP_DIM), lambda b, i: (0, 0)),
            pl.BlockSpec((SGU_GROUPS, c, c), lambda b, i: (0, 0, 0)),
            pl.BlockSpec((SGU_GROUPS, c, SGU_GROUP_DIM), lambda b, i: (0, 0, 0)),
            ANY_SPEC,
        ],
        out_specs=[
            pl.BlockSpec((tm, SGU_WIDTH), lambda b, i: (b * nt + i, 0)),
            pl.BlockSpec((None, c, SGU_WIDTH), lambda b, i: (b, 0, 0)),
        ],
        out_shape=[jax.ShapeDtypeStruct(dst.shape, dst.dtype),
                   jax.ShapeDtypeStruct((nb, c, SGU_WIDTH), F32)],
        input_output_aliases={5: 0},
        compiler_params=_cparams("parallel", "arbitrary"),
        name="sgu_prompt",
    )(z, z, norm_g, w_s[:, :c, :c], bs, dst)


def _sgu_sample_kernel(u_ref, sv_ref, ng_ref, w_ref, bs_ref, dst_ref, y_ref, vr_ref, *, nb, ts):
    del dst_ref
    v = jnp.concatenate([_sgu_norm_v(sv_ref, ng_ref, g) for g in range(SGU_GROUPS)], axis=1)
    vr_ref[...] = v
    v3 = v.reshape(nb, ts, SGU_WIDTH)
    ii = lax.broadcasted_iota(jnp.int32, (ts, SGU_WIDTH), 0)
    mixed = jnp.broadcast_to(bs_ref[...][None], (nb, ts, SGU_WIDTH))
    for j in range(ts):
        wj = jnp.where(ii >= j, w_ref[j], 0.0)
        mixed = mixed + wj[None] * v3[:, j:j + 1, :]
    y = jax.nn.gelu(u_ref[...]).reshape(nb, ts, SGU_WIDTH) * mixed
    y_ref[...] = y.reshape(nb * ts, SGU_WIDTH).astype(BF)


def _sgu_sample(z, norm_g, w_s, b_s, dst, *, nb, ts, row0):
    nrows = nb * ts
    rb = row0 // nrows
    w_exp = jnp.repeat(jnp.transpose(w_s[:, :ts, :ts], (2, 1, 0)), SGU_GROUP_DIM, axis=2)
    b_exp = jnp.repeat(jnp.transpose(b_s[:, :ts], (1, 0)), SGU_GROUP_DIM, axis=1)
    return pl.pallas_call(
        functools.partial(_sgu_sample_kernel, nb=nb, ts=ts),
        grid=(1,),
        in_specs=[
            pl.BlockSpec((nrows, SGU_WIDTH), lambda i: (rb, 0)),
            pl.BlockSpec((nrows, SGU_WIDTH), lambda i: (rb, 1)),
            pl.BlockSpec((SGU_GROUPS, SGU_GROUP_DIM), lambda i: (0, 0)),
            pl.BlockSpec((ts, ts, SGU_WIDTH), lambda i: (0, 0, 0)),
            pl.BlockSpec((ts, SGU_WIDTH), lambda i: (0, 0)),
            ANY_SPEC,
        ],
        out_specs=[pl.BlockSpec((nrows, SGU_WIDTH), lambda i: (rb, 0)),
                   pl.BlockSpec((nrows, SGU_WIDTH), lambda i: (0, 0))],
        out_shape=[jax.ShapeDtypeStruct(dst.shape, dst.dtype),
                   jax.ShapeDtypeStruct((nrows, SGU_WIDTH), F32)],
        input_output_aliases={5: 0},
        compiler_params=_cparams("arbitrary"),
        name="sgu_sample",
    )(z, z, norm_g, w_exp, b_exp, dst)


def _log_sigmoid(x):
    return jnp.minimum(x, 0.0) - jnp.log1p(jnp.exp(-jnp.abs(x)))


def _seg_scan(x, seg, pos, reverse):
    n = x.shape[0]
    r = pos & (seg - 1)
    t = 1
    while t < seg:
        if reverse:
            x = x + jnp.where(r < seg - t, pltpu.roll(x, n - t, 0), 0.0)
        else:
            x = x + jnp.where(r >= t, pltpu.roll(x, t, 0), 0.0)
        t *= 2
    return x


def _gla_decays(gk, c):
    n, w = gk.shape
    pos = lax.broadcasted_iota(jnp.int32, gk.shape, 0)
    out = {1: (jnp.exp(gk), None)}
    s = 2
    while s <= c:
        if s <= 8 or n % s:
            pre = _seg_scan(gk, s, pos, False)
            suf = _seg_scan(gk, s, pos, True) - gk
        else:
            h = s // 2
            p4 = pre.reshape(n // s, 2, h, w)
            s4 = suf.reshape(n // s, 2, h, w)
            tot = p4[:, :, h - 1:h, :]
            pre = jnp.concatenate([p4[:, 0:1], p4[:, 1:2] + tot[:, 0:1]], axis=1).reshape(n, w)
            suf = jnp.concatenate([s4[:, 0:1] + tot[:, 1:2], s4[:, 1:2]], axis=1).reshape(n, w)
        out[s] = (jnp.exp(pre), jnp.exp(suf))
        s *= 2
    return out


def _gla_intra(q, k, v, dec, c, sc):
    n = q.shape[0]
    ri = lax.broadcasted_iota(jnp.int32, (sc, sc), 0)
    ci = lax.broadcasted_iota(jnp.int32, (sc, sc), 1)
    same_chunk = _shr(ri, c) == _shr(ci, c)
    lane = lax.broadcasted_iota(jnp.int32, (sc, LANES), 1)
    levels = []
    levels.append((ri == ci, q.astype(BF), k.astype(BF)))
    s = 1
    while s < c:
        pre, _ = dec[s]
        suf = dec[s][1]
        ks = k if suf is None else k * suf
        mask = same_chunk & (_shr(ri, 2 * s) == _shr(ci, 2 * s)) & ((_shr(ri, s) & 1) == 1) \
            & ((_shr(ci, s) & 1) == 0)
        levels.append((mask, (q * pre).astype(BF), ks.astype(BF)))
        s *= 2
    outs = []
    for h in range(GLA_HEADS):
        pair = slice((h // 2) * LANES, (h // 2 + 1) * LANES)
        own = _shr(lane, GLA_K_DIM) == (h % 2)
        vh = v[:, h * GLA_V_DIM:(h + 1) * GLA_V_DIM].astype(BF)
        rows_out = []
        for b0 in range(0, n, sc):
            rows = slice(b0, b0 + sc)
            a = jnp.zeros((sc, sc), F32)
            for mask, qs, ks in levels:
                qm = jnp.where(own, qs[rows, pair], jnp.zeros((sc, LANES), BF))
                al = lax.dot_general(qm, ks[rows, pair], (((1,), (1,)), ((), ())),
                                     preferred_element_type=F32)
                a = a + jnp.where(mask, al, 0.0)
            rows_out.append(jnp.dot(a.astype(BF), vh[rows], preferred_element_type=F32))
        outs.append(jnp.concatenate(rows_out, axis=0) if len(rows_out) > 1 else rows_out[0])
    return outs


def _gla_gate(glr_ref, w2_ref, gb_ref):
    pre = jnp.dot(glr_ref[...].astype(BF), w2_ref[...], preferred_element_type=F32) + gb_ref[...]
    return _log_sigmoid(pre) / GLA_GATE_NORMALIZER


def _gla_finish(o_heads, g_ref, gn_ref, o_ref):
    for h in range(GLA_HEADS):
        sl = slice(h * GLA_V_DIM, (h + 1) * GLA_V_DIM)
        gate = g_ref[:, sl]
        o_ref[:, sl] = (_rms(o_heads[h], -1) * gn_ref[...] * (gate * jax.nn.sigmoid(gate))).astype(BF)


def _gla_prompt_kernel(q_ref, k_ref, v_ref, g_ref, glr_ref, w2_ref, gb_ref, gn_ref, dst_ref,
                       o_ref, s_out_ref, st_ref, *, tm, c):
    del dst_ref
    i = pl.program_id(1)

    @pl.when(i == 0)
    def _():
        st_ref[...] = jnp.zeros_like(st_ref)

    gk = _gla_gate(glr_ref, w2_ref, gb_ref)
    q = q_ref[...] * (GLA_K_DIM ** -0.5)
    k = k_ref[...]
    v = v_ref[...]
    dec = _gla_decays(gk, c)
    o_heads = _gla_intra(q, k, v, dec, c, min(tm, LANES))
    eb, esuf = dec[c]
    qd = (q * eb).astype(BF)
    kd = (k * esuf).astype(BF)
    lane = lax.broadcasted_iota(jnp.int32, (c, LANES), 1)
    for h in range(GLA_HEADS):
        pair = slice((h // 2) * LANES, (h // 2 + 1) * LANES)
        own = _shr(lane, GLA_K_DIM) == (h % 2)
        vh = v[:, h * GLA_V_DIM:(h + 1) * GLA_V_DIM].astype(BF)
        st = st_ref[h]
        inter = []
        for n in range(tm // c):
            rows = slice(n * c, (n + 1) * c)
            qm = jnp.where(own, qd[rows, pair], jnp.zeros((c, LANES), BF))
            inter.append(lax.dot_general(qm, st.astype(BF), (((1,), (1,)), ((), ())),
                                         preferred_element_type=F32))
            upd = lax.dot_general(vh[rows], kd[rows, pair], (((0,), (0,)), ((), ())),
                                  preferred_element_type=F32)
            st = eb[n * c + c - 1:n * c + c, pair] * st + upd
        st_ref[h] = st
        o_heads[h] = o_heads[h] + jnp.concatenate(inter, axis=0)
    _gla_finish(o_heads, g_ref, gn_ref, o_ref)

    @pl.when(i == pl.num_programs(1) - 1)
    def _():
        for h in range(GLA_HEADS):
            lo = (h % 2) * GLA_K_DIM
            s_out_ref[h] = st_ref[h].T[lo:lo + GLA_K_DIM, :]


def _gla_prompt(z, w2, gate_b, gla_norm, dst, *, nb, t):
    c = GLA_CHUNK
    tm = _pick_tile(t, 256, LANES)
    nt = t // tm
    return pl.pallas_call(
        functools.partial(_gla_prompt_kernel, tm=tm, c=c),
        grid=(nb, nt),
        in_specs=[
            pl.BlockSpec((tm, GLA_K_WIDTH), lambda b, i: (b * nt + i, 4)),
            pl.BlockSpec((tm, GLA_K_WIDTH), lambda b, i: (b * nt + i, 5)),
            pl.BlockSpec((tm, GLA_V_WIDTH), lambda b, i: (b * nt + i, 3)),
            pl.BlockSpec((tm, GLA_V_WIDTH), lambda b, i: (b * nt + i, 4)),
            pl.BlockSpec((tm, GLA_RANK_PAD), lambda b, i: (b * nt + i, 20)),
            pl.BlockSpec((GLA_RANK_PAD, GLA_K_WIDTH), lambda b, i: (0, 0)),
            pl.BlockSpec((1, GLA_K_WIDTH), lambda b, i: (0, 0)),
            pl.BlockSpec((1, GLA_V_DIM), lambda b, i: (0, 0)),
            ANY_SPEC,
        ],
        out_specs=[
            pl.BlockSpec((tm, GLA_V_WIDTH), lambda b, i: (b * nt + i, 0)),
            pl.BlockSpec((None, GLA_HEADS, GLA_K_DIM, GLA_V_DIM), lambda b, i: (b, 0, 0, 0)),
        ],
        out_shape=[jax.ShapeDtypeStruct(dst.shape, dst.dtype),
                   jax.ShapeDtypeStruct((nb, GLA_HEADS, GLA_K_DIM, GLA_V_DIM), F32)],
        scratch_shapes=[pltpu.VMEM((GLA_HEADS, GLA_V_DIM, LANES), F32)],
        input_output_aliases={8: 0},
        compiler_params=_cparams("parallel", "arbitrary"),
        name="gla_prompt",
    )(z, z, z, z, z, w2, gate_b.reshape(1, GLA_K_WIDTH), gla_norm.reshape(1, GLA_V_DIM), dst)


def _gla_sample_kernel(q_ref, k_ref, v_ref, g_ref, glr_ref, w2_ref, gb_ref, gbc_ref, gn_ref, s0_ref,
                       dst_ref, o_ref, s_out_ref, *, ts):
    del dst_ref
    gk = _gla_gate(glr_ref, w2_ref, gb_ref)
    q = q_ref[...] * (GLA_K_DIM ** -0.5)
    k = k_ref[...]
    v = v_ref[...]
    dec = _gla_decays(gk, ts)
    o_heads = _gla_intra(q, k, v, dec, ts, ts)
    eb, esuf = dec[ts]
    qd = (q * eb).astype(BF)
    kd = (k * esuf).astype(BF)
    pre_t = lax.dot_general(w2_ref[...], glr_ref[...].astype(BF), (((0,), (1,)), ((), ())),
                            preferred_element_type=F32) + gbc_ref[...]
    dlast = jnp.exp(jnp.sum(_log_sigmoid(pre_t) / GLA_GATE_NORMALIZER, axis=1, keepdims=True))
    lane = lax.broadcasted_iota(jnp.int32, (ts, LANES), 1)
    for h in range(GLA_HEADS):
        pair = slice((h // 2) * LANES, (h // 2 + 1) * LANES)
        own = _shr(lane, GLA_K_DIM) == (h % 2)
        vh = v[:, h * GLA_V_DIM:(h + 1) * GLA_V_DIM].astype(BF)
        s_pair = s0_ref[pair, :]
        qm = jnp.where(own, qd[:, pair], jnp.zeros((ts, LANES), BF))
        o_heads[h] = o_heads[h] + jnp.dot(qm, s_pair.astype(BF), preferred_element_type=F32)
        upd = lax.dot_general(kd[:, pair], vh, (((0,), (0,)), ((), ())),
                              preferred_element_type=F32)
        hr = slice(h * GLA_K_DIM, (h + 1) * GLA_K_DIM)
        lo = (h % 2) * GLA_K_DIM
        s_out_ref[hr, :] = dlast[hr] * s0_ref[hr, :] + upd[lo:lo + GLA_K_DIM]
    _gla_finish(o_heads, g_ref, gn_ref, o_ref)


def _gla_sample(z, w2, gate_b, gla_norm, s0, dst, *, nb, ts, row0):
    rb = row0 // ts
    s0r = s0.reshape(nb, GLA_HEADS * GLA_K_DIM, GLA_V_DIM)
    o, s_new = pl.pallas_call(
        functools.partial(_gla_sample_kernel, ts=ts),
        grid=(nb,),
        in_specs=[
            pl.BlockSpec((ts, GLA_K_WIDTH), lambda b: (rb + b, 4)),
            pl.BlockSpec((ts, GLA_K_WIDTH), lambda b: (rb + b, 5)),
            pl.BlockSpec((ts, GLA_V_WIDTH), lambda b: (rb + b, 3)),
            pl.BlockSpec((ts, GLA_V_WIDTH), lambda b: (rb + b, 4)),
            pl.BlockSpec((ts, GLA_RANK_PAD), lambda b: (rb + b, 20)),
            pl.BlockSpec((GLA_RANK_PAD, GLA_K_WIDTH), lambda b: (0, 0)),
            pl.BlockSpec((1, GLA_K_WIDTH), lambda b: (0, 0)),
            pl.BlockSpec((GLA_K_WIDTH, 1), lambda b: (0, 0)),
            pl.BlockSpec((1, GLA_V_DIM), lambda b: (0, 0)),
            pl.BlockSpec((None, GLA_HEADS * GLA_K_DIM, GLA_V_DIM), lambda b: (b, 0, 0)),
            ANY_SPEC,
        ],
        out_specs=[
            pl.BlockSpec((ts, GLA_V_WIDTH), lambda b: (rb + b, 0)),
            pl.BlockSpec((None, GLA_HEADS * GLA_K_DIM, GLA_V_DIM), lambda b: (b, 0, 0)),
        ],
        out_shape=[jax.ShapeDtypeStruct(dst.shape, dst.dtype),
                   jax.ShapeDtypeStruct((nb, GLA_HEADS * GLA_K_DIM, GLA_V_DIM), F32)],
        input_output_aliases={10: 0},
        compiler_params=_cparams("parallel"),
        name="gla_sample",
    )(z, z, z, z, z, w2, gate_b.reshape(1, GLA_K_WIDTH), gate_b.reshape(GLA_K_WIDTH, 1),
      gla_norm.reshape(1, GLA_V_DIM), s0r, dst)
    return o, s_new.reshape(nb, GLA_HEADS, GLA_K_DIM, GLA_V_DIM)


def kernel(x_prompt, x_sample, cache_k, cache_v, page_table, state_pool, state_gla, ffn1_norm, ffn1_w_gate, ffn1_w_up, ffn1_w_down, mix_norm, ffn2_norm, ffn2_w_gate, ffn2_w_up, ffn2_w_down, even_w_in, even_w_out, pool_w, pool_scale, diff_lambda, diff_subln, odd_w_in, odd_w_out, sgu_norm, sgu_w, sgu_b, gla_gate_w2, gla_gate_b, gla_norm, final_norm):
    nb, t, d = x_prompt.shape
    nbs, ts, _ = x_sample.shape
    depth = ffn1_norm.shape[0]
    n_p, n_s = nb * t, nbs * ts
    past_len = page_table.shape[1] * PAGE_SIZE
    assert ts < min(SGU_CHUNK, GLA_CHUNK) and ts & (ts - 1) == 0 and n_p % n_s == 0
    assert t % SGU_CHUNK == 0 and t >= POOL_PAD

    n = n_p + n_s
    x = jnp.concatenate([x_prompt.reshape(n_p, d), x_sample.reshape(n_s, d)], axis=0)
    tk = _pick_tile(t, 512, LANES)
    tq = _pick_tile(t, 2 * tk, tk)
    cos_p, sin_p = _rope_tables(jnp.arange(t))
    cos_s, sin_s = _rope_tables(past_len + jnp.arange(ts))
    cos_s, sin_s = jnp.tile(cos_s, (nbs, 1)), jnp.tile(sin_s, (nbs, 1))

    n_even = (depth + 1) // 2
    cache_kt = jnp.transpose(cache_k, (0, 1, 3, 4, 5, 2)).reshape(
        cache_k.shape[0], cache_k.shape[1], DA_WIDTH, PAGE_SIZE)
    cache_vh = jnp.transpose(cache_v, (0, 1, 3, 2, 4))
    kt_all = jnp.zeros((nb, n_even, DA_HEADS, DA_PAIR, t), F32)
    vh_all = jnp.zeros((nb, n_even, DA_HEADS, t, DA_PAIR), F32)
    pool_p, sgu_p, gla_p = [], [], []
    k_s, v_s, pool_s, sgu_s, gla_s = [], [], [], [], []
    for l in range(depth):
        i = l // 2
        x = _ffn(x, ffn1_norm[l], ffn1_w_gate, ffn1_w_up, ffn1_w_down, layer=l)
        if l % 2 == 0:
            lam_init = 0.8 - 0.6 * math.exp(-0.3 * l)
            w_in = even_w_in[i]
            w_in = jnp.concatenate([w_in[:, POOL_WIDTH:], w_in[:, :POOL_WIDTH]], axis=1).astype(BF)
            z = _inproj(x, mix_norm[l], w_in)
            pcol = 3 * DA_WIDTH // POOL_WIDTH
            w_bd = _block_diag(pool_w[i]).astype(BF)
            a1 = jnp.zeros((n, POOL_WIDTH), BF)
            a1 = _pool(z, z, w_bd, pool_scale[i], a1, nb=nb, t=t, row0=0, col_blk=pcol,
                       prev_is_state=False, pos0=0)
            st_pad = jnp.pad(state_pool[i], ((0, 0), (POOL_PAD - POOL_HIST, 0), (0, 0)))
            a1 = _pool(z, st_pad.reshape(nbs * POOL_PAD, POOL_WIDTH), w_bd, pool_scale[i], a1,
                       nb=nbs, t=ts, row0=n_p, col_blk=pcol, prev_is_state=True, pos0=past_len)
            qt, kb, vt, kn2, kt_all, vh_all = _prep_prompt(z, cos_p, sin_p, kt_all, vh_all, layer=i,
                                                           nb=nb, t=t, tk=tk)
            a2 = jnp.zeros((n, DA_WIDTH), BF)
            a2 = _attn_prompt(qt, kb, vt, kn2, diff_lambda[i], diff_subln[i], a2, tq=tq,
                              lam_init=lam_init)
            q_s, k_new = _rope_rows(z, cos_s, sin_s, row0=n_p, nrows=n_s)
            a2 = _attn_sample(q_s, k_new, z, cache_kt, cache_vh, page_table, diff_lambda[i],
                              diff_subln[i], a2, ts=ts, layer=i, row0=n_p, lam_init=lam_init)
            w_out = even_w_out[i].astype(BF)
            mix = (a1, a2, w_out[:POOL_WIDTH], w_out[POOL_WIDTH:])
            zs = z[n_p:].reshape(nbs, ts, -1)
            pool_p.append(jnp.stack([z[(b + 1) * t - POOL_HIST:(b + 1) * t, 3 * DA_WIDTH:]
                                     for b in range(nb)], axis=0))
            k_s.append(k_new.reshape(nbs, ts, DA_WIDTH))
            v_s.append(zs[:, :, 2 * DA_WIDTH:3 * DA_WIDTH])
            pool_s.append(jnp.concatenate([state_pool[i], zs[:, :, 3 * DA_WIDTH:]], axis=1)[:, -POOL_HIST:])
        else:
            w_in = jnp.pad(odd_w_in[i], ((0, 0), (0, GLA_RANK_PAD - GLA_GATE_RANK))).astype(BF)
            z = _inproj(x, mix_norm[l], w_in)
            w2 = jnp.pad(gla_gate_w2[i], ((0, GLA_RANK_PAD - GLA_GATE_RANK), (0, 0))).astype(BF)
            a1 = jnp.zeros((n, SGU_WIDTH), BF)
            a1, vr_p = _sgu_prompt(z, sgu_norm[i], sgu_w[i], sgu_b[i], a1, nb=nb, t=t)
            a1, vr_s = _sgu_sample(z, sgu_norm[i], sgu_w[i], sgu_b[i], a1, nb=nbs, ts=ts, row0=n_p)
            a2 = jnp.zeros((n, GLA_V_WIDTH), BF)
            a2, s_p = _gla_prompt(z, w2, gla_gate_b[i], gla_norm[i], a2, nb=nb, t=t)
            a2, s_s = _gla_sample(z, w2, gla_gate_b[i], gla_norm[i], state_gla[i], a2, nb=nbs, ts=ts,
                                  row0=n_p)
            w_out = odd_w_out[i].astype(BF)
            mix = (a1, a2, w_out[:SGU_WIDTH], w_out[SGU_WIDTH:])
            sgu_p.append(vr_p)
            gla_p.append(s_p)
            sgu_s.append(vr_s.reshape(nbs, ts, SGU_WIDTH))
            gla_s.append(s_s)
        x = _ffn(x, ffn2_norm[l], ffn2_w_gate, ffn2_w_up, ffn2_w_down, layer=l, mix=mix)
    y_p = _final_norm(x, final_norm, row0=0, nrows=n_p)
    y_s = _final_norm(x, final_norm, row0=n_p, nrows=n_s)
    k_rows_p = jnp.transpose(kt_all.reshape(nb, n_even, DA_HEADS, 2, DA_HEAD_DIM, t), (0, 1, 5, 2, 3, 4))
    v_rows_p = jnp.transpose(vh_all, (0, 1, 3, 2, 4))

    return (y_p.reshape(nb, t, d), y_s.reshape(nbs, ts, d), k_rows_p, v_rows_p,
            jnp.stack(pool_p, axis=0), jnp.stack(sgu_p, axis=0), jnp.stack(gla_p, axis=0),
            jnp.stack(k_s, axis=1).reshape(nbs, -1, ts, DA_HEADS, 2, DA_HEAD_DIM),
            jnp.stack(v_s, axis=1).reshape(nbs, -1, ts, DA_HEADS, DA_PAIR),
            jnp.stack(pool_s, axis=0), jnp.stack(sgu_s, axis=0), jnp.stack(gla_s, axis=0))
```

```python
import functools
import math

import numpy as np
import jax
import jax.numpy as jnp
from jax import lax
from jax.experimental import pallas as pl
from jax.experimental.pallas import tpu as pltpu

F32 = jnp.float32
BF = jnp.bfloat16

RMS_EPS = 1e-6
ROPE_THETA = 10000.0
PAGE_SIZE = 128

POOL_WINDOWS = (2, 4, 8, 16)
POOL_GROUP_DIM = 64
POOL_WIDTH = 256
POOL_HIST = 15
POOL_PAD = 16

DA_HEADS = 6
DA_HEAD_DIM = 64
DA_PAIR = 2 * DA_HEAD_DIM
DA_WIDTH = DA_HEADS * DA_PAIR
DA_VT_ROWS = DA_PAIR + 16
LOG2E = 1.4426950408889634
DA_SAFE_LOG2 = 60.0

SGU_GROUPS = 4
SGU_GROUP_DIM = 128
SGU_WIDTH = 512
SGU_CHUNK = 128

GLA_HEADS = 4
GLA_K_DIM = 64
GLA_V_DIM = 128
GLA_K_WIDTH = 256
GLA_V_WIDTH = 512
GLA_GATE_RANK = 16
GLA_GATE_NORMALIZER = 16.0
GLA_CHUNK = 64
GLA_RANK_PAD = 128

LANES = 128
VMEM_LIMIT = 56 * 1024 * 1024


def _cparams(*sem):
    return pltpu.CompilerParams(dimension_semantics=sem, vmem_limit_bytes=VMEM_LIMIT)


def _pick_tile(n, cap, mult=8):
    best = None
    for t in range(mult, min(n, cap) + 1, mult):
        if n % t == 0:
            best = t
    assert best is not None, (n, cap, mult)
    return best


def _shr(x, pow2):
    assert pow2 & (pow2 - 1) == 0
    return x >> (pow2.bit_length() - 1)


def _block_diag(w):
    g, a, b = w.shape
    out = jnp.zeros((g * a, g * b), w.dtype)
    for i in range(g):
        out = out.at[i * a:(i + 1) * a, i * b:(i + 1) * b].set(w[i])
    return out


ANY_SPEC = pl.BlockSpec(memory_space=pl.ANY)


def _rms(x, axis):
    return x * lax.rsqrt(jnp.mean(x * x, axis=axis, keepdims=True) + RMS_EPS)


def _ffn_kernel(*refs, nf, mixed):
    if mixed:
        x_ref, a1_ref, a2_ref, w1_ref, w2_ref, g_ref, wg_ref, wu_ref, wd_ref, o_ref, hn_ref, acc_ref = refs
    else:
        x_ref, g_ref, wg_ref, wu_ref, wd_ref, o_ref, hn_ref, acc_ref = refs
    f = pl.program_id(1)

    @pl.when(f == 0)
    def _():
        x = x_ref[...]
        if mixed:
            x = x + jnp.dot(a1_ref[...], w1_ref[...], preferred_element_type=F32)
            x = x + jnp.dot(a2_ref[...], w2_ref[...], preferred_element_type=F32)
            o_ref[...] = x
        hn_ref[...] = (_rms(x, -1) * g_ref[...]).astype(BF)
        acc_ref[...] = jnp.zeros_like(acc_ref)

    hn = hn_ref[...]
    a = jnp.dot(hn, wg_ref[...].astype(BF), preferred_element_type=F32)
    u = jnp.dot(hn, wu_ref[...].astype(BF), preferred_element_type=F32)
    h = (a * jax.nn.sigmoid(a) * u).astype(BF)
    acc_ref[...] += jnp.dot(h, wd_ref[...].astype(BF), preferred_element_type=F32)

    @pl.when(f == nf - 1)
    def _():
        o_ref[...] = (o_ref[...] if mixed else x_ref[...]) + 0.5 * acc_ref[...]


def _ffn(x, g, wg, wu, wd, *, layer, mix=None):
    n, d = x.shape
    ff = wg.shape[2]
    tm = _pick_tile(n, 1280)
    tf = _pick_tile(ff, 256, LANES)
    nf = ff // tf
    mix_specs, mix_args = [], []
    if mix is not None:
        a1, a2, w1, w2 = mix
        mix_specs = [
            pl.BlockSpec((tm, a1.shape[1]), lambda m, f: (m, 0)),
            pl.BlockSpec((tm, a2.shape[1]), lambda m, f: (m, 0)),
            pl.BlockSpec(w1.shape, lambda m, f: (0, 0)),
            pl.BlockSpec(w2.shape, lambda m, f: (0, 0)),
        ]
        mix_args = [a1, a2, w1, w2]
    return pl.pallas_call(
        functools.partial(_ffn_kernel, nf=nf, mixed=mix is not None),
        grid=(n // tm, nf),
        in_specs=[pl.BlockSpec((tm, d), lambda m, f: (m, 0))] + mix_specs + [
            pl.BlockSpec((1, d), lambda m, f: (0, 0)),
            pl.BlockSpec((None, d, tf), lambda m, f: (layer, 0, f)),
            pl.BlockSpec((None, d, tf), lambda m, f: (layer, 0, f)),
            pl.BlockSpec((None, tf, d), lambda m, f: (layer, f, 0)),
        ],
        out_specs=pl.BlockSpec((tm, d), lambda m, f: (m, 0)),
        out_shape=jax.ShapeDtypeStruct((n, d), F32),
        scratch_shapes=[pltpu.VMEM((tm, d), BF), pltpu.VMEM((tm, d), F32)],
        compiler_params=_cparams("parallel", "arbitrary"),
        name="ffn",
    )(x, *mix_args, g.reshape(1, d), wg, wu, wd)


def _inproj_kernel(x_ref, g_ref, w_ref, o_ref):
    hn = (_rms(x_ref[...], -1) * g_ref[...]).astype(BF)
    o_ref[...] = jnp.dot(hn, w_ref[...], preferred_element_type=F32)


def _inproj(x, g, w, *, row0=0, nrows=None):
    d = x.shape[1]
    n = x.shape[0] if nrows is None else nrows
    nout = w.shape[1]
    tm = _pick_tile(math.gcd(n, row0) if row0 else n, 640)
    rb = row0 // tm
    return pl.pallas_call(
        _inproj_kernel,
        grid=(n // tm,),
        in_specs=[
            pl.BlockSpec((tm, d), lambda m: (rb + m, 0)),
            pl.BlockSpec((1, d), lambda m: (0, 0)),
            pl.BlockSpec((d, nout), lambda m: (0, 0)),
        ],
        out_specs=pl.BlockSpec((tm, nout), lambda m: (m, 0)),
        out_shape=jax.ShapeDtypeStruct((n, nout), F32),
        compiler_params=_cparams("parallel"),
        name="inproj",
    )(x, g.reshape(1, d), w)


def _final_norm_kernel(x_ref, g_ref, o_ref):
    o_ref[...] = _rms(x_ref[...], -1) * g_ref[...]


def _final_norm(x, g, *, row0, nrows):
    d = x.shape[1]
    tm = _pick_tile(math.gcd(nrows, row0) if row0 else nrows, 1024)
    rb = row0 // tm
    return pl.pallas_call(
        _final_norm_kernel,
        grid=(nrows // tm,),
        in_specs=[pl.BlockSpec((tm, d), lambda m: (rb + m, 0)), pl.BlockSpec((1, d), lambda m: (0, 0))],
        out_specs=pl.BlockSpec((tm, d), lambda m: (m, 0)),
        out_shape=jax.ShapeDtypeStruct((nrows, d), F32),
        compiler_params=_cparams("parallel"),
        name="final_norm",
    )(x, g.reshape(1, d))


def _pool_kernel(pc_ref, pp_ref, w_ref, sc_ref, dst_ref, o_ref, *, tm, pos0, prev_at_first):
    del dst_ref
    i = pl.program_id(1)
    p = pc_ref[...]
    prev = pp_ref[...]
    if not prev_at_first:
        prev = jnp.where(i > 0, prev, 0.0)
    ext = jnp.concatenate([prev, p], axis=0)
    s2 = ext + pltpu.roll(ext, 1, 0)
    s4 = s2 + pltpu.roll(s2, 2, 0)
    s8 = s4 + pltpu.roll(s4, 4, 0)
    s16 = s8 + pltpu.roll(s8, 8, 0)
    grp = _shr(lax.broadcasted_iota(jnp.int32, (tm, POOL_WIDTH), 1), POOL_GROUP_DIM)
    pos = pos0 + i * tm + lax.broadcasted_iota(jnp.int32, (tm, POOL_WIDTH), 0)
    sums = (s2, s4, s8, s16)
    s = sums[3][POOL_PAD:]
    win = jnp.full((tm, POOL_WIDTH), POOL_WINDOWS[3], jnp.int32)
    for gi in (2, 1, 0):
        s = jnp.where(grp == gi, sums[gi][POOL_PAD:], s)
        win = jnp.where(grp == gi, POOL_WINDOWS[gi], win)
    cnt = jnp.minimum(pos + 1, win).astype(F32)
    diff = s / cnt - p
    y = jnp.dot(diff.astype(BF), w_ref[...], preferred_element_type=F32) * sc_ref[...]
    o_ref[...] = y.astype(BF)


def _pool(z, prev_src, w_bd, scale, dst, *, nb, t, zrow0, row0, col_blk, prev_is_state, pos0):
    tm = _pick_tile(t, 512)
    nt = t // tm
    rb0, zb0 = row0 // tm, zrow0 // tm
    if prev_is_state:
        assert nt == 1
        prev_spec = pl.BlockSpec((POOL_PAD, POOL_WIDTH), lambda b, i: (b, 0))
    else:
        r16 = tm // POOL_PAD
        base16 = zrow0 // POOL_PAD
        prev_spec = pl.BlockSpec(
            (POOL_PAD, POOL_WIDTH),
            lambda b, i: (jnp.maximum(base16 + (b * nt + i) * r16 - 1, 0), col_blk))
    return pl.pallas_call(
        functools.partial(_pool_kernel, tm=tm, pos0=pos0, prev_at_first=prev_is_state),
        grid=(nb, nt),
        in_specs=[
            pl.BlockSpec((tm, POOL_WIDTH), lambda b, i: (zb0 + b * nt + i, col_blk)),
            prev_spec,
            pl.BlockSpec((POOL_WIDTH, POOL_WIDTH), lambda b, i: (0, 0)),
            pl.BlockSpec((1, POOL_WIDTH), lambda b, i: (0, 0)),
            ANY_SPEC,
        ],
        out_specs=pl.BlockSpec((tm, POOL_WIDTH), lambda b, i: (rb0 + b * nt + i, 0)),
        out_shape=jax.ShapeDtypeStruct(dst.shape, dst.dtype),
        input_output_aliases={4: 0},
        compiler_params=_cparams("parallel", "arbitrary"),
        name="pool",
    )(z, prev_src, w_bd, scale.reshape(1, POOL_WIDTH), dst)


def _rope_tables(pos):
    half = DA_HEAD_DIM // 2
    inv = ROPE_THETA ** (-jnp.arange(half, dtype=F32) / half)
    ang = pos.astype(F32)[:, None] * inv[None, :]
    cos, sin = jnp.cos(ang), jnp.sin(ang)
    cos128 = jnp.concatenate([cos, cos, cos, cos], axis=1)
    sin128 = jnp.concatenate([-sin, sin, -sin, sin], axis=1)
    return cos128, sin128


def _rope128(x, cos, sin):
    lane = lax.broadcasted_iota(jnp.int32, x.shape, 1)
    first = (lane & (DA_HEAD_DIM - 1)) < (DA_HEAD_DIM // 2)
    partner = jnp.where(first, pltpu.roll(x, LANES - 32, 1), pltpu.roll(x, 32, 1))
    return x * cos + partner * sin


def _inproj_prep_kernel(x_ref, g_ref, w_ref, cos_ref, sin_ref, kt_dst_ref, vh_dst_ref,
                        qt_ref, kb_ref, vt_ref, kn_ref, zp_ref, kt_ref, vh_ref):
    del kt_dst_ref, vh_dst_ref
    hn = (_rms(x_ref[...], -1) * g_ref[...]).astype(BF)
    z = jnp.dot(hn, w_ref[...], preferred_element_type=F32)
    cos, sin = cos_ref[...], sin_ref[...]
    scale = DA_HEAD_DIM ** -0.5 * LOG2E
    ones = jnp.ones((DA_VT_ROWS - DA_PAIR, z.shape[0]), BF)
    for h in range(DA_HEADS):
        c0 = h * DA_PAIR
        q = _rope128(z[:, c0:c0 + DA_PAIR], cos, sin) * scale
        qt_ref[h] = q.T.astype(BF)
        k = _rope128(z[:, DA_WIDTH + c0:DA_WIDTH + c0 + DA_PAIR], cos, sin)
        kt_ref[h] = k.T
        kb = k.astype(BF)
        kb_ref[h] = kb
        kf = kb.astype(F32)
        kn2 = jnp.max(jnp.sum(kf * kf, axis=1, keepdims=True), axis=0, keepdims=True)
        kn_ref[h] = jnp.broadcast_to(kn2, kn_ref.shape[1:])
        v = z[:, 2 * DA_WIDTH + c0:2 * DA_WIDTH + c0 + DA_PAIR]
        vh_ref[h] = v
        vt_ref[h, :DA_PAIR] = v.T.astype(BF)
        vt_ref[h, DA_PAIR:] = ones
    zp_ref[...] = z[:, 3 * DA_WIDTH:]


def _inproj_prep(x, g, w, cos, sin, kt_all, vh_all, *, layer, nb, t, tk):
    d = x.shape[1]
    nt = t // tk
    return pl.pallas_call(
        _inproj_prep_kernel,
        grid=(nb, nt),
        in_specs=[
            pl.BlockSpec((tk, d), lambda b, i: (b * nt + i, 0)),
            pl.BlockSpec((1, d), lambda b, i: (0, 0)),
            pl.BlockSpec(w.shape, lambda b, i: (0, 0)),
            pl.BlockSpec((tk, LANES), lambda b, i: (i, 0)),
            pl.BlockSpec((tk, LANES), lambda b, i: (i, 0)),
            ANY_SPEC,
            ANY_SPEC,
        ],
        out_specs=[
            pl.BlockSpec((None, DA_HEADS, DA_PAIR, tk), lambda b, i: (b, 0, 0, i)),
            pl.BlockSpec((None, DA_HEADS, tk, DA_PAIR), lambda b, i: (b, 0, i, 0)),
            pl.BlockSpec((None, DA_HEADS, None, DA_VT_ROWS, tk), lambda b, i: (b, 0, i, 0, 0)),
            pl.BlockSpec((None, DA_HEADS, None, 8, LANES), lambda b, i: (b, 0, i, 0, 0)),
            pl.BlockSpec((tk, POOL_WIDTH), lambda b, i: (b * nt + i, 0)),
            pl.BlockSpec((None, None, DA_HEADS, DA_PAIR, tk), lambda b, i: (b, layer, 0, 0, i)),
            pl.BlockSpec((None, None, DA_HEADS, tk, DA_PAIR), lambda b, i: (b, layer, 0, i, 0)),
        ],
        out_shape=[
            jax.ShapeDtypeStruct((nb, DA_HEADS, DA_PAIR, t), BF),
            jax.ShapeDtypeStruct((nb, DA_HEADS, t, DA_PAIR), BF),
            jax.ShapeDtypeStruct((nb, DA_HEADS, nt, DA_VT_ROWS, tk), BF),
            jax.ShapeDtypeStruct((nb, DA_HEADS, nt, 8, LANES), F32),
            jax.ShapeDtypeStruct((nb * t, POOL_WIDTH), F32),
            jax.ShapeDtypeStruct(kt_all.shape, kt_all.dtype),
            jax.ShapeDtypeStruct(vh_all.shape, vh_all.dtype),
        ],
        input_output_aliases={5: 5, 6: 6},
        compiler_params=_cparams("parallel", "parallel"),
        name="inproj_prep",
    )(x, g.reshape(1, d), w, cos, sin, kt_all, vh_all)


def _lambda_value(lamv_ref, lam_init):
    lv = lamv_ref[...]
    s01 = jnp.sum(lv[0:1] * lv[1:2], axis=1, keepdims=True)
    s23 = jnp.sum(lv[2:3] * lv[3:4], axis=1, keepdims=True)
    return jnp.exp(s01) - jnp.exp(s23) + lam_init


def _attn_prompt_kernel(qt_ref, k_ref, vt_ref, kn_ref, lamv_ref, g_ref, dst_ref, o_ref,
                        qx_ref, sa_ref, sb_ref, m_ref, acc_ref, *, tq, tk, lam_init):
    del dst_ref
    ndiag = tq // tk
    qi = pl.program_id(2)
    nfull = qi * ndiag
    qt = qt_ref[...]
    row = lax.broadcasted_iota(jnp.int32, qt.shape, 0)
    zero = jnp.zeros_like(qt)
    qx_ref[:, :tq] = jnp.where(row < DA_HEAD_DIM, qt, zero)
    qx_ref[:, tq:] = jnp.where(row >= DA_HEAD_DIM, qt, zero)
    qf = qt.astype(F32)
    qn2 = jnp.max(jnp.sum(qf * qf, axis=0, keepdims=True))
    bounded = qn2 * jnp.max(kn_ref[...]) < DA_SAFE_LOG2 ** 2

    def keys(j):
        return k_ref[pl.ds(pl.multiple_of(j * tk, tk), tk), :]

    def diagonal_mask(shape, c, col0):
        kpos = c * tk + lax.broadcasted_iota(jnp.int32, shape, 0)
        qpos = (col0 + lax.broadcasted_iota(jnp.int32, shape, 1)) & (tq - 1)
        return kpos <= qpos

    def plain_chunk(j):
        p = jnp.exp2(jnp.dot(keys(j), qx_ref[...], preferred_element_type=F32))
        acc_ref[...] += jnp.dot(vt_ref[j], p.astype(BF), preferred_element_type=F32)

    def plain_diagonal(c):
        kj, vj = keys(nfull + c), vt_ref[nfull + c]
        for half in range(2):
            cols = slice(half * tq + c * tk, (half + 1) * tq)
            p = jnp.exp2(jnp.dot(kj, qx_ref[:, cols], preferred_element_type=F32))
            p = jnp.where(diagonal_mask(p.shape, c, c * tk), p, 0.0)
            d = jnp.dot(vj, p.astype(BF), preferred_element_type=F32)
            acc_ref[:, cols] = d if c == 0 else acc_ref[:, cols] + d

    @pl.when(bounded)
    def _():
        for c in range(ndiag):
            plain_diagonal(c)

        @pl.when((nfull & 1) == 1)
        def _():
            plain_chunk(0)

        def pair(pi, carry):
            j = (nfull & 1) + 2 * pi
            plain_chunk(j)
            plain_chunk(j + 1)
            return carry

        lax.fori_loop(0, nfull >> 1, pair, 0)

    def scores(j):
        return jnp.dot(keys(j), qx_ref[...], preferred_element_type=F32)

    def update(s, j):
        m_prev = m_ref[...]
        m_new = jnp.maximum(m_prev, jnp.max(s, axis=0, keepdims=True))
        p = jnp.exp2(s - m_new).astype(BF)
        acc_ref[...] = jnp.exp2(m_prev - m_new) * acc_ref[...] + jnp.dot(
            vt_ref[j], p, preferred_element_type=F32)
        m_ref[...] = m_new

    @pl.when(jnp.logical_not(bounded))
    def _():
        s = scores(nfull)
        s = jnp.where(diagonal_mask(s.shape, 0, 0), s, -jnp.inf)
        m0 = jnp.max(s, axis=0, keepdims=True)
        m_ref[...] = m0
        acc_ref[...] = jnp.dot(vt_ref[nfull], jnp.exp2(s - m0).astype(BF),
                               preferred_element_type=F32)
        for c in range(1, ndiag):
            s = scores(nfull + c)
            update(jnp.where(diagonal_mask(s.shape, c, 0), s, -jnp.inf), nfull + c)
        first = nfull & 1

        @pl.when(first == 1)
        def _():
            update(scores(0), 0)

        @pl.when(nfull >= 2)
        def _():
            sa_ref[...] = scores(first)

        def pair(pi, carry):
            j = first + 2 * pi
            sb_ref[...] = scores(j + 1)
            update(sa_ref[...], j)
            sa_ref[...] = scores(jnp.minimum(j + 2, nfull - 1))
            update(sb_ref[...], j + 1)
            return carry

        lax.fori_loop(0, nfull >> 1, pair, 0)

    acc = acc_ref[...]
    on = acc[:DA_PAIR] / acc[DA_PAIR:DA_PAIR + 1]
    lam = _lambda_value(lamv_ref, lam_init)
    ot = on[:, :tq] - lam * on[:, tq:]
    ot = _rms(ot, 0) * g_ref[...] * (1.0 - lam_init)
    o_ref[...] = ot.T.astype(BF)


def _attn_prompt(qt, kb, vt, kn2, lamv, subln_g, dst, *, tq, lam_init):
    nb, nh, _, t = qt.shape
    nk, tk = vt.shape[2], vt.shape[4]
    nq = t // tq
    return pl.pallas_call(
        functools.partial(_attn_prompt_kernel, tq=tq, tk=tk, lam_init=lam_init),
        grid=(nb, nh, nq),
        in_specs=[
            pl.BlockSpec((None, None, DA_PAIR, tq), lambda b, h, i: (b, h, 0, i)),
            pl.BlockSpec((None, None, t, DA_PAIR), lambda b, h, i: (b, h, 0, 0)),
            pl.BlockSpec((None, None, nk, DA_VT_ROWS, tk), lambda b, h, i: (b, h, 0, 0, 0)),
            pl.BlockSpec((None, None, nk, 8, LANES), lambda b, h, i: (b, h, 0, 0, 0)),
            pl.BlockSpec((4, DA_HEAD_DIM), lambda b, h, i: (0, 0)),
            pl.BlockSpec((DA_PAIR, 1), lambda b, h, i: (0, 0)),
            ANY_SPEC,
        ],
        out_specs=pl.BlockSpec((tq, DA_PAIR), lambda b, h, i: (b * nq + i, h)),
        out_shape=jax.ShapeDtypeStruct(dst.shape, dst.dtype),
        scratch_shapes=[pltpu.VMEM((DA_PAIR, 2 * tq), BF),
                        pltpu.VMEM((tk, 2 * tq), F32), pltpu.VMEM((tk, 2 * tq), F32),
                        pltpu.VMEM((1, 2 * tq), F32), pltpu.VMEM((DA_VT_ROWS, 2 * tq), F32)],
        input_output_aliases={6: 0},
        compiler_params=_cparams("parallel", "parallel", "arbitrary"),
        name="attn_prompt",
    )(qt, kb, vt, kn2, lamv, subln_g.reshape(DA_PAIR, 1), dst)


def _rope_rows_kernel(zq_ref, zk_ref, cos_ref, sin_ref, q_ref, k_ref):
    cos, sin = cos_ref[...], sin_ref[...]
    scale = DA_HEAD_DIM ** -0.5
    for h in range(DA_HEADS):
        sl = slice(h * DA_PAIR, (h + 1) * DA_PAIR)
        q_ref[:, sl] = (_rope128(zq_ref[:, sl], cos, sin) * scale).astype(BF)
        k_ref[:, sl] = _rope128(zk_ref[:, sl], cos, sin)


def _rope_rows(z, cos, sin, *, row0, nrows):
    rb = row0 // nrows
    return pl.pallas_call(
        _rope_rows_kernel,
        grid=(1,),
        in_specs=[
            pl.BlockSpec((nrows, DA_WIDTH), lambda i: (rb, 0)),
            pl.BlockSpec((nrows, DA_WIDTH), lambda i: (rb, 1)),
            pl.BlockSpec((nrows, LANES), lambda i: (0, 0)),
            pl.BlockSpec((nrows, LANES), lambda i: (0, 0)),
        ],
        out_specs=[pl.BlockSpec((nrows, DA_WIDTH), lambda i: (0, 0)),
                   pl.BlockSpec((nrows, DA_WIDTH), lambda i: (0, 0))],
        out_shape=[jax.ShapeDtypeStruct((nrows, DA_WIDTH), BF),
                   jax.ShapeDtypeStruct((nrows, DA_WIDTH), F32)],
        compiler_params=_cparams("arbitrary"),
        name="rope_rows",
    )(z, z, cos, sin)


def _attn_sample_kernel(pt_ref, q_ref, kn_ref, vn_ref, sel_ref, lamv_ref, g_ref, *rest,
                        pp, nsteps, ts, lam_init):
    kpages, vpages = rest[:pp], rest[pp:2 * pp]
    _, o_ref, m_ref, l_ref, acc_ref, qbd_ref = rest[2 * pp:]
    s = pl.program_id(1)
    nrow = 2 * DA_HEADS * ts

    @pl.when(s == 0)
    def _():
        q = q_ref[...]
        qrep = jnp.concatenate([q] * (2 * DA_HEADS), axis=0)
        qbd_ref[...] = jnp.where(sel_ref[...] > 0, qrep, jnp.zeros_like(qrep))
        m_ref[...] = jnp.full(m_ref.shape, -jnp.inf, F32)
        l_ref[...] = jnp.zeros_like(l_ref)
        acc_ref[...] = jnp.zeros_like(acc_ref)

    def update(sc, vv, causal):
        if causal:
            tq = lax.broadcasted_iota(jnp.int32, sc.shape, 0) & (ts - 1)
            tk = lax.broadcasted_iota(jnp.int32, sc.shape, 1)
            sc = jnp.where(tk <= tq, sc, -jnp.inf)
        m_prev = m_ref[...]
        m_new = jnp.maximum(m_prev, jnp.max(sc, axis=1, keepdims=True))
        alpha = jnp.exp(m_prev - m_new)
        p = jnp.exp(sc - m_new)
        l_ref[...] = alpha * l_ref[...] + jnp.sum(p, axis=1, keepdims=True)
        acc_ref[...] = alpha * acc_ref[...] + jnp.dot(p.astype(BF), vv, preferred_element_type=F32)
        m_ref[...] = m_new

    if pp:
        kt = jnp.concatenate([r[...] for r in kpages], axis=1).astype(BF)
        vv = jnp.concatenate(
            [jnp.concatenate([r[h] for h in range(DA_HEADS)], axis=1) for r in vpages],
            axis=0).astype(BF)
        update(jnp.dot(qbd_ref[...], kt, preferred_element_type=F32), vv, False)

    @pl.when(s == nsteps - 1)
    def _():
        sc_new = lax.dot_general(qbd_ref[...], kn_ref[...].astype(BF), (((1,), (1,)), ((), ())),
                                 preferred_element_type=F32)
        update(sc_new, vn_ref[...].astype(BF), True)
        on = acc_ref[...] / l_ref[...]
        lam = _lambda_value(lamv_ref, lam_init)
        half = nrow // 2
        d = on[:half] - lam * on[half:]
        outs = []
        for h in range(DA_HEADS):
            blk = d[h * ts:(h + 1) * ts, h * DA_PAIR:(h + 1) * DA_PAIR]
            outs.append(_rms(blk, -1) * g_ref[...] * (1.0 - lam_init))
        o_ref[...] = jnp.concatenate(outs, axis=1).astype(BF)


def _attn_sample(q_s, k_new, z, cache_kt, cache_vh, page_table, lamv, subln_g, dst, *,
                 ts, layer, zrow0, row0, lam_init):
    nb = q_s.shape[0] // ts
    n_pages = page_table.shape[1]
    pp = 0
    for cand in (16, 8, 4, 2, 1):
        if n_pages and n_pages % cand == 0:
            pp = cand
            break
    nsteps = max(n_pages // pp, 1) if pp else 1
    nrow = 2 * DA_HEADS * ts
    r = np.arange(nrow)[:, None] // ts
    c = np.arange(DA_WIDTH)[None, :] // DA_HEAD_DIM
    sel = jnp.asarray(((r % DA_HEADS) * 2 + r // DA_HEADS == c).astype(np.float32))
    rb, zb = row0 // ts, zrow0 // ts

    def kpage_spec(i):
        return pl.BlockSpec((None, None, DA_WIDTH, PAGE_SIZE),
                            lambda b, s, pt: (pt[b * n_pages + s * pp + i], layer, 0, 0))

    def vpage_spec(i):
        return pl.BlockSpec((None, None, DA_HEADS, PAGE_SIZE, DA_PAIR),
                            lambda b, s, pt: (pt[b * n_pages + s * pp + i], layer, 0, 0, 0))

    in_specs = [
        pl.BlockSpec((ts, DA_WIDTH), lambda b, s, pt: (b, 0)),
        pl.BlockSpec((ts, DA_WIDTH), lambda b, s, pt: (b, 0)),
        pl.BlockSpec((ts, DA_WIDTH), lambda b, s, pt: (zb + b, 2)),
        pl.BlockSpec((nrow, DA_WIDTH), lambda b, s, pt: (0, 0)),
        pl.BlockSpec((4, DA_HEAD_DIM), lambda b, s, pt: (0, 0)),
        pl.BlockSpec((1, DA_PAIR), lambda b, s, pt: (0, 0)),
    ] + [kpage_spec(i) for i in range(pp)] + [vpage_spec(i) for i in range(pp)] + [ANY_SPEC]
    grid_spec = pltpu.PrefetchScalarGridSpec(
        num_scalar_prefetch=1,
        grid=(nb, nsteps),
        in_specs=in_specs,
        out_specs=pl.BlockSpec((ts, DA_WIDTH), lambda b, s, pt: (rb + b, 0)),
        scratch_shapes=[pltpu.VMEM((nrow, 1), F32), pltpu.VMEM((nrow, 1), F32),
                        pltpu.VMEM((nrow, DA_WIDTH), F32), pltpu.VMEM((nrow, DA_WIDTH), BF)],
    )
    return pl.pallas_call(
        functools.partial(_attn_sample_kernel, pp=pp, nsteps=nsteps, ts=ts, lam_init=lam_init),
        grid_spec=grid_spec,
        out_shape=jax.ShapeDtypeStruct(dst.shape, dst.dtype),
        input_output_aliases={7 + 2 * pp: 0},
        compiler_params=_cparams("parallel", "arbitrary"),
        name="attn_sample",
    )(page_table.reshape(-1), q_s, k_new, z, sel, lamv, subln_g.reshape(1, DA_PAIR),
      *([cache_kt] * pp), *([cache_vh] * pp), dst)


def _sgu_norm_v(sv_ref, ng_ref, g):
    x = jax.nn.gelu(sv_ref[:, g * SGU_GROUP_DIM:(g + 1) * SGU_GROUP_DIM])
    return _rms(x, -1) * ng_ref[g:g + 1, :]


def _sgu_prompt_kernel(u_ref, sv_ref, ng_ref, w_ref, bs_ref, dst_ref, y_ref, vr_ref, *, tm, c):
    del dst_ref
    last = pl.program_id(1) == pl.num_programs(1) - 1
    ri = lax.broadcasted_iota(jnp.int32, (c, c), 0)
    ci = lax.broadcasted_iota(jnp.int32, (c, c), 1)
    for g in range(SGU_GROUPS):
        sl = slice(g * SGU_GROUP_DIM, (g + 1) * SGU_GROUP_DIM)
        v = _sgu_norm_v(sv_ref, ng_ref, g)

        @pl.when(last)
        def _():
            vr_ref[:, sl] = v[tm - c:]

        vb = v.astype(BF)
        w = jnp.where(ri >= ci, w_ref[g], 0.0).astype(BF)
        for n in range(tm // c):
            rows = slice(n * c, (n + 1) * c)
            mixed = jnp.dot(w, vb[rows], preferred_element_type=F32) + bs_ref[g]
            y_ref[rows, sl] = (jax.nn.gelu(u_ref[rows, sl]) * mixed).astype(BF)


def _sgu_prompt(z, norm_g, w_s, b_s, dst, *, nb, t):
    c = SGU_CHUNK
    tm = _pick_tile(t, 512, c)
    nt = t // tm
    bs = jnp.broadcast_to(b_s[:, :c, None], (SGU_GROUPS, c, SGU_GROUP_DIM))
    return pl.pallas_call(
        functools.partial(_sgu_prompt_kernel, tm=tm, c=c),
        grid=(nb, nt),
        in_specs=[
            pl.BlockSpec((tm, SGU_WIDTH), lambda b, i: (b * nt + i, 0)),
            pl.BlockSpec((tm, SGU_WIDTH), lambda b, i: (b * nt + i, 1)),
            pl.BlockSpec((SGU_GROUPS, SGU_GROUP_DIM), lambda b, i: (0, 0)),
            pl.BlockSpec((SGU_GROUPS, c, c), lambda b, i: (0, 0, 0)),
            pl.BlockSpec((SGU_GROUPS, c, SGU_GROUP_DIM), lambda b, i: (0, 0, 0)),
            ANY_SPEC,
        ],
        out_specs=[
            pl.BlockSpec((tm, SGU_WIDTH), lambda b, i: (b * nt + i, 0)),
            pl.BlockSpec((None, c, SGU_WIDTH), lambda b, i: (b, 0, 0)),
        ],
        out_shape=[jax.ShapeDtypeStruct(dst.shape, dst.dtype),
                   jax.ShapeDtypeStruct((nb, c, SGU_WIDTH), F32)],
        input_output_aliases={5: 0},
        compiler_params=_cparams("parallel", "arbitrary"),
        name="sgu_prompt",
    )(z, z, norm_g, w_s[:, :c, :c], bs, dst)


def _sgu_sample_kernel(u_ref, sv_ref, ng_ref, w_ref, bs_ref, dst_ref, y_ref, vr_ref, *, nb, ts):
    del dst_ref
    v = jnp.concatenate([_sgu_norm_v(sv_ref, ng_ref, g) for g in range(SGU_GROUPS)], axis=1)
    vr_ref[...] = v
    v3 = v.reshape(nb, ts, SGU_WIDTH)
    ii = lax.broadcasted_iota(jnp.int32, (ts, SGU_WIDTH), 0)
    mixed = jnp.broadcast_to(bs_ref[...][None], (nb, ts, SGU_WIDTH))
    for j in range(ts):
        wj = jnp.where(ii >= j, w_ref[j], 0.0)
        mixed = mixed + wj[None] * v3[:, j:j + 1, :]
    y = jax.nn.gelu(u_ref[...]).reshape(nb, ts, SGU_WIDTH) * mixed
    y_ref[...] = y.reshape(nb * ts, SGU_WIDTH).astype(BF)


def _sgu_sample(z, norm_g, w_s, b_s, dst, *, nb, ts, row0):
    nrows = nb * ts
    rb = row0 // nrows
    w_exp = jnp.repeat(jnp.transpose(w_s[:, :ts, :ts], (2, 1, 0)), SGU_GROUP_DIM, axis=2)
    b_exp = jnp.repeat(jnp.transpose(b_s[:, :ts], (1, 0)), SGU_GROUP_DIM, axis=1)
    return pl.pallas_call(
        functools.partial(_sgu_sample_kernel, nb=nb, ts=ts),
        grid=(1,),
        in_specs=[
            pl.BlockSpec((nrows, SGU_WIDTH), lambda i: (rb, 0)),
            pl.BlockSpec((nrows, SGU_WIDTH), lambda i: (rb, 1)),
            pl.BlockSpec((SGU_GROUPS, SGU_GROUP_DIM), lambda i: (0, 0)),
            pl.BlockSpec((ts, ts, SGU_WIDTH), lambda i: (0, 0, 0)),
            pl.BlockSpec((ts, SGU_WIDTH), lambda i: (0, 0)),
            ANY_SPEC,
        ],
        out_specs=[pl.BlockSpec((nrows, SGU_WIDTH), lambda i: (rb, 0)),
                   pl.BlockSpec((nrows, SGU_WIDTH), lambda i: (0, 0))],
        out_shape=[jax.ShapeDtypeStruct(dst.shape, dst.dtype),
                   jax.ShapeDtypeStruct((nrows, SGU_WIDTH), F32)],
        input_output_aliases={5: 0},
        compiler_params=_cparams("arbitrary"),
        name="sgu_sample",
    )(z, z, norm_g, w_exp, b_exp, dst)


def _log_sigmoid(x):
    return jnp.minimum(x, 0.0) - jnp.log1p(jnp.exp(-jnp.abs(x)))


def _seg_scan(x, seg, pos, reverse):
    n = x.shape[0]
    r = pos & (seg - 1)
    t = 1
    while t < seg:
        if reverse:
            x = x + jnp.where(r < seg - t, pltpu.roll(x, n - t, 0), 0.0)
        else:
            x = x + jnp.where(r >= t, pltpu.roll(x, t, 0), 0.0)
        t *= 2
    return x


def _gla_decays(gk, c):
    n, w = gk.shape
    pos = lax.broadcasted_iota(jnp.int32, gk.shape, 0)
    out = {1: (jnp.exp(gk), None)}
    s = 2
    while s <= c:
        if s <= 8 or n % s:
            pre = _seg_scan(gk, s, pos, False)
            suf = _seg_scan(gk, s, pos, True) - gk
        else:
            h = s // 2
            p4 = pre.reshape(n // s, 2, h, w)
            s4 = suf.reshape(n // s, 2, h, w)
            tot = p4[:, :, h - 1:h, :]
            pre = jnp.concatenate([p4[:, 0:1], p4[:, 1:2] + tot[:, 0:1]], axis=1).reshape(n, w)
            suf = jnp.concatenate([s4[:, 0:1] + tot[:, 1:2], s4[:, 1:2]], axis=1).reshape(n, w)
        out[s] = (jnp.exp(pre), jnp.exp(suf))
        s *= 2
    return out


def _gla_intra(q, k, v, dec, c, sc):
    n = q.shape[0]
    ri = lax.broadcasted_iota(jnp.int32, (sc, sc), 0)
    ci = lax.broadcasted_iota(jnp.int32, (sc, sc), 1)
    same_chunk = _shr(ri, c) == _shr(ci, c)
    lane = lax.broadcasted_iota(jnp.int32, (sc, LANES), 1)
    levels = []
    levels.append((ri == ci, q.astype(BF), k.astype(BF)))
    s = 1
    while s < c:
        pre, _ = dec[s]
        suf = dec[s][1]
        ks = k if suf is None else k * suf
        mask = same_chunk & (_shr(ri, 2 * s) == _shr(ci, 2 * s)) & ((_shr(ri, s) & 1) == 1) \
            & ((_shr(ci, s) & 1) == 0)
        levels.append((mask, (q * pre).astype(BF), ks.astype(BF)))
        s *= 2
    outs = []
    for h in range(GLA_HEADS):
        pair = slice((h // 2) * LANES, (h // 2 + 1) * LANES)
        own = _shr(lane, GLA_K_DIM) == (h % 2)
        vh = v[:, h * GLA_V_DIM:(h + 1) * GLA_V_DIM].astype(BF)
        rows_out = []
        for b0 in range(0, n, sc):
            rows = slice(b0, b0 + sc)
            a = jnp.zeros((sc, sc), F32)
            for mask, qs, ks in levels:
                qm = jnp.where(own, qs[rows, pair], jnp.zeros((sc, LANES), BF))
                al = lax.dot_general(qm, ks[rows, pair], (((1,), (1,)), ((), ())),
                                     preferred_element_type=F32)
                a = a + jnp.where(mask, al, 0.0)
            rows_out.append(jnp.dot(a.astype(BF), vh[rows], preferred_element_type=F32))
        outs.append(jnp.concatenate(rows_out, axis=0) if len(rows_out) > 1 else rows_out[0])
    return outs


def _gla_gate(glr_ref, w2_ref, gb_ref):
    pre = jnp.dot(glr_ref[...].astype(BF), w2_ref[...], preferred_element_type=F32) + gb_ref[...]
    return _log_sigmoid(pre) / GLA_GATE_NORMALIZER


def _gla_finish(o_heads, g_ref, gn_ref, o_ref):
    for h in range(GLA_HEADS):
        sl = slice(h * GLA_V_DIM, (h + 1) * GLA_V_DIM)
        gate = g_ref[:, sl]
        o_ref[:, sl] = (_rms(o_heads[h], -1) * gn_ref[...] * (gate * jax.nn.sigmoid(gate))).astype(BF)


def _gla_prompt_kernel(q_ref, k_ref, v_ref, g_ref, glr_ref, w2_ref, gb_ref, gn_ref, dst_ref,
                       o_ref, s_out_ref, st_ref, *, tm, c):
    del dst_ref
    i = pl.program_id(1)

    @pl.when(i == 0)
    def _():
        st_ref[...] = jnp.zeros_like(st_ref)

    gk = _gla_gate(glr_ref, w2_ref, gb_ref)
    q = q_ref[...] * (GLA_K_DIM ** -0.5)
    k = k_ref[...]
    v = v_ref[...]
    dec = _gla_decays(gk, c)
    o_heads = _gla_intra(q, k, v, dec, c, min(tm, LANES))
    eb, esuf = dec[c]
    qd = (q * eb).astype(BF)
    kd = (k * esuf).astype(BF)
    lane = lax.broadcasted_iota(jnp.int32, (c, LANES), 1)
    for h in range(GLA_HEADS):
        pair = slice((h // 2) * LANES, (h // 2 + 1) * LANES)
        own = _shr(lane, GLA_K_DIM) == (h % 2)
        vh = v[:, h * GLA_V_DIM:(h + 1) * GLA_V_DIM].astype(BF)
        st = st_ref[h]
        inter = []
        for n in range(tm // c):
            rows = slice(n * c, (n + 1) * c)
            qm = jnp.where(own, qd[rows, pair], jnp.zeros((c, LANES), BF))
            inter.append(lax.dot_general(qm, st.astype(BF), (((1,), (1,)), ((), ())),
                                         preferred_element_type=F32))
            upd = lax.dot_general(vh[rows], kd[rows, pair], (((0,), (0,)), ((), ())),
                                  preferred_element_type=F32)
            st = eb[n * c + c - 1:n * c + c, pair] * st + upd
        st_ref[h] = st
        o_heads[h] = o_heads[h] + jnp.concatenate(inter, axis=0)
    _gla_finish(o_heads, g_ref, gn_ref, o_ref)

    @pl.when(i == pl.num_programs(1) - 1)
    def _():
        for h in range(GLA_HEADS):
            lo = (h % 2) * GLA_K_DIM
            s_out_ref[h] = st_ref[h].T[lo:lo + GLA_K_DIM, :]


def _gla_prompt(z, w2, gate_b, gla_norm, dst, *, nb, t):
    c = GLA_CHUNK
    tm = _pick_tile(t, 256, LANES)
    nt = t // tm
    return pl.pallas_call(
        functools.partial(_gla_prompt_kernel, tm=tm, c=c),
        grid=(nb, nt),
        in_specs=[
            pl.BlockSpec((tm, GLA_K_WIDTH), lambda b, i: (b * nt + i, 4)),
            pl.BlockSpec((tm, GLA_K_WIDTH), lambda b, i: (b * nt + i, 5)),
            pl.BlockSpec((tm, GLA_V_WIDTH), lambda b, i: (b * nt + i, 3)),
            pl.BlockSpec((tm, GLA_V_WIDTH), lambda b, i: (b * nt + i, 4)),
            pl.BlockSpec((tm, GLA_RANK_PAD), lambda b, i: (b * nt + i, 20)),
            pl.BlockSpec((GLA_RANK_PAD, GLA_K_WIDTH), lambda b, i: (0, 0)),
            pl.BlockSpec((1, GLA_K_WIDTH), lambda b, i: (0, 0)),
            pl.BlockSpec((1, GLA_V_DIM), lambda b, i: (0, 0)),
            ANY_SPEC,
        ],
        out_specs=[
            pl.BlockSpec((tm, GLA_V_WIDTH), lambda b, i: (b * nt + i, 0)),
            pl.BlockSpec((None, GLA_HEADS, GLA_K_DIM, GLA_V_DIM), lambda b, i: (b, 0, 0, 0)),
        ],
        out_shape=[jax.ShapeDtypeStruct(dst.shape, dst.dtype),
                   jax.ShapeDtypeStruct((nb, GLA_HEADS, GLA_K_DIM, GLA_V_DIM), F32)],
        scratch_shapes=[pltpu.VMEM((GLA_HEADS, GLA_V_DIM, LANES), F32)],
        input_output_aliases={8: 0},
        compiler_params=_cparams("parallel", "arbitrary"),
        name="gla_prompt",
    )(z, z, z, z, z, w2, gate_b.reshape(1, GLA_K_WIDTH), gla_norm.reshape(1, GLA_V_DIM), dst)


def _gla_sample_kernel(q_ref, k_ref, v_ref, g_ref, glr_ref, w2_ref, gb_ref, gbc_ref, gn_ref, s0_ref,
                       dst_ref, o_ref, s_out_ref, *, ts):
    del dst_ref
    gk = _gla_gate(glr_ref, w2_ref, gb_ref)
    q = q_ref[...] * (GLA_K_DIM ** -0.5)
    k = k_ref[...]
    v = v_ref[...]
    dec = _gla_decays(gk, ts)
    o_heads = _gla_intra(q, k, v, dec, ts, ts)
    eb, esuf = dec[ts]
    qd = (q * eb).astype(BF)
    kd = (k * esuf).astype(BF)
    pre_t = lax.dot_general(w2_ref[...], glr_ref[...].astype(BF), (((0,), (1,)), ((), ())),
                            preferred_element_type=F32) + gbc_ref[...]
    dlast = jnp.exp(jnp.sum(_log_sigmoid(pre_t) / GLA_GATE_NORMALIZER, axis=1, keepdims=True))
    lane = lax.broadcasted_iota(jnp.int32, (ts, LANES), 1)
    for h in range(GLA_HEADS):
        pair = slice((h // 2) * LANES, (h // 2 + 1) * LANES)
        own = _shr(lane, GLA_K_DIM) == (h % 2)
        vh = v[:, h * GLA_V_DIM:(h + 1) * GLA_V_DIM].astype(BF)
        s_pair = s0_ref[pair, :]
        qm = jnp.where(own, qd[:, pair], jnp.zeros((ts, LANES), BF))
        o_heads[h] = o_heads[h] + jnp.dot(qm, s_pair.astype(BF), preferred_element_type=F32)
        upd = lax.dot_general(kd[:, pair], vh, (((0,), (0,)), ((), ())),
                              preferred_element_type=F32)
        hr = slice(h * GLA_K_DIM, (h + 1) * GLA_K_DIM)
        lo = (h % 2) * GLA_K_DIM
        s_out_ref[hr, :] = dlast[hr] * s0_ref[hr, :] + upd[lo:lo + GLA_K_DIM]
    _gla_finish(o_heads, g_ref, gn_ref, o_ref)


def _gla_sample(z, w2, gate_b, gla_norm, s0, dst, *, nb, ts, row0):
    rb = row0 // ts
    s0r = s0.reshape(nb, GLA_HEADS * GLA_K_DIM, GLA_V_DIM)
    o, s_new = pl.pallas_call(
        functools.partial(_gla_sample_kernel, ts=ts),
        grid=(nb,),
        in_specs=[
            pl.BlockSpec((ts, GLA_K_WIDTH), lambda b: (rb + b, 4)),
            pl.BlockSpec((ts, GLA_K_WIDTH), lambda b: (rb + b, 5)),
            pl.BlockSpec((ts, GLA_V_WIDTH), lambda b: (rb + b, 3)),
            pl.BlockSpec((ts, GLA_V_WIDTH), lambda b: (rb + b, 4)),
            pl.BlockSpec((ts, GLA_RANK_PAD), lambda b: (rb + b, 20)),
            pl.BlockSpec((GLA_RANK_PAD, GLA_K_WIDTH), lambda b: (0, 0)),
            pl.BlockSpec((1, GLA_K_WIDTH), lambda b: (0, 0)),
            pl.BlockSpec((GLA_K_WIDTH, 1), lambda b: (0, 0)),
            pl.BlockSpec((1, GLA_V_DIM), lambda b: (0, 0)),
            pl.BlockSpec((None, GLA_HEADS * GLA_K_DIM, GLA_V_DIM), lambda b: (b, 0, 0)),
            ANY_SPEC,
        ],
        out_specs=[
            pl.BlockSpec((ts, GLA_V_WIDTH), lambda b: (rb + b, 0)),
            pl.BlockSpec((None, GLA_HEADS * GLA_K_DIM, GLA_V_DIM), lambda b: (b, 0, 0)),
        ],
        out_shape=[jax.ShapeDtypeStruct(dst.shape, dst.dtype),
                   jax.ShapeDtypeStruct((nb, GLA_HEADS * GLA_K_DIM, GLA_V_DIM), F32)],
        input_output_aliases={10: 0},
        compiler_params=_cparams("parallel"),
        name="gla_sample",
    )(z, z, z, z, z, w2, gate_b.reshape(1, GLA_K_WIDTH), gate_b.reshape(GLA_K_WIDTH, 1),
      gla_norm.reshape(1, GLA_V_DIM), s0r, dst)
    return o, s_new.reshape(nb, GLA_HEADS, GLA_K_DIM, GLA_V_DIM)


def kernel(x_prompt, x_sample, cache_k, cache_v, page_table, state_pool, state_gla, ffn1_norm, ffn1_w_gate, ffn1_w_up, ffn1_w_down, mix_norm, ffn2_norm, ffn2_w_gate, ffn2_w_up, ffn2_w_down, even_w_in, even_w_out, pool_w, pool_scale, diff_lambda, diff_subln, odd_w_in, odd_w_out, sgu_norm, sgu_w, sgu_b, gla_gate_w2, gla_gate_b, gla_norm, final_norm):
    nb, t, d = x_prompt.shape
    nbs, ts, _ = x_sample.shape
    depth = ffn1_norm.shape[0]
    n_p, n_s = nb * t, nbs * ts
    past_len = page_table.shape[1] * PAGE_SIZE
    assert ts < min(SGU_CHUNK, GLA_CHUNK) and ts & (ts - 1) == 0 and n_p % n_s == 0
    assert t % SGU_CHUNK == 0 and t >= POOL_PAD

    n = n_p + n_s
    x = jnp.concatenate([x_prompt.reshape(n_p, d), x_sample.reshape(n_s, d)], axis=0)
    tk = _pick_tile(t, 512, LANES)
    tq = _pick_tile(t, 2 * tk, tk)
    cos_p, sin_p = _rope_tables(jnp.arange(t))
    cos_s, sin_s = _rope_tables(past_len + jnp.arange(ts))
    cos_s, sin_s = jnp.tile(cos_s, (nbs, 1)), jnp.tile(sin_s, (nbs, 1))

    n_even = (depth + 1) // 2
    cache_kt = jnp.transpose(cache_k, (0, 1, 3, 4, 5, 2)).reshape(
        cache_k.shape[0], cache_k.shape[1], DA_WIDTH, PAGE_SIZE)
    cache_vh = jnp.transpose(cache_v, (0, 1, 3, 2, 4))
    kt_all = jnp.zeros((nb, n_even, DA_HEADS, DA_PAIR, t), F32)
    vh_all = jnp.zeros((nb, n_even, DA_HEADS, t, DA_PAIR), F32)
    pool_p, sgu_p, gla_p = [], [], []
    k_s, v_s, pool_s, sgu_s, gla_s = [], [], [], [], []
    for l in range(depth):
        i = l // 2
        x = _ffn(x, ffn1_norm[l], ffn1_w_gate, ffn1_w_up, ffn1_w_down, layer=l)
        if l % 2 == 0:
            lam_init = 0.8 - 0.6 * math.exp(-0.3 * l)
            w_in = even_w_in[i]
            w_in = jnp.concatenate([w_in[:, POOL_WIDTH:], w_in[:, :POOL_WIDTH]], axis=1).astype(BF)
            qt, kb, vt, kn2, zp, kt_all, vh_all = _inproj_prep(
                x, mix_norm[l], w_in, cos_p, sin_p, kt_all, vh_all, layer=i, nb=nb, t=t, tk=tk)
            z_s = _inproj(x, mix_norm[l], w_in, row0=n_p, nrows=n_s)
            pcol = 3 * DA_WIDTH // POOL_WIDTH
            w_bd = _block_diag(pool_w[i]).astype(BF)
            a1 = jnp.zeros((n, POOL_WIDTH), BF)
            a1 = _pool(zp, zp, w_bd, pool_scale[i], a1, nb=nb, t=t, zrow0=0, row0=0, col_blk=0,
                       prev_is_state=False, pos0=0)
            st_pad = jnp.pad(state_pool[i], ((0, 0), (POOL_PAD - POOL_HIST, 0), (0, 0)))
            a1 = _pool(z_s, st_pad.reshape(nbs * POOL_PAD, POOL_WIDTH), w_bd, pool_scale[i], a1,
                       nb=nbs, t=ts, zrow0=0, row0=n_p, col_blk=pcol, prev_is_state=True,
                       pos0=past_len)
            a2 = jnp.zeros((n, DA_WIDTH), BF)
            a2 = _attn_prompt(qt, kb, vt, kn2, diff_lambda[i], diff_subln[i], a2, tq=tq,
                              lam_init=lam_init)
            q_s, k_new = _rope_rows(z_s, cos_s, sin_s, row0=0, nrows=n_s)
            a2 = _attn_sample(q_s, k_new, z_s, cache_kt, cache_vh, page_table, diff_lambda[i],
                              diff_subln[i], a2, ts=ts, layer=i, zrow0=0, row0=n_p,
                              lam_init=lam_init)
            w_out = even_w_out[i].astype(BF)
            mix = (a1, a2, w_out[:POOL_WIDTH], w_out[POOL_WIDTH:])
            zs = z_s.reshape(nbs, ts, -1)
            pool_p.append(zp.reshape(nb, t, POOL_WIDTH)[:, t - POOL_HIST:])
            k_s.append(k_new.reshape(nbs, ts, DA_WIDTH))
            v_s.append(zs[:, :, 2 * DA_WIDTH:3 * DA_WIDTH])
            pool_s.append(jnp.concatenate([state_pool[i], zs[:, :, 3 * DA_WIDTH:]], axis=1)[:, -POOL_HIST:])
        else:
            w_in = jnp.pad(odd_w_in[i], ((0, 0), (0, GLA_RANK_PAD - GLA_GATE_RANK))).astype(BF)
            z = _inproj(x, mix_norm[l], w_in)
            w2 = jnp.pad(gla_gate_w2[i], ((0, GLA_RANK_PAD - GLA_GATE_RANK), (0, 0))).astype(BF)
            a1 = jnp.zeros((n, SGU_WIDTH), BF)
            a1, vr_p = _sgu_prompt(z, sgu_norm[i], sgu_w[i], sgu_b[i], a1, nb=nb, t=t)
            a1, vr_s = _sgu_sample(z, sgu_norm[i], sgu_w[i], sgu_b[i], a1, nb=nbs, ts=ts, row0=n_p)
            a2 = jnp.zeros((n, GLA_V_WIDTH), BF)
            a2, s_p = _gla_prompt(z, w2, gla_gate_b[i], gla_norm[i], a2, nb=nb, t=t)
            a2, s_s = _gla_sample(z, w2, gla_gate_b[i], gla_norm[i], state_gla[i], a2, nb=nbs, ts=ts,
                                  row0=n_p)
            w_out = odd_w_out[i].astype(BF)
            mix = (a1, a2, w_out[:SGU_WIDTH], w_out[SGU_WIDTH:])
            sgu_p.append(vr_p)
            gla_p.append(s_p)
            sgu_s.append(vr_s.reshape(nbs, ts, SGU_WIDTH))
            gla_s.append(s_s)
        x = _ffn(x, ffn2_norm[l], ffn2_w_gate, ffn2_w_up, ffn2_w_down, layer=l, mix=mix)
    y_p = _final_norm(x, final_norm, row0=0, nrows=n_p)
    y_s = _final_norm(x, final_norm, row0=n_p, nrows=n_s)
    k_rows_p = jnp.transpose(kt_all.reshape(nb, n_even, DA_HEADS, 2, DA_HEAD_DIM, t), (0, 1, 5, 2, 3, 4))
    v_rows_p = jnp.transpose(vh_all, (0, 1, 3, 2, 4))

    return (y_p.reshape(nb, t, d), y_s.reshape(nbs, ts, d), k_rows_p, v_rows_p,
            jnp.stack(pool_p, axis=0), jnp.stack(sgu_p, axis=0), jnp.stack(gla_p, axis=0),
            jnp.stack(k_s, axis=1).reshape(nbs, -1, ts, DA_HEADS, 2, DA_HEAD_DIM),
            jnp.stack(v_s, axis=1).reshape(nbs, -1, ts, DA_HEADS, DA_PAIR),
            jnp.stack(pool_s, axis=0), jnp.stack(sgu_s, axis=0), jnp.stack(gla_s, axis=0))
```

```python
import functools
import math

import numpy as np
import jax
import jax.numpy as jnp
from jax import lax
from jax.experimental import pallas as pl
from jax.experimental.pallas import tpu as pltpu

F32 = jnp.float32
BF = jnp.bfloat16

RMS_EPS = 1e-6
ROPE_THETA = 10000.0
PAGE_SIZE = 128

POOL_WINDOWS = (2, 4, 8, 16)
POOL_GROUP_DIM = 64
POOL_WIDTH = 256
POOL_HIST = 15
POOL_PAD = 16

DA_HEADS = 6
DA_HEAD_DIM = 64
DA_PAIR = 2 * DA_HEAD_DIM
DA_WIDTH = DA_HEADS * DA_PAIR
DA_VT_ROWS = DA_PAIR + 16
LOG2E = 1.4426950408889634
DA_SAFE_LOG2 = 60.0

SGU_GROUPS = 4
SGU_GROUP_DIM = 128
SGU_WIDTH = 512
SGU_CHUNK = 128

GLA_HEADS = 4
GLA_K_DIM = 64
GLA_V_DIM = 128
GLA_K_WIDTH = 256
GLA_V_WIDTH = 512
GLA_GATE_RANK = 16
GLA_GATE_NORMALIZER = 16.0
GLA_CHUNK = 64
GLA_RANK_PAD = 128

LANES = 128
VMEM_LIMIT = 56 * 1024 * 1024


def _cparams(*sem):
    return pltpu.CompilerParams(dimension_semantics=sem, vmem_limit_bytes=VMEM_LIMIT)


def _pick_tile(n, cap, mult=8):
    best = None
    for t in range(mult, min(n, cap) + 1, mult):
        if n % t == 0:
            best = t
    assert best is not None, (n, cap, mult)
    return best


def _shr(x, pow2):
    assert pow2 & (pow2 - 1) == 0
    return x >> (pow2.bit_length() - 1)


def _block_diag(w):
    g, a, b = w.shape
    out = jnp.zeros((g * a, g * b), w.dtype)
    for i in range(g):
        out = out.at[i * a:(i + 1) * a, i * b:(i + 1) * b].set(w[i])
    return out


ANY_SPEC = pl.BlockSpec(memory_space=pl.ANY)


def _rms(x, axis):
    return x * lax.rsqrt(jnp.mean(x * x, axis=axis, keepdims=True) + RMS_EPS)


def _ffn_kernel(*refs, nf, mixed):
    if mixed:
        x_ref, a1_ref, a2_ref, w1_ref, w2_ref, g_ref, wg_ref, wu_ref, wd_ref, o_ref, hn_ref, acc_ref = refs
    else:
        x_ref, g_ref, wg_ref, wu_ref, wd_ref, o_ref, hn_ref, acc_ref = refs
    f = pl.program_id(1)

    @pl.when(f == 0)
    def _():
        x = x_ref[...]
        if mixed:
            x = x + jnp.dot(a1_ref[...], w1_ref[...], preferred_element_type=F32)
            x = x + jnp.dot(a2_ref[...], w2_ref[...], preferred_element_type=F32)
            o_ref[...] = x
        hn_ref[...] = (_rms(x, -1) * g_ref[...]).astype(BF)
        acc_ref[...] = jnp.zeros_like(acc_ref)

    hn = hn_ref[...]
    a = jnp.dot(hn, wg_ref[...].astype(BF), preferred_element_type=F32)
    u = jnp.dot(hn, wu_ref[...].astype(BF), preferred_element_type=F32)
    h = (a * jax.nn.sigmoid(a) * u).astype(BF)
    acc_ref[...] += jnp.dot(h, wd_ref[...].astype(BF), preferred_element_type=F32)

    @pl.when(f == nf - 1)
    def _():
        o_ref[...] = (o_ref[...] if mixed else x_ref[...]) + 0.5 * acc_ref[...]


def _ffn(x, g, wg, wu, wd, *, layer, mix=None):
    n, d = x.shape
    ff = wg.shape[2]
    tm = _pick_tile(n, 1280)
    tf = _pick_tile(ff, 256, LANES)
    nf = ff // tf
    mix_specs, mix_args = [], []
    if mix is not None:
        a1, a2, w1, w2 = mix
        mix_specs = [
            pl.BlockSpec((tm, a1.shape[1]), lambda m, f: (m, 0)),
            pl.BlockSpec((tm, a2.shape[1]), lambda m, f: (m, 0)),
            pl.BlockSpec(w1.shape, lambda m, f: (0, 0)),
            pl.BlockSpec(w2.shape, lambda m, f: (0, 0)),
        ]
        mix_args = [a1, a2, w1, w2]
    return pl.pallas_call(
        functools.partial(_ffn_kernel, nf=nf, mixed=mix is not None),
        grid=(n // tm, nf),
        in_specs=[pl.BlockSpec((tm, d), lambda m, f: (m, 0))] + mix_specs + [
            pl.BlockSpec((1, d), lambda m, f: (0, 0)),
            pl.BlockSpec((None, d, tf), lambda m, f: (layer, 0, f)),
            pl.BlockSpec((None, d, tf), lambda m, f: (layer, 0, f)),
            pl.BlockSpec((None, tf, d), lambda m, f: (layer, f, 0)),
        ],
        out_specs=pl.BlockSpec((tm, d), lambda m, f: (m, 0)),
        out_shape=jax.ShapeDtypeStruct((n, d), F32),
        scratch_shapes=[pltpu.VMEM((tm, d), BF), pltpu.VMEM((tm, d), F32)],
        compiler_params=_cparams("parallel", "arbitrary"),
        name="ffn",
    )(x, *mix_args, g.reshape(1, d), wg, wu, wd)


def _inproj_kernel(x_ref, g_ref, w_ref, o_ref):
    hn = (_rms(x_ref[...], -1) * g_ref[...]).astype(BF)
    o_ref[...] = jnp.dot(hn, w_ref[...], preferred_element_type=F32)


def _inproj(x, g, w, *, row0=0, nrows=None):
    d = x.shape[1]
    n = x.shape[0] if nrows is None else nrows
    nout = w.shape[1]
    tm = _pick_tile(math.gcd(n, row0) if row0 else n, 640)
    rb = row0 // tm
    return pl.pallas_call(
        _inproj_kernel,
        grid=(n // tm,),
        in_specs=[
            pl.BlockSpec((tm, d), lambda m: (rb + m, 0)),
            pl.BlockSpec((1, d), lambda m: (0, 0)),
            pl.BlockSpec((d, nout), lambda m: (0, 0)),
        ],
        out_specs=pl.BlockSpec((tm, nout), lambda m: (m, 0)),
        out_shape=jax.ShapeDtypeStruct((n, nout), F32),
        compiler_params=_cparams("parallel"),
        name="inproj",
    )(x, g.reshape(1, d), w)


def _final_norm_kernel(x_ref, g_ref, o_ref):
    o_ref[...] = _rms(x_ref[...], -1) * g_ref[...]


def _final_norm(x, g, *, row0, nrows):
    d = x.shape[1]
    tm = _pick_tile(math.gcd(nrows, row0) if row0 else nrows, 1024)
    rb = row0 // tm
    return pl.pallas_call(
        _final_norm_kernel,
        grid=(nrows // tm,),
        in_specs=[pl.BlockSpec((tm, d), lambda m: (rb + m, 0)), pl.BlockSpec((1, d), lambda m: (0, 0))],
        out_specs=pl.BlockSpec((tm, d), lambda m: (m, 0)),
        out_shape=jax.ShapeDtypeStruct((nrows, d), F32),
        compiler_params=_cparams("parallel"),
        name="final_norm",
    )(x, g.reshape(1, d))


def _pool_kernel(pc_ref, pp_ref, w_ref, sc_ref, dst_ref, o_ref, *, tm, pos0, prev_at_first):
    del dst_ref
    i = pl.program_id(1)
    p = pc_ref[...]
    prev = pp_ref[...]
    if not prev_at_first:
        prev = jnp.where(i > 0, prev, 0.0)
    ext = jnp.concatenate([prev, p], axis=0)
    s2 = ext + pltpu.roll(ext, 1, 0)
    s4 = s2 + pltpu.roll(s2, 2, 0)
    s8 = s4 + pltpu.roll(s4, 4, 0)
    s16 = s8 + pltpu.roll(s8, 8, 0)
    grp = _shr(lax.broadcasted_iota(jnp.int32, (tm, POOL_WIDTH), 1), POOL_GROUP_DIM)
    pos = pos0 + i * tm + lax.broadcasted_iota(jnp.int32, (tm, POOL_WIDTH), 0)
    sums = (s2, s4, s8, s16)
    s = sums[3][POOL_PAD:]
    win = jnp.full((tm, POOL_WIDTH), POOL_WINDOWS[3], jnp.int32)
    for gi in (2, 1, 0):
        s = jnp.where(grp == gi, sums[gi][POOL_PAD:], s)
        win = jnp.where(grp == gi, POOL_WINDOWS[gi], win)
    cnt = jnp.minimum(pos + 1, win).astype(F32)
    diff = s / cnt - p
    y = jnp.dot(diff.astype(BF), w_ref[...], preferred_element_type=F32) * sc_ref[...]
    o_ref[...] = y.astype(BF)


def _pool(z, prev_src, w_bd, scale, dst, *, nb, t, zrow0, row0, col_blk, prev_is_state, pos0):
    tm = _pick_tile(t, 512)
    nt = t // tm
    rb0, zb0 = row0 // tm, zrow0 // tm
    if prev_is_state:
        assert nt == 1
        prev_spec = pl.BlockSpec((POOL_PAD, POOL_WIDTH), lambda b, i: (b, 0))
    else:
        r16 = tm // POOL_PAD
        base16 = zrow0 // POOL_PAD
        prev_spec = pl.BlockSpec(
            (POOL_PAD, POOL_WIDTH),
            lambda b, i: (jnp.maximum(base16 + (b * nt + i) * r16 - 1, 0), col_blk))
    return pl.pallas_call(
        functools.partial(_pool_kernel, tm=tm, pos0=pos0, prev_at_first=prev_is_state),
        grid=(nb, nt),
        in_specs=[
            pl.BlockSpec((tm, POOL_WIDTH), lambda b, i: (zb0 + b * nt + i, col_blk)),
            prev_spec,
            pl.BlockSpec((POOL_WIDTH, POOL_WIDTH), lambda b, i: (0, 0)),
            pl.BlockSpec((1, POOL_WIDTH), lambda b, i: (0, 0)),
            ANY_SPEC,
        ],
        out_specs=pl.BlockSpec((tm, POOL_WIDTH), lambda b, i: (rb0 + b * nt + i, 0)),
        out_shape=jax.ShapeDtypeStruct(dst.shape, dst.dtype),
        input_output_aliases={4: 0},
        compiler_params=_cparams("parallel", "arbitrary"),
        name="pool",
    )(z, prev_src, w_bd, scale.reshape(1, POOL_WIDTH), dst)


def _rope_tables(pos):
    half = DA_HEAD_DIM // 2
    inv = ROPE_THETA ** (-jnp.arange(half, dtype=F32) / half)
    ang = pos.astype(F32)[:, None] * inv[None, :]
    cos, sin = jnp.cos(ang), jnp.sin(ang)
    cos128 = jnp.concatenate([cos, cos, cos, cos], axis=1)
    sin128 = jnp.concatenate([-sin, sin, -sin, sin], axis=1)
    return cos128, sin128


def _rope128(x, cos, sin):
    lane = lax.broadcasted_iota(jnp.int32, x.shape, 1)
    first = (lane & (DA_HEAD_DIM - 1)) < (DA_HEAD_DIM // 2)
    partner = jnp.where(first, pltpu.roll(x, LANES - 32, 1), pltpu.roll(x, 32, 1))
    return x * cos + partner * sin


def _inproj_prep_kernel(x_ref, g_ref, w_ref, cos_ref, sin_ref, kt_dst_ref, vh_dst_ref,
                        qt_ref, kb_ref, vt_ref, kn_ref, zp_ref, kt_ref, vh_ref):
    del kt_dst_ref, vh_dst_ref
    hn = (_rms(x_ref[...], -1) * g_ref[...]).astype(BF)
    z = jnp.dot(hn, w_ref[...], preferred_element_type=F32)
    cos, sin = cos_ref[...], sin_ref[...]
    scale = DA_HEAD_DIM ** -0.5 * LOG2E
    ones = jnp.ones((DA_VT_ROWS - DA_PAIR, z.shape[0]), BF)
    for h in range(DA_HEADS):
        c0 = h * DA_PAIR
        q = _rope128(z[:, c0:c0 + DA_PAIR], cos, sin) * scale
        qt_ref[h] = q.T.astype(BF)
        k = _rope128(z[:, DA_WIDTH + c0:DA_WIDTH + c0 + DA_PAIR], cos, sin)
        kt_ref[h] = k.T
        kb = k.astype(BF)
        kb_ref[h] = kb
        kf = kb.astype(F32)
        kn2 = jnp.max(jnp.sum(kf * kf, axis=1, keepdims=True), axis=0, keepdims=True)
        kn_ref[h] = jnp.broadcast_to(kn2, kn_ref.shape[1:])
        v = z[:, 2 * DA_WIDTH + c0:2 * DA_WIDTH + c0 + DA_PAIR]
        vh_ref[h] = v
        vt_ref[h, :DA_PAIR] = v.T.astype(BF)
        vt_ref[h, DA_PAIR:] = ones
    zp_ref[...] = z[:, 3 * DA_WIDTH:]


def _inproj_prep(x, g, w, cos, sin, kt_all, vh_all, *, layer, nb, t, tk):
    d = x.shape[1]
    nt = t // tk
    return pl.pallas_call(
        _inproj_prep_kernel,
        grid=(nb, nt),
        in_specs=[
            pl.BlockSpec((tk, d), lambda b, i: (b * nt + i, 0)),
            pl.BlockSpec((1, d), lambda b, i: (0, 0)),
            pl.BlockSpec(w.shape, lambda b, i: (0, 0)),
            pl.BlockSpec((tk, LANES), lambda b, i: (i, 0)),
            pl.BlockSpec((tk, LANES), lambda b, i: (i, 0)),
            ANY_SPEC,
            ANY_SPEC,
        ],
        out_specs=[
            pl.BlockSpec((None, DA_HEADS, DA_PAIR, tk), lambda b, i: (b, 0, 0, i)),
            pl.BlockSpec((None, DA_HEADS, tk, DA_PAIR), lambda b, i: (b, 0, i, 0)),
            pl.BlockSpec((None, DA_HEADS, None, DA_VT_ROWS, tk), lambda b, i: (b, 0, i, 0, 0)),
            pl.BlockSpec((None, DA_HEADS, None, 8, LANES), lambda b, i: (b, 0, i, 0, 0)),
            pl.BlockSpec((tk, POOL_WIDTH), lambda b, i: (b * nt + i, 0)),
            pl.BlockSpec((None, None, DA_HEADS, DA_PAIR, tk), lambda b, i: (b, layer, 0, 0, i)),
            pl.BlockSpec((None, None, DA_HEADS, tk, DA_PAIR), lambda b, i: (b, layer, 0, i, 0)),
        ],
        out_shape=[
            jax.ShapeDtypeStruct((nb, DA_HEADS, DA_PAIR, t), BF),
            jax.ShapeDtypeStruct((nb, DA_HEADS, t, DA_PAIR), BF),
            jax.ShapeDtypeStruct((nb, DA_HEADS, nt, DA_VT_ROWS, tk), BF),
            jax.ShapeDtypeStruct((nb, DA_HEADS, nt, 8, LANES), F32),
            jax.ShapeDtypeStruct((nb * t, POOL_WIDTH), F32),
            jax.ShapeDtypeStruct(kt_all.shape, kt_all.dtype),
            jax.ShapeDtypeStruct(vh_all.shape, vh_all.dtype),
        ],
        input_output_aliases={5: 5, 6: 6},
        compiler_params=_cparams("parallel", "parallel"),
        name="inproj_prep",
    )(x, g.reshape(1, d), w, cos, sin, kt_all, vh_all)


def _lambda_value(lamv_ref, lam_init):
    lv = lamv_ref[...]
    s01 = jnp.sum(lv[0:1] * lv[1:2], axis=1, keepdims=True)
    s23 = jnp.sum(lv[2:3] * lv[3:4], axis=1, keepdims=True)
    return jnp.exp(s01) - jnp.exp(s23) + lam_init


def _attn_prompt_kernel(qt_ref, k_ref, vt_ref, kn_ref, lamv_ref, g_ref, dst_ref, o_ref,
                        qx_ref, sa_ref, sb_ref, m_ref, acc_ref, *, tq, tk, lam_init):
    del dst_ref
    ndiag = tq // tk
    qi = pl.program_id(2)
    nfull = qi * ndiag
    qt = qt_ref[...]
    row = lax.broadcasted_iota(jnp.int32, qt.shape, 0)
    zero = jnp.zeros_like(qt)
    qx_ref[:, :tq] = jnp.where(row < DA_HEAD_DIM, qt, zero)
    qx_ref[:, tq:] = jnp.where(row >= DA_HEAD_DIM, qt, zero)
    qf = qt.astype(F32)
    qn2 = jnp.max(jnp.sum(qf * qf, axis=0, keepdims=True))
    bounded = qn2 * jnp.max(kn_ref[...]) < DA_SAFE_LOG2 ** 2

    def keys(j):
        return k_ref[pl.ds(pl.multiple_of(j * tk, tk), tk), :]

    def diagonal_mask(shape, c, col0):
        kpos = c * tk + lax.broadcasted_iota(jnp.int32, shape, 0)
        qpos = (col0 + lax.broadcasted_iota(jnp.int32, shape, 1)) & (tq - 1)
        return kpos <= qpos

    psum = slice(DA_PAIR, DA_PAIR + 8)

    def plain_update(j, p, cols, first):
        d = jnp.dot(vt_ref[j, :DA_PAIR], p, preferred_element_type=F32)
        part = jnp.sum(p.astype(F32).reshape(tk // 8, 8, p.shape[1]), axis=0)
        acc_ref[:DA_PAIR, cols] = d if first else acc_ref[:DA_PAIR, cols] + d
        acc_ref[psum, cols] = part if first else acc_ref[psum, cols] + part

    def plain_chunk(j):
        p = jnp.exp2(jnp.dot(keys(j), qx_ref[...], preferred_element_type=F32))
        plain_update(j, p.astype(BF), slice(None), False)

    def plain_diagonal(c):
        kj = keys(nfull + c)
        for half in range(2):
            cols = slice(half * tq + c * tk, (half + 1) * tq)
            p = jnp.exp2(jnp.dot(kj, qx_ref[:, cols], preferred_element_type=F32))
            p = jnp.where(diagonal_mask(p.shape, c, c * tk), p, 0.0)
            plain_update(nfull + c, p.astype(BF), cols, c == 0)

    @pl.when(bounded)
    def _():
        for c in range(ndiag):
            plain_diagonal(c)

        @pl.when((nfull & 1) == 1)
        def _():
            plain_chunk(0)

        def pair(pi, carry):
            j = (nfull & 1) + 2 * pi
            plain_chunk(j)
            plain_chunk(j + 1)
            return carry

        lax.fori_loop(0, nfull >> 1, pair, 0)
        acc_ref[DA_PAIR:DA_PAIR + 1] = jnp.sum(acc_ref[psum], axis=0, keepdims=True)

    def scores(j):
        return jnp.dot(keys(j), qx_ref[...], preferred_element_type=F32)

    def update(s, j):
        m_prev = m_ref[...]
        m_new = jnp.maximum(m_prev, jnp.max(s, axis=0, keepdims=True))
        p = jnp.exp2(s - m_new).astype(BF)
        acc_ref[...] = jnp.exp2(m_prev - m_new) * acc_ref[...] + jnp.dot(
            vt_ref[j], p, preferred_element_type=F32)
        m_ref[...] = m_new

    @pl.when(jnp.logical_not(bounded))
    def _():
        s = scores(nfull)
        s = jnp.where(diagonal_mask(s.shape, 0, 0), s, -jnp.inf)
        m0 = jnp.max(s, axis=0, keepdims=True)
        m_ref[...] = m0
        acc_ref[...] = jnp.dot(vt_ref[nfull], jnp.exp2(s - m0).astype(BF),
                               preferred_element_type=F32)
        for c in range(1, ndiag):
            s = scores(nfull + c)
            update(jnp.where(diagonal_mask(s.shape, c, 0), s, -jnp.inf), nfull + c)
        first = nfull & 1

        @pl.when(first == 1)
        def _():
            update(scores(0), 0)

        @pl.when(nfull >= 2)
        def _():
            sa_ref[...] = scores(first)

        def pair(pi, carry):
            j = first + 2 * pi
            sb_ref[...] = scores(j + 1)
            update(sa_ref[...], j)
            sa_ref[...] = scores(jnp.minimum(j + 2, nfull - 1))
            update(sb_ref[...], j + 1)
            return carry

        lax.fori_loop(0, nfull >> 1, pair, 0)

    on = acc_ref[:DA_PAIR] / acc_ref[DA_PAIR:DA_PAIR + 1]
    lam = _lambda_value(lamv_ref, lam_init)
    ot = on[:, :tq] - lam * on[:, tq:]
    ot = _rms(ot, 0) * g_ref[...] * (1.0 - lam_init)
    o_ref[...] = ot.T.astype(BF)


def _attn_prompt(qt, kb, vt, kn2, lamv, subln_g, dst, *, tq, lam_init):
    nb, nh, _, t = qt.shape
    nk, tk = vt.shape[2], vt.shape[4]
    nq = t // tq
    return pl.pallas_call(
        functools.partial(_attn_prompt_kernel, tq=tq, tk=tk, lam_init=lam_init),
        grid=(nb, nh, nq),
        in_specs=[
            pl.BlockSpec((None, None, DA_PAIR, tq), lambda b, h, i: (b, h, 0, i)),
            pl.BlockSpec((None, None, t, DA_PAIR), lambda b, h, i: (b, h, 0, 0)),
            pl.BlockSpec((None, None, nk, DA_VT_ROWS, tk), lambda b, h, i: (b, h, 0, 0, 0)),
            pl.BlockSpec((None, None, nk, 8, LANES), lambda b, h, i: (b, h, 0, 0, 0)),
            pl.BlockSpec((4, DA_HEAD_DIM), lambda b, h, i: (0, 0)),
            pl.BlockSpec((DA_PAIR, 1), lambda b, h, i: (0, 0)),
            ANY_SPEC,
        ],
        out_specs=pl.BlockSpec((tq, DA_PAIR), lambda b, h, i: (b * nq + i, h)),
        out_shape=jax.ShapeDtypeStruct(dst.shape, dst.dtype),
        scratch_shapes=[pltpu.VMEM((DA_PAIR, 2 * tq), BF),
                        pltpu.VMEM((tk, 2 * tq), F32), pltpu.VMEM((tk, 2 * tq), F32),
                        pltpu.VMEM((1, 2 * tq), F32), pltpu.VMEM((DA_VT_ROWS, 2 * tq), F32)],
        input_output_aliases={6: 0},
        compiler_params=_cparams("parallel", "parallel", "arbitrary"),
        name="attn_prompt",
    )(qt, kb, vt, kn2, lamv, subln_g.reshape(DA_PAIR, 1), dst)


def _rope_rows_kernel(zq_ref, zk_ref, cos_ref, sin_ref, q_ref, k_ref):
    cos, sin = cos_ref[...], sin_ref[...]
    scale = DA_HEAD_DIM ** -0.5
    for h in range(DA_HEADS):
        sl = slice(h * DA_PAIR, (h + 1) * DA_PAIR)
        q_ref[:, sl] = (_rope128(zq_ref[:, sl], cos, sin) * scale).astype(BF)
        k_ref[:, sl] = _rope128(zk_ref[:, sl], cos, sin)


def _rope_rows(z, cos, sin, *, row0, nrows):
    rb = row0 // nrows
    return pl.pallas_call(
        _rope_rows_kernel,
        grid=(1,),
        in_specs=[
            pl.BlockSpec((nrows, DA_WIDTH), lambda i: (rb, 0)),
            pl.BlockSpec((nrows, DA_WIDTH), lambda i: (rb, 1)),
            pl.BlockSpec((nrows, LANES), lambda i: (0, 0)),
            pl.BlockSpec((nrows, LANES), lambda i: (0, 0)),
        ],
        out_specs=[pl.BlockSpec((nrows, DA_WIDTH), lambda i: (0, 0)),
                   pl.BlockSpec((nrows, DA_WIDTH), lambda i: (0, 0))],
        out_shape=[jax.ShapeDtypeStruct((nrows, DA_WIDTH), BF),
                   jax.ShapeDtypeStruct((nrows, DA_WIDTH), F32)],
        compiler_params=_cparams("arbitrary"),
        name="rope_rows",
    )(z, z, cos, sin)


def _attn_sample_kernel(pt_ref, q_ref, kn_ref, vn_ref, sel_ref, lamv_ref, g_ref, *rest,
                        pp, nsteps, ts, lam_init):
    kpages, vpages = rest[:pp], rest[pp:2 * pp]
    _, o_ref, m_ref, l_ref, acc_ref, qbd_ref = rest[2 * pp:]
    s = pl.program_id(1)
    nrow = 2 * DA_HEADS * ts

    @pl.when(s == 0)
    def _():
        q = q_ref[...]
        qrep = jnp.concatenate([q] * (2 * DA_HEADS), axis=0)
        qbd_ref[...] = jnp.where(sel_ref[...] > 0, qrep, jnp.zeros_like(qrep))
        m_ref[...] = jnp.full(m_ref.shape, -jnp.inf, F32)
        l_ref[...] = jnp.zeros_like(l_ref)
        acc_ref[...] = jnp.zeros_like(acc_ref)

    def update(sc, vv, causal):
        if causal:
            tq = lax.broadcasted_iota(jnp.int32, sc.shape, 0) & (ts - 1)
            tk = lax.broadcasted_iota(jnp.int32, sc.shape, 1)
            sc = jnp.where(tk <= tq, sc, -jnp.inf)
        m_prev = m_ref[...]
        m_new = jnp.maximum(m_prev, jnp.max(sc, axis=1, keepdims=True))
        alpha = jnp.exp(m_prev - m_new)
        p = jnp.exp(sc - m_new)
        l_ref[...] = alpha * l_ref[...] + jnp.sum(p, axis=1, keepdims=True)
        acc_ref[...] = alpha * acc_ref[...] + jnp.dot(p.astype(BF), vv, preferred_element_type=F32)
        m_ref[...] = m_new

    if pp:
        kt = jnp.concatenate([r[...] for r in kpages], axis=1).astype(BF)
        vv = jnp.concatenate(
            [jnp.concatenate([r[h] for h in range(DA_HEADS)], axis=1) for r in vpages],
            axis=0).astype(BF)
        update(jnp.dot(qbd_ref[...], kt, preferred_element_type=F32), vv, False)

    @pl.when(s == nsteps - 1)
    def _():
        sc_new = lax.dot_general(qbd_ref[...], kn_ref[...].astype(BF), (((1,), (1,)), ((), ())),
                                 preferred_element_type=F32)
        update(sc_new, vn_ref[...].astype(BF), True)
        on = acc_ref[...] / l_ref[...]
        lam = _lambda_value(lamv_ref, lam_init)
        half = nrow // 2
        d = on[:half] - lam * on[half:]
        outs = []
        for h in range(DA_HEADS):
            blk = d[h * ts:(h + 1) * ts, h * DA_PAIR:(h + 1) * DA_PAIR]
            outs.append(_rms(blk, -1) * g_ref[...] * (1.0 - lam_init))
        o_ref[...] = jnp.concatenate(outs, axis=1).astype(BF)


def _attn_sample(q_s, k_new, z, cache_kt, cache_vh, page_table, lamv, subln_g, dst, *,
                 ts, layer, zrow0, row0, lam_init):
    nb = q_s.shape[0] // ts
    n_pages = page_table.shape[1]
    pp = 0
    for cand in (16, 8, 4, 2, 1):
        if n_pages and n_pages % cand == 0:
            pp = cand
            break
    nsteps = max(n_pages // pp, 1) if pp else 1
    nrow = 2 * DA_HEADS * ts
    r = np.arange(nrow)[:, None] // ts
    c = np.arange(DA_WIDTH)[None, :] // DA_HEAD_DIM
    sel = jnp.asarray(((r % DA_HEADS) * 2 + r // DA_HEADS == c).astype(np.float32))
    rb, zb = row0 // ts, zrow0 // ts

    def kpage_spec(i):
        return pl.BlockSpec((None, None, DA_WIDTH, PAGE_SIZE),
                            lambda b, s, pt: (pt[b * n_pages + s * pp + i], layer, 0, 0))

    def vpage_spec(i):
        return pl.BlockSpec((None, None, DA_HEADS, PAGE_SIZE, DA_PAIR),
                            lambda b, s, pt: (pt[b * n_pages + s * pp + i], layer, 0, 0, 0))

    in_specs = [
        pl.BlockSpec((ts, DA_WIDTH), lambda b, s, pt: (b, 0)),
        pl.BlockSpec((ts, DA_WIDTH), lambda b, s, pt: (b, 0)),
        pl.BlockSpec((ts, DA_WIDTH), lambda b, s, pt: (zb + b, 2)),
        pl.BlockSpec((nrow, DA_WIDTH), lambda b, s, pt: (0, 0)),
        pl.BlockSpec((4, DA_HEAD_DIM), lambda b, s, pt: (0, 0)),
        pl.BlockSpec((1, DA_PAIR), lambda b, s, pt: (0, 0)),
    ] + [kpage_spec(i) for i in range(pp)] + [vpage_spec(i) for i in range(pp)] + [ANY_SPEC]
    grid_spec = pltpu.PrefetchScalarGridSpec(
        num_scalar_prefetch=1,
        grid=(nb, nsteps),
        in_specs=in_specs,
        out_specs=pl.BlockSpec((ts, DA_WIDTH), lambda b, s, pt: (rb + b, 0)),
        scratch_shapes=[pltpu.VMEM((nrow, 1), F32), pltpu.VMEM((nrow, 1), F32),
                        pltpu.VMEM((nrow, DA_WIDTH), F32), pltpu.VMEM((nrow, DA_WIDTH), BF)],
    )
    return pl.pallas_call(
        functools.partial(_attn_sample_kernel, pp=pp, nsteps=nsteps, ts=ts, lam_init=lam_init),
        grid_spec=grid_spec,
        out_shape=jax.ShapeDtypeStruct(dst.shape, dst.dtype),
        input_output_aliases={7 + 2 * pp: 0},
        compiler_params=_cparams("parallel", "arbitrary"),
        name="attn_sample",
    )(page_table.reshape(-1), q_s, k_new, z, sel, lamv, subln_g.reshape(1, DA_PAIR),
      *([cache_kt] * pp), *([cache_vh] * pp), dst)


def _sgu_norm_v(sv_ref, ng_ref, g):
    x = jax.nn.gelu(sv_ref[:, g * SGU_GROUP_DIM:(g + 1) * SGU_GROUP_DIM])
    return _rms(x, -1) * ng_ref[g:g + 1, :]


def _sgu_prompt_kernel(u_ref, sv_ref, ng_ref, w_ref, bs_ref, dst_ref, y_ref, vr_ref, *, tm, c):
    del dst_ref
    last = pl.program_id(1) == pl.num_programs(1) - 1
    ri = lax.broadcasted_iota(jnp.int32, (c, c), 0)
    ci = lax.broadcasted_iota(jnp.int32, (c, c), 1)
    for g in range(SGU_GROUPS):
        sl = slice(g * SGU_GROUP_DIM, (g + 1) * SGU_GROUP_DIM)
        v = _sgu_norm_v(sv_ref, ng_ref, g)

        @pl.when(last)
        def _():
            vr_ref[:, sl] = v[tm - c:]

        vb = v.astype(BF)
        w = jnp.where(ri >= ci, w_ref[g], 0.0).astype(BF)
        for n in range(tm // c):
            rows = slice(n * c, (n + 1) * c)
            mixed = jnp.dot(w, vb[rows], preferred_element_type=F32) + bs_ref[g]
            y_ref[rows, sl] = (jax.nn.gelu(u_ref[rows, sl]) * mixed).astype(BF)


def _sgu_prompt(z, norm_g, w_s, b_s, dst, *, nb, t):
    c = SGU_CHUNK
    tm = _pick_tile(t, 512, c)
    nt = t // tm
    bs = jnp.broadcast_to(b_s[:, :c, None], (SGU_GROUPS, c, SGU_GROUP_DIM))
    return pl.pallas_call(
        functools.partial(_sgu_prompt_kernel, tm=tm, c=c),
        grid=(nb, nt),
        in_specs=[
            pl.BlockSpec((tm, SGU_WIDTH), lambda b, i: (b * nt + i, 0)),
            pl.BlockSpec((tm, SGU_WIDTH), lambda b, i: (b * nt + i, 1)),
            pl.BlockSpec((SGU_GROUPS, SGU_GROUP_DIM), lambda b, i: (0, 0)),
            pl.BlockSpec((SGU_GROUPS, c, c), lambda b, i: (0, 0, 0)),
            pl.BlockSpec((SGU_GROUPS, c, SGU_GROUP_DIM), lambda b, i: (0, 0, 0)),
            ANY_SPEC,
        ],
        out_specs=[
            pl.BlockSpec((tm, SGU_WIDTH), lambda b, i: (b * nt + i, 0)),
            pl.BlockSpec((None, c, SGU_WIDTH), lambda b, i: (b, 0, 0)),
        ],
        out_shape=[jax.ShapeDtypeStruct(dst.shape, dst.dtype),
                   jax.ShapeDtypeStruct((nb, c, SGU_WIDTH), F32)],
        input_output_aliases={5: 0},
        compiler_params=_cparams("parallel", "arbitrary"),
        name="sgu_prompt",
    )(z, z, norm_g, w_s[:, :c, :c], bs, dst)


def _sgu_sample_kernel(u_ref, sv_ref, ng_ref, w_ref, bs_ref, dst_ref, y_ref, vr_ref, *, nb, ts):
    del dst_ref
    v = jnp.concatenate([_sgu_norm_v(sv_ref, ng_ref, g) for g in range(SGU_GROUPS)], axis=1)
    vr_ref[...] = v
    v3 = v.reshape(nb, ts, SGU_WIDTH)
    ii = lax.broadcasted_iota(jnp.int32, (ts, SGU_WIDTH), 0)
    mixed = jnp.broadcast_to(bs_ref[...][None], (nb, ts, SGU_WIDTH))
    for j in range(ts):
        wj = jnp.where(ii >= j, w_ref[j], 0.0)
        mixed = mixed + wj[None] * v3[:, j:j + 1, :]
    y = jax.nn.gelu(u_ref[...]).reshape(nb, ts, SGU_WIDTH) * mixed
    y_ref[...] = y.reshape(nb * ts, SGU_WIDTH).astype(BF)


def _sgu_sample(z, norm_g, w_s, b_s, dst, *, nb, ts, row0):
    nrows = nb * ts
    rb = row0 // nrows
    w_exp = jnp.repeat(jnp.transpose(w_s[:, :ts, :ts], (2, 1, 0)), SGU_GROUP_DIM, axis=2)
    b_exp = jnp.repeat(jnp.transpose(b_s[:, :ts], (1, 0)), SGU_GROUP_DIM, axis=1)
    return pl.pallas_call(
        functools.partial(_sgu_sample_kernel, nb=nb, ts=ts),
        grid=(1,),
        in_specs=[
            pl.BlockSpec((nrows, SGU_WIDTH), lambda i: (rb, 0)),
            pl.BlockSpec((nrows, SGU_WIDTH), lambda i: (rb, 1)),
            pl.BlockSpec((SGU_GROUPS, SGU_GROUP_DIM), lambda i: (0, 0)),
            pl.BlockSpec((ts, ts, SGU_WIDTH), lambda i: (0, 0, 0)),
            pl.BlockSpec((ts, SGU_WIDTH), lambda i: (0, 0)),
            ANY_SPEC,
        ],
        out_specs=[pl.BlockSpec((nrows, SGU_WIDTH), lambda i: (rb, 0)),
                   pl.BlockSpec((nrows, SGU_WIDTH), lambda i: (0, 0))],
        out_shape=[jax.ShapeDtypeStruct(dst.shape, dst.dtype),
                   jax.ShapeDtypeStruct((nrows, SGU_WIDTH), F32)],
        input_output_aliases={5: 0},
        compiler_params=_cparams("arbitrary"),
        name="sgu_sample",
    )(z, z, norm_g, w_exp, b_exp, dst)


def _log_sigmoid(x):
    return jnp.minimum(x, 0.0) - jnp.log(1.0 + jnp.exp(-jnp.abs(x)))


def _seg_scan(x, seg, pos, reverse):
    n = x.shape[0]
    r = pos & (seg - 1)
    t = 1
    while t < seg:
        if reverse:
            x = x + jnp.where(r < seg - t, pltpu.roll(x, n - t, 0), 0.0)
        else:
            x = x + jnp.where(r >= t, pltpu.roll(x, t, 0), 0.0)
        t *= 2
    return x


def _gla_decays(gk, c):
    n, w = gk.shape
    pos = lax.broadcasted_iota(jnp.int32, gk.shape, 0)
    out = {1: (jnp.exp(gk), None)}
    s = 2
    while s <= c:
        if s <= 8 or n % s:
            pre = _seg_scan(gk, s, pos, False)
            suf = _seg_scan(gk, s, pos, True) - gk
        else:
            h = s // 2
            p4 = pre.reshape(n // s, 2, h, w)
            s4 = suf.reshape(n // s, 2, h, w)
            tot = p4[:, :, h - 1:h, :]
            pre = jnp.concatenate([p4[:, 0:1], p4[:, 1:2] + tot[:, 0:1]], axis=1).reshape(n, w)
            suf = jnp.concatenate([s4[:, 0:1] + tot[:, 1:2], s4[:, 1:2]], axis=1).reshape(n, w)
        out[s] = (jnp.exp(pre), jnp.exp(suf))
        s *= 2
    return out


def _gla_intra(q, k, v, dec, c, sc):
    n = q.shape[0]
    ri = lax.broadcasted_iota(jnp.int32, (sc, sc), 0)
    ci = lax.broadcasted_iota(jnp.int32, (sc, sc), 1)
    same_chunk = _shr(ri, c) == _shr(ci, c)
    lane = lax.broadcasted_iota(jnp.int32, (sc, LANES), 1)
    levels = []
    levels.append((ri == ci, q.astype(BF), k.astype(BF)))
    s = 1
    while s < c:
        pre, _ = dec[s]
        suf = dec[s][1]
        ks = k if suf is None else k * suf
        mask = same_chunk & (_shr(ri, 2 * s) == _shr(ci, 2 * s)) & ((_shr(ri, s) & 1) == 1) \
            & ((_shr(ci, s) & 1) == 0)
        levels.append((mask, (q * pre).astype(BF), ks.astype(BF)))
        s *= 2
    outs = []
    for h in range(GLA_HEADS):
        pair = slice((h // 2) * LANES, (h // 2 + 1) * LANES)
        own = _shr(lane, GLA_K_DIM) == (h % 2)
        vh = v[:, h * GLA_V_DIM:(h + 1) * GLA_V_DIM].astype(BF)
        rows_out = []
        for b0 in range(0, n, sc):
            rows = slice(b0, b0 + sc)
            a = jnp.zeros((sc, sc), F32)
            for mask, qs, ks in levels:
                qm = jnp.where(own, qs[rows, pair], jnp.zeros((sc, LANES), BF))
                al = lax.dot_general(qm, ks[rows, pair], (((1,), (1,)), ((), ())),
                                     preferred_element_type=F32)
                a = a + jnp.where(mask, al, 0.0)
            rows_out.append(jnp.dot(a.astype(BF), vh[rows], preferred_element_type=F32))
        outs.append(jnp.concatenate(rows_out, axis=0) if len(rows_out) > 1 else rows_out[0])
    return outs


def _gla_gate(glr_ref, w2_ref, gb_ref):
    pre = jnp.dot(glr_ref[...].astype(BF), w2_ref[...], preferred_element_type=F32) + gb_ref[...]
    return _log_sigmoid(pre) / GLA_GATE_NORMALIZER


def _gla_finish(o_heads, g_ref, gn_ref, o_ref):
    for h in range(GLA_HEADS):
        sl = slice(h * GLA_V_DIM, (h + 1) * GLA_V_DIM)
        gate = g_ref[:, sl]
        o_ref[:, sl] = (_rms(o_heads[h], -1) * gn_ref[...] * (gate * jax.nn.sigmoid(gate))).astype(BF)


def _gla_prompt_kernel(q_ref, k_ref, v_ref, g_ref, glr_ref, w2_ref, gb_ref, gn_ref, dst_ref,
                       o_ref, s_out_ref, st_ref, *, tm, c):
    del dst_ref
    i = pl.program_id(1)

    @pl.when(i == 0)
    def _():
        st_ref[...] = jnp.zeros_like(st_ref)

    gk = _gla_gate(glr_ref, w2_ref, gb_ref)
    q = q_ref[...] * (GLA_K_DIM ** -0.5)
    k = k_ref[...]
    v = v_ref[...]
    dec = _gla_decays(gk, c)
    o_heads = _gla_intra(q, k, v, dec, c, min(tm, LANES))
    eb, esuf = dec[c]
    qd = (q * eb).astype(BF)
    kd = (k * esuf).astype(BF)
    lane = lax.broadcasted_iota(jnp.int32, (c, LANES), 1)
    for h in range(GLA_HEADS):
        pair = slice((h // 2) * LANES, (h // 2 + 1) * LANES)
        own = _shr(lane, GLA_K_DIM) == (h % 2)
        vh = v[:, h * GLA_V_DIM:(h + 1) * GLA_V_DIM].astype(BF)
        st = st_ref[h]
        inter = []
        for n in range(tm // c):
            rows = slice(n * c, (n + 1) * c)
            qm = jnp.where(own, qd[rows, pair], jnp.zeros((c, LANES), BF))
            inter.append(lax.dot_general(qm, st.astype(BF), (((1,), (1,)), ((), ())),
                                         preferred_element_type=F32))
            upd = lax.dot_general(vh[rows], kd[rows, pair], (((0,), (0,)), ((), ())),
                                  preferred_element_type=F32)
            st = eb[n * c + c - 1:n * c + c, pair] * st + upd
        st_ref[h] = st
        o_heads[h] = o_heads[h] + jnp.concatenate(inter, axis=0)
    _gla_finish(o_heads, g_ref, gn_ref, o_ref)

    @pl.when(i == pl.num_programs(1) - 1)
    def _():
        for h in range(GLA_HEADS):
            lo = (h % 2) * GLA_K_DIM
            s_out_ref[h] = st_ref[h].T[lo:lo + GLA_K_DIM, :]


def _gla_prompt(z, w2, gate_b, gla_norm, dst, *, nb, t):
    c = GLA_CHUNK
    tm = _pick_tile(t, 256, LANES)
    nt = t // tm
    return pl.pallas_call(
        functools.partial(_gla_prompt_kernel, tm=tm, c=c),
        grid=(nb, nt),
        in_specs=[
            pl.BlockSpec((tm, GLA_K_WIDTH), lambda b, i: (b * nt + i, 4)),
            pl.BlockSpec((tm, GLA_K_WIDTH), lambda b, i: (b * nt + i, 5)),
            pl.BlockSpec((tm, GLA_V_WIDTH), lambda b, i: (b * nt + i, 3)),
            pl.BlockSpec((tm, GLA_V_WIDTH), lambda b, i: (b * nt + i, 4)),
            pl.BlockSpec((tm, GLA_RANK_PAD), lambda b, i: (b * nt + i, 20)),
            pl.BlockSpec((GLA_RANK_PAD, GLA_K_WIDTH), lambda b, i: (0, 0)),
            pl.BlockSpec((1, GLA_K_WIDTH), lambda b, i: (0, 0)),
            pl.BlockSpec((1, GLA_V_DIM), lambda b, i: (0, 0)),
            ANY_SPEC,
        ],
        out_specs=[
            pl.BlockSpec((tm, GLA_V_WIDTH), lambda b, i: (b * nt + i, 0)),
            pl.BlockSpec((None, GLA_HEADS, GLA_K_DIM, GLA_V_DIM), lambda b, i: (b, 0, 0, 0)),
        ],
        out_shape=[jax.ShapeDtypeStruct(dst.shape, dst.dtype),
                   jax.ShapeDtypeStruct((nb, GLA_HEADS, GLA_K_DIM, GLA_V_DIM), F32)],
        scratch_shapes=[pltpu.VMEM((GLA_HEADS, GLA_V_DIM, LANES), F32)],
        input_output_aliases={8: 0},
        compiler_params=_cparams("parallel", "arbitrary"),
        name="gla_prompt",
    )(z, z, z, z, z, w2, gate_b.reshape(1, GLA_K_WIDTH), gla_norm.reshape(1, GLA_V_DIM), dst)


def _gla_sample_kernel(q_ref, k_ref, v_ref, g_ref, glr_ref, w2_ref, gb_ref, gbc_ref, gn_ref, s0_ref,
                       dst_ref, o_ref, s_out_ref, *, ts):
    del dst_ref
    gk = _gla_gate(glr_ref, w2_ref, gb_ref)
    q = q_ref[...] * (GLA_K_DIM ** -0.5)
    k = k_ref[...]
    v = v_ref[...]
    dec = _gla_decays(gk, ts)
    o_heads = _gla_intra(q, k, v, dec, ts, ts)
    eb, esuf = dec[ts]
    qd = (q * eb).astype(BF)
    kd = (k * esuf).astype(BF)
    pre_t = lax.dot_general(w2_ref[...], glr_ref[...].astype(BF), (((0,), (1,)), ((), ())),
                            preferred_element_type=F32) + gbc_ref[...]
    dlast = jnp.exp(jnp.sum(_log_sigmoid(pre_t) / GLA_GATE_NORMALIZER, axis=1, keepdims=True))
    lane = lax.broadcasted_iota(jnp.int32, (ts, LANES), 1)
    for h in range(GLA_HEADS):
        pair = slice((h // 2) * LANES, (h // 2 + 1) * LANES)
        own = _shr(lane, GLA_K_DIM) == (h % 2)
        vh = v[:, h * GLA_V_DIM:(h + 1) * GLA_V_DIM].astype(BF)
        s_pair = s0_ref[pair, :]
        qm = jnp.where(own, qd[:, pair], jnp.zeros((ts, LANES), BF))
        o_heads[h] = o_heads[h] + jnp.dot(qm, s_pair.astype(BF), preferred_element_type=F32)
        upd = lax.dot_general(kd[:, pair], vh, (((0,), (0,)), ((), ())),
                              preferred_element_type=F32)
        hr = slice(h * GLA_K_DIM, (h + 1) * GLA_K_DIM)
        lo = (h % 2) * GLA_K_DIM
        s_out_ref[hr, :] = dlast[hr] * s0_ref[hr, :] + upd[lo:lo + GLA_K_DIM]
    _gla_finish(o_heads, g_ref, gn_ref, o_ref)


def _gla_sample(z, w2, gate_b, gla_norm, s0, dst, *, nb, ts, row0):
    rb = row0 // ts
    s0r = s0.reshape(nb, GLA_HEADS * GLA_K_DIM, GLA_V_DIM)
    o, s_new = pl.pallas_call(
        functools.partial(_gla_sample_kernel, ts=ts),
        grid=(nb,),
        in_specs=[
            pl.BlockSpec((ts, GLA_K_WIDTH), lambda b: (rb + b, 4)),
            pl.BlockSpec((ts, GLA_K_WIDTH), lambda b: (rb + b, 5)),
            pl.BlockSpec((ts, GLA_V_WIDTH), lambda b: (rb + b, 3)),
            pl.BlockSpec((ts, GLA_V_WIDTH), lambda b: (rb + b, 4)),
            pl.BlockSpec((ts, GLA_RANK_PAD), lambda b: (rb + b, 20)),
            pl.BlockSpec((GLA_RANK_PAD, GLA_K_WIDTH), lambda b: (0, 0)),
            pl.BlockSpec((1, GLA_K_WIDTH), lambda b: (0, 0)),
            pl.BlockSpec((GLA_K_WIDTH, 1), lambda b: (0, 0)),
            pl.BlockSpec((1, GLA_V_DIM), lambda b: (0, 0)),
            pl.BlockSpec((None, GLA_HEADS * GLA_K_DIM, GLA_V_DIM), lambda b: (b, 0, 0)),
            ANY_SPEC,
        ],
        out_specs=[
            pl.BlockSpec((ts, GLA_V_WIDTH), lambda b: (rb + b, 0)),
            pl.BlockSpec((None, GLA_HEADS * GLA_K_DIM, GLA_V_DIM), lambda b: (b, 0, 0)),
        ],
        out_shape=[jax.ShapeDtypeStruct(dst.shape, dst.dtype),
                   jax.ShapeDtypeStruct((nb, GLA_HEADS * GLA_K_DIM, GLA_V_DIM), F32)],
        input_output_aliases={10: 0},
        compiler_params=_cparams("parallel"),
        name="gla_sample",
    )(z, z, z, z, z, w2, gate_b.reshape(1, GLA_K_WIDTH), gate_b.reshape(GLA_K_WIDTH, 1),
      gla_norm.reshape(1, GLA_V_DIM), s0r, dst)
    return o, s_new.reshape(nb, GLA_HEADS, GLA_K_DIM, GLA_V_DIM)


def kernel(x_prompt, x_sample, cache_k, cache_v, page_table, state_pool, state_gla, ffn1_norm, ffn1_w_gate, ffn1_w_up, ffn1_w_down, mix_norm, ffn2_norm, ffn2_w_gate, ffn2_w_up, ffn2_w_down, even_w_in, even_w_out, pool_w, pool_scale, diff_lambda, diff_subln, odd_w_in, odd_w_out, sgu_norm, sgu_w, sgu_b, gla_gate_w2, gla_gate_b, gla_norm, final_norm):
    nb, t, d = x_prompt.shape
    nbs, ts, _ = x_sample.shape
    depth = ffn1_norm.shape[0]
    n_p, n_s = nb * t, nbs * ts
    past_len = page_table.shape[1] * PAGE_SIZE
    assert ts < min(SGU_CHUNK, GLA_CHUNK) and ts & (ts - 1) == 0 and n_p % n_s == 0
    assert t % SGU_CHUNK == 0 and t >= POOL_PAD

    n = n_p + n_s
    x = jnp.concatenate([x_prompt.reshape(n_p, d), x_sample.reshape(n_s, d)], axis=0)
    tk = _pick_tile(t, 512, LANES)
    tq = _pick_tile(t, 2 * tk, tk)
    cos_p, sin_p = _rope_tables(jnp.arange(t))
    cos_s, sin_s = _rope_tables(past_len + jnp.arange(ts))
    cos_s, sin_s = jnp.tile(cos_s, (nbs, 1)), jnp.tile(sin_s, (nbs, 1))

    n_even = (depth + 1) // 2
    cache_kt = jnp.transpose(cache_k, (0, 1, 3, 4, 5, 2)).reshape(
        cache_k.shape[0], cache_k.shape[1], DA_WIDTH, PAGE_SIZE)
    cache_vh = jnp.transpose(cache_v, (0, 1, 3, 2, 4))
    kt_all = jnp.zeros((nb, n_even, DA_HEADS, DA_PAIR, t), F32)
    vh_all = jnp.zeros((nb, n_even, DA_HEADS, t, DA_PAIR), F32)
    pool_p, sgu_p, gla_p = [], [], []
    k_s, v_s, pool_s, sgu_s, gla_s = [], [], [], [], []
    for l in range(depth):
        i = l // 2
        x = _ffn(x, ffn1_norm[l], ffn1_w_gate, ffn1_w_up, ffn1_w_down, layer=l)
        if l % 2 == 0:
            lam_init = 0.8 - 0.6 * math.exp(-0.3 * l)
            w_in = even_w_in[i]
            w_in = jnp.concatenate([w_in[:, POOL_WIDTH:], w_in[:, :POOL_WIDTH]], axis=1).astype(BF)
            qt, kb, vt, kn2, zp, kt_all, vh_all = _inproj_prep(
                x, mix_norm[l], w_in, cos_p, sin_p, kt_all, vh_all, layer=i, nb=nb, t=t, tk=tk)
            z_s = _inproj(x, mix_norm[l], w_in, row0=n_p, nrows=n_s)
            pcol = 3 * DA_WIDTH // POOL_WIDTH
            w_bd = _block_diag(pool_w[i]).astype(BF)
            a1 = jnp.zeros((n, POOL_WIDTH), BF)
            a1 = _pool(zp, zp, w_bd, pool_scale[i], a1, nb=nb, t=t, zrow0=0, row0=0, col_blk=0,
                       prev_is_state=False, pos0=0)
            st_pad = jnp.pad(state_pool[i], ((0, 0), (POOL_PAD - POOL_HIST, 0), (0, 0)))
            a1 = _pool(z_s, st_pad.reshape(nbs * POOL_PAD, POOL_WIDTH), w_bd, pool_scale[i], a1,
                       nb=nbs, t=ts, zrow0=0, row0=n_p, col_blk=pcol, prev_is_state=True,
                       pos0=past_len)
            a2 = jnp.zeros((n, DA_WIDTH), BF)
            a2 = _attn_prompt(qt, kb, vt, kn2, diff_lambda[i], diff_subln[i], a2, tq=tq,
                              lam_init=lam_init)
            q_s, k_new = _rope_rows(z_s, cos_s, sin_s, row0=0, nrows=n_s)
            a2 = _attn_sample(q_s, k_new, z_s, cache_kt, cache_vh, page_table, diff_lambda[i],
                              diff_subln[i], a2, ts=ts, layer=i, zrow0=0, row0=n_p,
                              lam_init=lam_init)
            w_out = even_w_out[i].astype(BF)
            mix = (a1, a2, w_out[:POOL_WIDTH], w_out[POOL_WIDTH:])
            zs = z_s.reshape(nbs, ts, -1)
            pool_p.append(zp.reshape(nb, t, POOL_WIDTH)[:, t - POOL_HIST:])
            k_s.append(k_new.reshape(nbs, ts, DA_WIDTH))
            v_s.append(zs[:, :, 2 * DA_WIDTH:3 * DA_WIDTH])
            pool_s.append(jnp.concatenate([state_pool[i], zs[:, :, 3 * DA_WIDTH:]], axis=1)[:, -POOL_HIST:])
        else:
            w_in = jnp.pad(odd_w_in[i], ((0, 0), (0, GLA_RANK_PAD - GLA_GATE_RANK))).astype(BF)
            z = _inproj(x, mix_norm[l], w_in)
            w2 = jnp.pad(gla_gate_w2[i], ((0, GLA_RANK_PAD - GLA_GATE_RANK), (0, 0))).astype(BF)
            a1 = jnp.zeros((n, SGU_WIDTH), BF)
            a1, vr_p = _sgu_prompt(z, sgu_norm[i], sgu_w[i], sgu_b[i], a1, nb=nb, t=t)
            a1, vr_s = _sgu_sample(z, sgu_norm[i], sgu_w[i], sgu_b[i], a1, nb=nbs, ts=ts, row0=n_p)
            a2 = jnp.zeros((n, GLA_V_WIDTH), BF)
            a2, s_p = _gla_prompt(z, w2, gla_gate_b[i], gla_norm[i], a2, nb=nb, t=t)
            a2, s_s = _gla_sample(z, w2, gla_gate_b[i], gla_norm[i], state_gla[i], a2, nb=nbs, ts=ts,
                                  row0=n_p)
            w_out = odd_w_out[i].astype(BF)
            mix = (a1, a2, w_out[:SGU_WIDTH], w_out[SGU_WIDTH:])
            sgu_p.append(vr_p)
            gla_p.append(s_p)
            sgu_s.append(vr_s.reshape(nbs, ts, SGU_WIDTH))
            gla_s.append(s_s)
        x = _ffn(x, ffn2_norm[l], ffn2_w_gate, ffn2_w_up, ffn2_w_down, layer=l, mix=mix)
    y_p = _final_norm(x, final_norm, row0=0, nrows=n_p)
    y_s = _final_norm(x, final_norm, row0=n_p, nrows=n_s)
    k_rows_p = jnp.transpose(kt_all.reshape(nb, n_even, DA_HEADS, 2, DA_HEAD_DIM, t), (0, 1, 5, 2, 3, 4))
    v_rows_p = jnp.transpose(vh_all, (0, 1, 3, 2, 4))

    return (y_p.reshape(nb, t, d), y_s.reshape(nbs, ts, d), k_rows_p, v_rows_p,
            jnp.stack(pool_p, axis=0), jnp.stack(sgu_p, axis=0), jnp.stack(gla_p, axis=0),
            jnp.stack(k_s, axis=1).reshape(nbs, -1, ts, DA_HEADS, 2, DA_HEAD_DIM),
            jnp.stack(v_s, axis=1).reshape(nbs, -1, ts, DA_HEADS, DA_PAIR),
            jnp.stack(pool_s, axis=0), jnp.stack(sgu_s, axis=0), jnp.stack(gla_s, axis=0))
```

```python
import functools
import math

import numpy as np
import jax
import jax.numpy as jnp
from jax import lax
from jax.experimental import pallas as pl
from jax.experimental.pallas import tpu as pltpu

F32 = jnp.float32
BF = jnp.bfloat16

RMS_EPS = 1e-6
ROPE_THETA = 10000.0
PAGE_SIZE = 128

POOL_WINDOWS = (2, 4, 8, 16)
POOL_GROUP_DIM = 64
POOL_WIDTH = 256
POOL_HIST = 15
POOL_PAD = 16

DA_HEADS = 6
DA_HEAD_DIM = 64
DA_PAIR = 2 * DA_HEAD_DIM
DA_WIDTH = DA_HEADS * DA_PAIR
DA_VT_ROWS = DA_PAIR + 16
LOG2E = 1.4426950408889634
DA_SAFE_LOG2 = 60.0

SGU_GROUPS = 4
SGU_GROUP_DIM = 128
SGU_WIDTH = 512
SGU_CHUNK = 128

GLA_HEADS = 4
GLA_K_DIM = 64
GLA_V_DIM = 128
GLA_K_WIDTH = 256
GLA_V_WIDTH = 512
GLA_GATE_RANK = 16
GLA_GATE_NORMALIZER = 16.0
GLA_CHUNK = 64
GLA_RANK_PAD = 128

LANES = 128
VMEM_LIMIT = 56 * 1024 * 1024

FFN_ROWS = 1280
FFN_COLS = 256
INPROJ_ROWS = 640
NORM_ROWS = 1024
MIXER_ROWS = 512
GLA_ROWS = 256
ATTN_KEY_CHUNK = 512
ATTN_QUERY_CHUNKS = 2
DECODE_PAGES = (16, 8, 4, 2, 1)


def _cparams(*sem):
    return pltpu.CompilerParams(dimension_semantics=sem, vmem_limit_bytes=VMEM_LIMIT)


def _pick_tile(n, cap, mult=8):
    best = None
    for t in range(mult, min(n, cap) + 1, mult):
        if n % t == 0:
            best = t
    assert best is not None, (n, cap, mult)
    return best


def _shr(x, pow2):
    assert pow2 & (pow2 - 1) == 0
    return x >> (pow2.bit_length() - 1)


def _block_diag(w):
    g, a, b = w.shape
    out = jnp.zeros((g * a, g * b), w.dtype)
    for i in range(g):
        out = out.at[i * a:(i + 1) * a, i * b:(i + 1) * b].set(w[i])
    return out


ANY_SPEC = pl.BlockSpec(memory_space=pl.ANY)


def _rms(x, axis):
    return x * lax.rsqrt(jnp.mean(x * x, axis=axis, keepdims=True) + RMS_EPS)


def _ffn_kernel(*refs, nf, mixed):
    if mixed:
        x_ref, a1_ref, a2_ref, w1_ref, w2_ref, g_ref, wg_ref, wu_ref, wd_ref, o_ref, hn_ref, acc_ref = refs
    else:
        x_ref, g_ref, wg_ref, wu_ref, wd_ref, o_ref, hn_ref, acc_ref = refs
    f = pl.program_id(1)

    @pl.when(f == 0)
    def _():
        x = x_ref[...]
        if mixed:
            x = x + jnp.dot(a1_ref[...], w1_ref[...], preferred_element_type=F32)
            x = x + jnp.dot(a2_ref[...], w2_ref[...], preferred_element_type=F32)
            o_ref[...] = x
        hn_ref[...] = (_rms(x, -1) * g_ref[...]).astype(BF)
        acc_ref[...] = jnp.zeros_like(acc_ref)

    hn = hn_ref[...]
    a = jnp.dot(hn, wg_ref[...].astype(BF), preferred_element_type=F32)
    u = jnp.dot(hn, wu_ref[...].astype(BF), preferred_element_type=F32)
    h = (a * jax.nn.sigmoid(a) * u).astype(BF)
    acc_ref[...] += jnp.dot(h, wd_ref[...].astype(BF), preferred_element_type=F32)

    @pl.when(f == nf - 1)
    def _():
        o_ref[...] = (o_ref[...] if mixed else x_ref[...]) + 0.5 * acc_ref[...]


def _ffn(x, g, wg, wu, wd, *, layer, mix=None):
    n, d = x.shape
    ff = wg.shape[2]
    tm = _pick_tile(n, FFN_ROWS)
    tf = _pick_tile(ff, FFN_COLS, LANES)
    nf = ff // tf
    mix_specs, mix_args = [], []
    if mix is not None:
        a1, a2, w1, w2 = mix
        mix_specs = [
            pl.BlockSpec((tm, a1.shape[1]), lambda m, f: (m, 0)),
            pl.BlockSpec((tm, a2.shape[1]), lambda m, f: (m, 0)),
            pl.BlockSpec(w1.shape, lambda m, f: (0, 0)),
            pl.BlockSpec(w2.shape, lambda m, f: (0, 0)),
        ]
        mix_args = [a1, a2, w1, w2]
    return pl.pallas_call(
        functools.partial(_ffn_kernel, nf=nf, mixed=mix is not None),
        grid=(n // tm, nf),
        in_specs=[pl.BlockSpec((tm, d), lambda m, f: (m, 0))] + mix_specs + [
            pl.BlockSpec((1, d), lambda m, f: (0, 0)),
            pl.BlockSpec((None, d, tf), lambda m, f: (layer, 0, f)),
            pl.BlockSpec((None, d, tf), lambda m, f: (layer, 0, f)),
            pl.BlockSpec((None, tf, d), lambda m, f: (layer, f, 0)),
        ],
        out_specs=pl.BlockSpec((tm, d), lambda m, f: (m, 0)),
        out_shape=jax.ShapeDtypeStruct((n, d), F32),
        scratch_shapes=[pltpu.VMEM((tm, d), BF), pltpu.VMEM((tm, d), F32)],
        compiler_params=_cparams("parallel", "arbitrary"),
        name="ffn",
    )(x, *mix_args, g.reshape(1, d), wg, wu, wd)


def _inproj_kernel(x_ref, g_ref, w_ref, o_ref):
    hn = (_rms(x_ref[...], -1) * g_ref[...]).astype(BF)
    o_ref[...] = jnp.dot(hn, w_ref[...], preferred_element_type=F32)


def _inproj(x, g, w, *, row0=0, nrows=None):
    d = x.shape[1]
    n = x.shape[0] if nrows is None else nrows
    nout = w.shape[1]
    tm = _pick_tile(math.gcd(n, row0) if row0 else n, INPROJ_ROWS)
    rb = row0 // tm
    return pl.pallas_call(
        _inproj_kernel,
        grid=(n // tm,),
        in_specs=[
            pl.BlockSpec((tm, d), lambda m: (rb + m, 0)),
            pl.BlockSpec((1, d), lambda m: (0, 0)),
            pl.BlockSpec((d, nout), lambda m: (0, 0)),
        ],
        out_specs=pl.BlockSpec((tm, nout), lambda m: (m, 0)),
        out_shape=jax.ShapeDtypeStruct((n, nout), F32),
        compiler_params=_cparams("parallel"),
        name="inproj",
    )(x, g.reshape(1, d), w)


def _final_norm_kernel(x_ref, g_ref, o_ref):
    o_ref[...] = _rms(x_ref[...], -1) * g_ref[...]


def _final_norm(x, g, *, row0, nrows):
    d = x.shape[1]
    tm = _pick_tile(math.gcd(nrows, row0) if row0 else nrows, NORM_ROWS)
    rb = row0 // tm
    return pl.pallas_call(
        _final_norm_kernel,
        grid=(nrows // tm,),
        in_specs=[pl.BlockSpec((tm, d), lambda m: (rb + m, 0)), pl.BlockSpec((1, d), lambda m: (0, 0))],
        out_specs=pl.BlockSpec((tm, d), lambda m: (m, 0)),
        out_shape=jax.ShapeDtypeStruct((nrows, d), F32),
        compiler_params=_cparams("parallel"),
        name="final_norm",
    )(x, g.reshape(1, d))


def _pool_kernel(pc_ref, pp_ref, w_ref, sc_ref, dst_ref, o_ref, *, tm, pos0, prev_at_first):
    del dst_ref
    i = pl.program_id(1)
    p = pc_ref[...]
    prev = pp_ref[...]
    if not prev_at_first:
        prev = jnp.where(i > 0, prev, 0.0)
    ext = jnp.concatenate([prev, p], axis=0)
    s2 = ext + pltpu.roll(ext, 1, 0)
    s4 = s2 + pltpu.roll(s2, 2, 0)
    s8 = s4 + pltpu.roll(s4, 4, 0)
    s16 = s8 + pltpu.roll(s8, 8, 0)
    grp = _shr(lax.broadcasted_iota(jnp.int32, (tm, POOL_WIDTH), 1), POOL_GROUP_DIM)
    pos = pos0 + i * tm + lax.broadcasted_iota(jnp.int32, (tm, POOL_WIDTH), 0)
    sums = (s2, s4, s8, s16)
    s = sums[3][POOL_PAD:]
    win = jnp.full((tm, POOL_WIDTH), POOL_WINDOWS[3], jnp.int32)
    for gi in (2, 1, 0):
        s = jnp.where(grp == gi, sums[gi][POOL_PAD:], s)
        win = jnp.where(grp == gi, POOL_WINDOWS[gi], win)
    cnt = jnp.minimum(pos + 1, win).astype(F32)
    diff = s / cnt - p
    y = jnp.dot(diff.astype(BF), w_ref[...], preferred_element_type=F32) * sc_ref[...]
    o_ref[...] = y.astype(BF)


def _pool(z, prev_src, w_bd, scale, dst, *, nb, t, zrow0, row0, col_blk, prev_is_state, pos0):
    tm = _pick_tile(t, MIXER_ROWS)
    nt = t // tm
    rb0, zb0 = row0 // tm, zrow0 // tm
    if prev_is_state:
        assert nt == 1
        prev_spec = pl.BlockSpec((POOL_PAD, POOL_WIDTH), lambda b, i: (b, 0))
    else:
        r16 = tm // POOL_PAD
        base16 = zrow0 // POOL_PAD
        prev_spec = pl.BlockSpec(
            (POOL_PAD, POOL_WIDTH),
            lambda b, i: (jnp.maximum(base16 + (b * nt + i) * r16 - 1, 0), col_blk))
    return pl.pallas_call(
        functools.partial(_pool_kernel, tm=tm, pos0=pos0, prev_at_first=prev_is_state),
        grid=(nb, nt),
        in_specs=[
            pl.BlockSpec((tm, POOL_WIDTH), lambda b, i: (zb0 + b * nt + i, col_blk)),
            prev_spec,
            pl.BlockSpec((POOL_WIDTH, POOL_WIDTH), lambda b, i: (0, 0)),
            pl.BlockSpec((1, POOL_WIDTH), lambda b, i: (0, 0)),
            ANY_SPEC,
        ],
        out_specs=pl.BlockSpec((tm, POOL_WIDTH), lambda b, i: (rb0 + b * nt + i, 0)),
        out_shape=jax.ShapeDtypeStruct(dst.shape, dst.dtype),
        input_output_aliases={4: 0},
        compiler_params=_cparams("parallel", "arbitrary"),
        name="pool",
    )(z, prev_src, w_bd, scale.reshape(1, POOL_WIDTH), dst)


def _rope_tables(pos):
    half = DA_HEAD_DIM // 2
    inv = ROPE_THETA ** (-jnp.arange(half, dtype=F32) / half)
    ang = pos.astype(F32)[:, None] * inv[None, :]
    cos, sin = jnp.cos(ang), jnp.sin(ang)
    cos128 = jnp.concatenate([cos, cos, cos, cos], axis=1)
    sin128 = jnp.concatenate([-sin, sin, -sin, sin], axis=1)
    return cos128, sin128


def _rope128(x, cos, sin):
    lane = lax.broadcasted_iota(jnp.int32, x.shape, 1)
    first = (lane & (DA_HEAD_DIM - 1)) < (DA_HEAD_DIM // 2)
    partner = jnp.where(first, pltpu.roll(x, LANES - 32, 1), pltpu.roll(x, 32, 1))
    return x * cos + partner * sin


def _inproj_prep_kernel(x_ref, g_ref, w_ref, cos_ref, sin_ref, kt_dst_ref, vh_dst_ref,
                        qt_ref, kb_ref, vt_ref, kn_ref, zp_ref, kt_ref, vh_ref):
    del kt_dst_ref, vh_dst_ref
    hn = (_rms(x_ref[...], -1) * g_ref[...]).astype(BF)
    z = jnp.dot(hn, w_ref[...], preferred_element_type=F32)
    cos, sin = cos_ref[...], sin_ref[...]
    scale = DA_HEAD_DIM ** -0.5 * LOG2E
    ones = jnp.ones((DA_VT_ROWS - DA_PAIR, z.shape[0]), BF)
    for h in range(DA_HEADS):
        c0 = h * DA_PAIR
        q = _rope128(z[:, c0:c0 + DA_PAIR], cos, sin) * scale
        qt_ref[h] = q.T.astype(BF)
        k = _rope128(z[:, DA_WIDTH + c0:DA_WIDTH + c0 + DA_PAIR], cos, sin)
        kt_ref[h] = k.T
        kb = k.astype(BF)
        kb_ref[h] = kb
        kf = kb.astype(F32)
        kn2 = jnp.max(jnp.sum(kf * kf, axis=1, keepdims=True), axis=0, keepdims=True)
        kn_ref[h] = jnp.broadcast_to(kn2, kn_ref.shape[1:])
        v = z[:, 2 * DA_WIDTH + c0:2 * DA_WIDTH + c0 + DA_PAIR]
        vh_ref[h] = v
        vt_ref[h, :DA_PAIR] = v.T.astype(BF)
        vt_ref[h, DA_PAIR:] = ones
    zp_ref[...] = z[:, 3 * DA_WIDTH:]


def _inproj_prep(x, g, w, cos, sin, kt_all, vh_all, *, layer, nb, t, tk):
    d = x.shape[1]
    nt = t // tk
    return pl.pallas_call(
        _inproj_prep_kernel,
        grid=(nb, nt),
        in_specs=[
            pl.BlockSpec((tk, d), lambda b, i: (b * nt + i, 0)),
            pl.BlockSpec((1, d), lambda b, i: (0, 0)),
            pl.BlockSpec(w.shape, lambda b, i: (0, 0)),
            pl.BlockSpec((tk, LANES), lambda b, i: (i, 0)),
            pl.BlockSpec((tk, LANES), lambda b, i: (i, 0)),
            ANY_SPEC,
            ANY_SPEC,
        ],
        out_specs=[
            pl.BlockSpec((None, DA_HEADS, DA_PAIR, tk), lambda b, i: (b, 0, 0, i)),
            pl.BlockSpec((None, DA_HEADS, tk, DA_PAIR), lambda b, i: (b, 0, i, 0)),
            pl.BlockSpec((None, DA_HEADS, None, DA_VT_ROWS, tk), lambda b, i: (b, 0, i, 0, 0)),
            pl.BlockSpec((None, DA_HEADS, None, 8, LANES), lambda b, i: (b, 0, i, 0, 0)),
            pl.BlockSpec((tk, POOL_WIDTH), lambda b, i: (b * nt + i, 0)),
            pl.BlockSpec((None, None, DA_HEADS, DA_PAIR, tk), lambda b, i: (b, layer, 0, 0, i)),
            pl.BlockSpec((None, None, DA_HEADS, tk, DA_PAIR), lambda b, i: (b, layer, 0, i, 0)),
        ],
        out_shape=[
            jax.ShapeDtypeStruct((nb, DA_HEADS, DA_PAIR, t), BF),
            jax.ShapeDtypeStruct((nb, DA_HEADS, t, DA_PAIR), BF),
            jax.ShapeDtypeStruct((nb, DA_HEADS, nt, DA_VT_ROWS, tk), BF),
            jax.ShapeDtypeStruct((nb, DA_HEADS, nt, 8, LANES), F32),
            jax.ShapeDtypeStruct((nb * t, POOL_WIDTH), F32),
            jax.ShapeDtypeStruct(kt_all.shape, kt_all.dtype),
            jax.ShapeDtypeStruct(vh_all.shape, vh_all.dtype),
        ],
        input_output_aliases={5: 5, 6: 6},
        compiler_params=_cparams("parallel", "parallel"),
        name="inproj_prep",
    )(x, g.reshape(1, d), w, cos, sin, kt_all, vh_all)


def _lambda_value(lamv_ref, lam_init):
    lv = lamv_ref[...]
    s01 = jnp.sum(lv[0:1] * lv[1:2], axis=1, keepdims=True)
    s23 = jnp.sum(lv[2:3] * lv[3:4], axis=1, keepdims=True)
    return jnp.exp(s01) - jnp.exp(s23) + lam_init


def _attn_prompt_kernel(qt_ref, k_ref, vt_ref, kn_ref, lamv_ref, g_ref, dst_ref, o_ref,
                        qx_ref, sa_ref, sb_ref, m_ref, acc_ref, *, tq, tk, lam_init):
    del dst_ref
    ndiag = tq // tk
    qi = pl.program_id(2)
    nfull = qi * ndiag
    qt = qt_ref[...]
    row = lax.broadcasted_iota(jnp.int32, qt.shape, 0)
    zero = jnp.zeros_like(qt)
    qx_ref[:, :tq] = jnp.where(row < DA_HEAD_DIM, qt, zero)
    qx_ref[:, tq:] = jnp.where(row >= DA_HEAD_DIM, qt, zero)
    qf = qt.astype(F32)
    qn2 = jnp.max(jnp.sum(qf * qf, axis=0, keepdims=True))
    bounded = qn2 * jnp.max(kn_ref[...]) < DA_SAFE_LOG2 ** 2

    def keys(j):
        return k_ref[pl.ds(pl.multiple_of(j * tk, tk), tk), :]

    def diagonal_mask(shape, c, col0):
        kpos = c * tk + lax.broadcasted_iota(jnp.int32, shape, 0)
        qpos = (col0 + lax.broadcasted_iota(jnp.int32, shape, 1)) & (tq - 1)
        return kpos <= qpos

    psum = slice(DA_PAIR, DA_PAIR + 8)

    def plain_update(j, p, cols, first):
        d = jnp.dot(vt_ref[j, :DA_PAIR], p, preferred_element_type=F32)
        part = jnp.sum(p.astype(F32).reshape(tk // 8, 8, p.shape[1]), axis=0)
        acc_ref[:DA_PAIR, cols] = d if first else acc_ref[:DA_PAIR, cols] + d
        acc_ref[psum, cols] = part if first else acc_ref[psum, cols] + part

    def plain_chunk(j):
        p = jnp.exp2(jnp.dot(keys(j), qx_ref[...], preferred_element_type=F32))
        plain_update(j, p.astype(BF), slice(None), False)

    def plain_diagonal(c):
        kj = keys(nfull + c)
        for half in range(2):
            cols = slice(half * tq + c * tk, (half + 1) * tq)
            p = jnp.exp2(jnp.dot(kj, qx_ref[:, cols], preferred_element_type=F32))
            p = jnp.where(diagonal_mask(p.shape, c, c * tk), p, 0.0)
            plain_update(nfull + c, p.astype(BF), cols, c == 0)

    @pl.when(bounded)
    def _():
        for c in range(ndiag):
            plain_diagonal(c)

        @pl.when((nfull & 1) == 1)
        def _():
            plain_chunk(0)

        def pair(pi, carry):
            j = (nfull & 1) + 2 * pi
            plain_chunk(j)
            plain_chunk(j + 1)
            return carry

        lax.fori_loop(0, nfull >> 1, pair, 0)
        acc_ref[DA_PAIR:DA_PAIR + 1] = jnp.sum(acc_ref[psum], axis=0, keepdims=True)

    def scores(j):
        return jnp.dot(keys(j), qx_ref[...], preferred_element_type=F32)

    def update(s, j):
        m_prev = m_ref[...]
        m_new = jnp.maximum(m_prev, jnp.max(s, axis=0, keepdims=True))
        p = jnp.exp2(s - m_new).astype(BF)
        acc_ref[...] = jnp.exp2(m_prev - m_new) * acc_ref[...] + jnp.dot(
            vt_ref[j], p, preferred_element_type=F32)
        m_ref[...] = m_new

    @pl.when(jnp.logical_not(bounded))
    def _():
        s = scores(nfull)
        s = jnp.where(diagonal_mask(s.shape, 0, 0), s, -jnp.inf)
        m0 = jnp.max(s, axis=0, keepdims=True)
        m_ref[...] = m0
        acc_ref[...] = jnp.dot(vt_ref[nfull], jnp.exp2(s - m0).astype(BF),
                               preferred_element_type=F32)
        for c in range(1, ndiag):
            s = scores(nfull + c)
            update(jnp.where(diagonal_mask(s.shape, c, 0), s, -jnp.inf), nfull + c)
        first = nfull & 1

        @pl.when(first == 1)
        def _():
            update(scores(0), 0)

        @pl.when(nfull >= 2)
        def _():
            sa_ref[...] = scores(first)

        def pair(pi, carry):
            j = first + 2 * pi
            sb_ref[...] = scores(j + 1)
            update(sa_ref[...], j)
            sa_ref[...] = scores(jnp.minimum(j + 2, nfull - 1))
            update(sb_ref[...], j + 1)
            return carry

        lax.fori_loop(0, nfull >> 1, pair, 0)

    on = acc_ref[:DA_PAIR] / acc_ref[DA_PAIR:DA_PAIR + 1]
    lam = _lambda_value(lamv_ref, lam_init)
    ot = on[:, :tq] - lam * on[:, tq:]
    ot = _rms(ot, 0) * g_ref[...] * (1.0 - lam_init)
    o_ref[...] = ot.T.astype(BF)


def _attn_prompt(qt, kb, vt, kn2, lamv, subln_g, dst, *, tq, lam_init):
    nb, nh, _, t = qt.shape
    nk, tk = vt.shape[2], vt.shape[4]
    nq = t // tq
    return pl.pallas_call(
        functools.partial(_attn_prompt_kernel, tq=tq, tk=tk, lam_init=lam_init),
        grid=(nb, nh, nq),
        in_specs=[
            pl.BlockSpec((None, None, DA_PAIR, tq), lambda b, h, i: (b, h, 0, i)),
            pl.BlockSpec((None, None, t, DA_PAIR), lambda b, h, i: (b, h, 0, 0)),
            pl.BlockSpec((None, None, nk, DA_VT_ROWS, tk), lambda b, h, i: (b, h, 0, 0, 0)),
            pl.BlockSpec((None, None, nk, 8, LANES), lambda b, h, i: (b, h, 0, 0, 0)),
            pl.BlockSpec((4, DA_HEAD_DIM), lambda b, h, i: (0, 0)),
            pl.BlockSpec((DA_PAIR, 1), lambda b, h, i: (0, 0)),
            ANY_SPEC,
        ],
        out_specs=pl.BlockSpec((tq, DA_PAIR), lambda b, h, i: (b * nq + i, h)),
        out_shape=jax.ShapeDtypeStruct(dst.shape, dst.dtype),
        scratch_shapes=[pltpu.VMEM((DA_PAIR, 2 * tq), BF),
                        pltpu.VMEM((tk, 2 * tq), F32), pltpu.VMEM((tk, 2 * tq), F32),
                        pltpu.VMEM((1, 2 * tq), F32), pltpu.VMEM((DA_VT_ROWS, 2 * tq), F32)],
        input_output_aliases={6: 0},
        compiler_params=_cparams("parallel", "parallel", "arbitrary"),
        name="attn_prompt",
    )(qt, kb, vt, kn2, lamv, subln_g.reshape(DA_PAIR, 1), dst)


def _rope_rows_kernel(zq_ref, zk_ref, cos_ref, sin_ref, q_ref, k_ref):
    cos, sin = cos_ref[...], sin_ref[...]
    scale = DA_HEAD_DIM ** -0.5
    for h in range(DA_HEADS):
        sl = slice(h * DA_PAIR, (h + 1) * DA_PAIR)
        q_ref[:, sl] = (_rope128(zq_ref[:, sl], cos, sin) * scale).astype(BF)
        k_ref[:, sl] = _rope128(zk_ref[:, sl], cos, sin)


def _rope_rows(z, cos, sin, *, row0, nrows):
    rb = row0 // nrows
    return pl.pallas_call(
        _rope_rows_kernel,
        grid=(1,),
        in_specs=[
            pl.BlockSpec((nrows, DA_WIDTH), lambda i: (rb, 0)),
            pl.BlockSpec((nrows, DA_WIDTH), lambda i: (rb, 1)),
            pl.BlockSpec((nrows, LANES), lambda i: (0, 0)),
            pl.BlockSpec((nrows, LANES), lambda i: (0, 0)),
        ],
        out_specs=[pl.BlockSpec((nrows, DA_WIDTH), lambda i: (0, 0)),
                   pl.BlockSpec((nrows, DA_WIDTH), lambda i: (0, 0))],
        out_shape=[jax.ShapeDtypeStruct((nrows, DA_WIDTH), BF),
                   jax.ShapeDtypeStruct((nrows, DA_WIDTH), F32)],
        compiler_params=_cparams("arbitrary"),
        name="rope_rows",
    )(z, z, cos, sin)


def _attn_sample_kernel(pt_ref, q_ref, kn_ref, vn_ref, sel_ref, lamv_ref, g_ref, *rest,
                        pp, nsteps, ts, lam_init):
    kpages, vpages = rest[:pp], rest[pp:2 * pp]
    _, o_ref, m_ref, l_ref, acc_ref, qbd_ref = rest[2 * pp:]
    s = pl.program_id(1)
    nrow = 2 * DA_HEADS * ts

    @pl.when(s == 0)
    def _():
        q = q_ref[...]
        qrep = jnp.concatenate([q] * (2 * DA_HEADS), axis=0)
        qbd_ref[...] = jnp.where(sel_ref[...] > 0, qrep, jnp.zeros_like(qrep))
        m_ref[...] = jnp.full(m_ref.shape, -jnp.inf, F32)
        l_ref[...] = jnp.zeros_like(l_ref)
        acc_ref[...] = jnp.zeros_like(acc_ref)

    def update(sc, vv, causal):
        if causal:
            tq = lax.broadcasted_iota(jnp.int32, sc.shape, 0) & (ts - 1)
            tk = lax.broadcasted_iota(jnp.int32, sc.shape, 1)
            sc = jnp.where(tk <= tq, sc, -jnp.inf)
        m_prev = m_ref[...]
        m_new = jnp.maximum(m_prev, jnp.max(sc, axis=1, keepdims=True))
        alpha = jnp.exp(m_prev - m_new)
        p = jnp.exp(sc - m_new)
        l_ref[...] = alpha * l_ref[...] + jnp.sum(p, axis=1, keepdims=True)
        acc_ref[...] = alpha * acc_ref[...] + jnp.dot(p.astype(BF), vv, preferred_element_type=F32)
        m_ref[...] = m_new

    if pp:
        kt = jnp.concatenate([r[...] for r in kpages], axis=1).astype(BF)
        vv = jnp.concatenate(
            [jnp.concatenate([r[h] for h in range(DA_HEADS)], axis=1) for r in vpages],
            axis=0).astype(BF)
        update(jnp.dot(qbd_ref[...], kt, preferred_element_type=F32), vv, False)

    @pl.when(s == nsteps - 1)
    def _():
        sc_new = lax.dot_general(qbd_ref[...], kn_ref[...].astype(BF), (((1,), (1,)), ((), ())),
                                 preferred_element_type=F32)
        update(sc_new, vn_ref[...].astype(BF), True)
        on = acc_ref[...] / l_ref[...]
        lam = _lambda_value(lamv_ref, lam_init)
        half = nrow // 2
        d = on[:half] - lam * on[half:]
        outs = []
        for h in range(DA_HEADS):
            blk = d[h * ts:(h + 1) * ts, h * DA_PAIR:(h + 1) * DA_PAIR]
            outs.append(_rms(blk, -1) * g_ref[...] * (1.0 - lam_init))
        o_ref[...] = jnp.concatenate(outs, axis=1).astype(BF)


def _attn_sample(q_s, k_new, z, cache_kt, cache_vh, page_table, lamv, subln_g, dst, *,
                 ts, layer, zrow0, row0, lam_init):
    nb = q_s.shape[0] // ts
    n_pages = page_table.shape[1]
    pp = 0
    for cand in DECODE_PAGES:
        if n_pages and n_pages % cand == 0:
            pp = cand
            break
    nsteps = max(n_pages // pp, 1) if pp else 1
    nrow = 2 * DA_HEADS * ts
    r = np.arange(nrow)[:, None] // ts
    c = np.arange(DA_WIDTH)[None, :] // DA_HEAD_DIM
    sel = jnp.asarray(((r % DA_HEADS) * 2 + r // DA_HEADS == c).astype(np.float32))
    rb, zb = row0 // ts, zrow0 // ts

    def kpage_spec(i):
        return pl.BlockSpec((None, None, DA_WIDTH, PAGE_SIZE),
                            lambda b, s, pt: (pt[b * n_pages + s * pp + i], layer, 0, 0))

    def vpage_spec(i):
        return pl.BlockSpec((None, None, DA_HEADS, PAGE_SIZE, DA_PAIR),
                            lambda b, s, pt: (pt[b * n_pages + s * pp + i], layer, 0, 0, 0))

    in_specs = [
        pl.BlockSpec((ts, DA_WIDTH), lambda b, s, pt: (b, 0)),
        pl.BlockSpec((ts, DA_WIDTH), lambda b, s, pt: (b, 0)),
        pl.BlockSpec((ts, DA_WIDTH), lambda b, s, pt: (zb + b, 2)),
        pl.BlockSpec((nrow, DA_WIDTH), lambda b, s, pt: (0, 0)),
        pl.BlockSpec((4, DA_HEAD_DIM), lambda b, s, pt: (0, 0)),
        pl.BlockSpec((1, DA_PAIR), lambda b, s, pt: (0, 0)),
    ] + [kpage_spec(i) for i in range(pp)] + [vpage_spec(i) for i in range(pp)] + [ANY_SPEC]
    grid_spec = pltpu.PrefetchScalarGridSpec(
        num_scalar_prefetch=1,
        grid=(nb, nsteps),
        in_specs=in_specs,
        out_specs=pl.BlockSpec((ts, DA_WIDTH), lambda b, s, pt: (rb + b, 0)),
        scratch_shapes=[pltpu.VMEM((nrow, 1), F32), pltpu.VMEM((nrow, 1), F32),
                        pltpu.VMEM((nrow, DA_WIDTH), F32), pltpu.VMEM((nrow, DA_WIDTH), BF)],
    )
    return pl.pallas_call(
        functools.partial(_attn_sample_kernel, pp=pp, nsteps=nsteps, ts=ts, lam_init=lam_init),
        grid_spec=grid_spec,
        out_shape=jax.ShapeDtypeStruct(dst.shape, dst.dtype),
        input_output_aliases={7 + 2 * pp: 0},
        compiler_params=_cparams("parallel", "arbitrary"),
        name="attn_sample",
    )(page_table.reshape(-1), q_s, k_new, z, sel, lamv, subln_g.reshape(1, DA_PAIR),
      *([cache_kt] * pp), *([cache_vh] * pp), dst)


def _sgu_norm_v(sv_ref, ng_ref, g):
    x = jax.nn.gelu(sv_ref[:, g * SGU_GROUP_DIM:(g + 1) * SGU_GROUP_DIM])
    return _rms(x, -1) * ng_ref[g:g + 1, :]


def _sgu_prompt_kernel(u_ref, sv_ref, ng_ref, w_ref, bs_ref, dst_ref, y_ref, vr_ref, *, tm, c):
    del dst_ref
    last = pl.program_id(1) == pl.num_programs(1) - 1
    ri = lax.broadcasted_iota(jnp.int32, (c, c), 0)
    ci = lax.broadcasted_iota(jnp.int32, (c, c), 1)
    for g in range(SGU_GROUPS):
        sl = slice(g * SGU_GROUP_DIM, (g + 1) * SGU_GROUP_DIM)
        v = _sgu_norm_v(sv_ref, ng_ref, g)

        @pl.when(last)
        def _():
            vr_ref[:, sl] = v[tm - c:]

        vb = v.astype(BF)
        w = jnp.where(ri >= ci, w_ref[g], 0.0).astype(BF)
        for n in range(tm // c):
            rows = slice(n * c, (n + 1) * c)
            mixed = jnp.dot(w, vb[rows], preferred_element_type=F32) + bs_ref[g]
            y_ref[rows, sl] = (jax.nn.gelu(u_ref[rows, sl]) * mixed).astype(BF)


def _sgu_prompt(z, norm_g, w_s, b_s, dst, *, nb, t):
    c = SGU_CHUNK
    tm = _pick_tile(t, MIXER_ROWS, c)
    nt = t // tm
    bs = jnp.broadcast_to(b_s[:, :c, None], (SGU_GROUPS, c, SGU_GROUP_DIM))
    return pl.pallas_call(
        functools.partial(_sgu_prompt_kernel, tm=tm, c=c),
        grid=(nb, nt),
        in_specs=[
            pl.BlockSpec((tm, SGU_WIDTH), lambda b, i: (b * nt + i, 0)),
            pl.BlockSpec((tm, SGU_WIDTH), lambda b, i: (b * nt + i, 1)),
            pl.BlockSpec((SGU_GROUPS, SGU_GROUP_DIM), lambda b, i: (0, 0)),
            pl.BlockSpec((SGU_GROUPS, c, c), lambda b, i: (0, 0, 0)),
            pl.BlockSpec((SGU_GROUPS, c, SGU_GROUP_DIM), lambda b, i: (0, 0, 0)),
            ANY_SPEC,
        ],
        out_specs=[
            pl.BlockSpec((tm, SGU_WIDTH), lambda b, i: (b * nt + i, 0)),
            pl.BlockSpec((None, c, SGU_WIDTH), lambda b, i: (b, 0, 0)),
        ],
        out_shape=[jax.ShapeDtypeStruct(dst.shape, dst.dtype),
                   jax.ShapeDtypeStruct((nb, c, SGU_WIDTH), F32)],
        input_output_aliases={5: 0},
        compiler_params=_cparams("parallel", "arbitrary"),
        name="sgu_prompt",
    )(z, z, norm_g, w_s[:, :c, :c], bs, dst)


def _sgu_sample_kernel(u_ref, sv_ref, ng_ref, w_ref, bs_ref, dst_ref, y_ref, vr_ref, *, nb, ts):
    del dst_ref
    v = jnp.concatenate([_sgu_norm_v(sv_ref, ng_ref, g) for g in range(SGU_GROUPS)], axis=1)
    vr_ref[...] = v
    v3 = v.reshape(nb, ts, SGU_WIDTH)
    ii = lax.broadcasted_iota(jnp.int32, (ts, SGU_WIDTH), 0)
    mixed = jnp.broadcast_to(bs_ref[...][None], (nb, ts, SGU_WIDTH))
    for j in range(ts):
        wj = jnp.where(ii >= j, w_ref[j], 0.0)
        mixed = mixed + wj[None] * v3[:, j:j + 1, :]
    y = jax.nn.gelu(u_ref[...]).reshape(nb, ts, SGU_WIDTH) * mixed
    y_ref[...] = y.reshape(nb * ts, SGU_WIDTH).astype(BF)


def _sgu_sample(z, norm_g, w_s, b_s, dst, *, nb, ts, row0):
    nrows = nb * ts
    rb = row0 // nrows
    w_exp = jnp.repeat(jnp.transpose(w_s[:, :ts, :ts], (2, 1, 0)), SGU_GROUP_DIM, axis=2)
    b_exp = jnp.repeat(jnp.transpose(b_s[:, :ts], (1, 0)), SGU_GROUP_DIM, axis=1)
    return pl.pallas_call(
        functools.partial(_sgu_sample_kernel, nb=nb, ts=ts),
        grid=(1,),
        in_specs=[
            pl.BlockSpec((nrows, SGU_WIDTH), lambda i: (rb, 0)),
            pl.BlockSpec((nrows, SGU_WIDTH), lambda i: (rb, 1)),
            pl.BlockSpec((SGU_GROUPS, SGU_GROUP_DIM), lambda i: (0, 0)),
            pl.BlockSpec((ts, ts, SGU_WIDTH), lambda i: (0, 0, 0)),
            pl.BlockSpec((ts, SGU_WIDTH), lambda i: (0, 0)),
            ANY_SPEC,
        ],
        out_specs=[pl.BlockSpec((nrows, SGU_WIDTH), lambda i: (rb, 0)),
                   pl.BlockSpec((nrows, SGU_WIDTH), lambda i: (0, 0))],
        out_shape=[jax.ShapeDtypeStruct(dst.shape, dst.dtype),
                   jax.ShapeDtypeStruct((nrows, SGU_WIDTH), F32)],
        input_output_aliases={5: 0},
        compiler_params=_cparams("arbitrary"),
        name="sgu_sample",
    )(z, z, norm_g, w_exp, b_exp, dst)


def _log_sigmoid(x):
    return jnp.minimum(x, 0.0) - jnp.log(1.0 + jnp.exp(-jnp.abs(x)))


def _seg_scan(x, seg, pos, reverse):
    n = x.shape[0]
    r = pos & (seg - 1)
    t = 1
    while t < seg:
        if reverse:
            x = x + jnp.where(r < seg - t, pltpu.roll(x, n - t, 0), 0.0)
        else:
            x = x + jnp.where(r >= t, pltpu.roll(x, t, 0), 0.0)
        t *= 2
    return x


def _gla_decays(gk, c):
    n, w = gk.shape
    pos = lax.broadcasted_iota(jnp.int32, gk.shape, 0)
    out = {1: (jnp.exp(gk), None)}
    s = 2
    while s <= c:
        if s <= 8 or n % s:
            pre = _seg_scan(gk, s, pos, False)
            suf = _seg_scan(gk, s, pos, True) - gk
        else:
            h = s // 2
            p4 = pre.reshape(n // s, 2, h, w)
            s4 = suf.reshape(n // s, 2, h, w)
            tot = p4[:, :, h - 1:h, :]
            pre = jnp.concatenate([p4[:, 0:1], p4[:, 1:2] + tot[:, 0:1]], axis=1).reshape(n, w)
            suf = jnp.concatenate([s4[:, 0:1] + tot[:, 1:2], s4[:, 1:2]], axis=1).reshape(n, w)
        out[s] = (jnp.exp(pre), jnp.exp(suf))
        s *= 2
    return out


def _gla_intra(q, k, v, dec, c, sc):
    n = q.shape[0]
    ri = lax.broadcasted_iota(jnp.int32, (sc, sc), 0)
    ci = lax.broadcasted_iota(jnp.int32, (sc, sc), 1)
    same_chunk = _shr(ri, c) == _shr(ci, c)
    lane = lax.broadcasted_iota(jnp.int32, (sc, LANES), 1)
    levels = []
    levels.append((ri == ci, q.astype(BF), k.astype(BF)))
    s = 1
    while s < c:
        pre, _ = dec[s]
        suf = dec[s][1]
        ks = k if suf is None else k * suf
        mask = same_chunk & (_shr(ri, 2 * s) == _shr(ci, 2 * s)) & ((_shr(ri, s) & 1) == 1) \
            & ((_shr(ci, s) & 1) == 0)
        levels.append((mask, (q * pre).astype(BF), ks.astype(BF)))
        s *= 2
    outs = []
    for h in range(GLA_HEADS):
        pair = slice((h // 2) * LANES, (h // 2 + 1) * LANES)
        own = _shr(lane, GLA_K_DIM) == (h % 2)
        vh = v[:, h * GLA_V_DIM:(h + 1) * GLA_V_DIM].astype(BF)
        rows_out = []
        for b0 in range(0, n, sc):
            rows = slice(b0, b0 + sc)
            a = jnp.zeros((sc, sc), F32)
            for mask, qs, ks in levels:
                qm = jnp.where(own, qs[rows, pair], jnp.zeros((sc, LANES), BF))
                al = lax.dot_general(qm, ks[rows, pair], (((1,), (1,)), ((), ())),
                                     preferred_element_type=F32)
                a = a + jnp.where(mask, al, 0.0)
            rows_out.append(jnp.dot(a.astype(BF), vh[rows], preferred_element_type=F32))
        outs.append(jnp.concatenate(rows_out, axis=0) if len(rows_out) > 1 else rows_out[0])
    return outs


def _gla_gate(glr_ref, w2_ref, gb_ref):
    pre = jnp.dot(glr_ref[...].astype(BF), w2_ref[...], preferred_element_type=F32) + gb_ref[...]
    return _log_sigmoid(pre) / GLA_GATE_NORMALIZER


def _gla_finish(o_heads, g_ref, gn_ref, o_ref):
    for h in range(GLA_HEADS):
        sl = slice(h * GLA_V_DIM, (h + 1) * GLA_V_DIM)
        gate = g_ref[:, sl]
        o_ref[:, sl] = (_rms(o_heads[h], -1) * gn_ref[...] * (gate * jax.nn.sigmoid(gate))).astype(BF)


def _gla_prompt_kernel(q_ref, k_ref, v_ref, g_ref, glr_ref, w2_ref, gb_ref, gn_ref, dst_ref,
                       o_ref, s_out_ref, st_ref, *, tm, c):
    del dst_ref
    i = pl.program_id(1)

    @pl.when(i == 0)
    def _():
        st_ref[...] = jnp.zeros_like(st_ref)

    gk = _gla_gate(glr_ref, w2_ref, gb_ref)
    q = q_ref[...] * (GLA_K_DIM ** -0.5)
    k = k_ref[...]
    v = v_ref[...]
    dec = _gla_decays(gk, c)
    o_heads = _gla_intra(q, k, v, dec, c, min(tm, LANES))
    eb, esuf = dec[c]
    qd = (q * eb).astype(BF)
    kd = (k * esuf).astype(BF)
    lane = lax.broadcasted_iota(jnp.int32, (c, LANES), 1)
    for h in range(GLA_HEADS):
        pair = slice((h // 2) * LANES, (h // 2 + 1) * LANES)
        own = _shr(lane, GLA_K_DIM) == (h % 2)
        vh = v[:, h * GLA_V_DIM:(h + 1) * GLA_V_DIM].astype(BF)
        st = st_ref[h]
        inter = []
        for n in range(tm // c):
            rows = slice(n * c, (n + 1) * c)
            qm = jnp.where(own, qd[rows, pair], jnp.zeros((c, LANES), BF))
            inter.append(lax.dot_general(qm, st.astype(BF), (((1,), (1,)), ((), ())),
                                         preferred_element_type=F32))
            upd = lax.dot_general(vh[rows], kd[rows, pair], (((0,), (0,)), ((), ())),
                                  preferred_element_type=F32)
            st = eb[n * c + c - 1:n * c + c, pair] * st + upd
        st_ref[h] = st
        o_heads[h] = o_heads[h] + jnp.concatenate(inter, axis=0)
    _gla_finish(o_heads, g_ref, gn_ref, o_ref)

    @pl.when(i == pl.num_programs(1) - 1)
    def _():
        for h in range(GLA_HEADS):
            lo = (h % 2) * GLA_K_DIM
            s_out_ref[h] = st_ref[h].T[lo:lo + GLA_K_DIM, :]


def _gla_prompt(z, w2, gate_b, gla_norm, dst, *, nb, t):
    c = GLA_CHUNK
    tm = _pick_tile(t, GLA_ROWS, LANES)
    nt = t // tm
    return pl.pallas_call(
        functools.partial(_gla_prompt_kernel, tm=tm, c=c),
        grid=(nb, nt),
        in_specs=[
            pl.BlockSpec((tm, GLA_K_WIDTH), lambda b, i: (b * nt + i, 4)),
            pl.BlockSpec((tm, GLA_K_WIDTH), lambda b, i: (b * nt + i, 5)),
            pl.BlockSpec((tm, GLA_V_WIDTH), lambda b, i: (b * nt + i, 3)),
            pl.BlockSpec((tm, GLA_V_WIDTH), lambda b, i: (b * nt + i, 4)),
            pl.BlockSpec((tm, GLA_RANK_PAD), lambda b, i: (b * nt + i, 20)),
            pl.BlockSpec((GLA_RANK_PAD, GLA_K_WIDTH), lambda b, i: (0, 0)),
            pl.BlockSpec((1, GLA_K_WIDTH), lambda b, i: (0, 0)),
            pl.BlockSpec((1, GLA_V_DIM), lambda b, i: (0, 0)),
            ANY_SPEC,
        ],
        out_specs=[
            pl.BlockSpec((tm, GLA_V_WIDTH), lambda b, i: (b * nt + i, 0)),
            pl.BlockSpec((None, GLA_HEADS, GLA_K_DIM, GLA_V_DIM), lambda b, i: (b, 0, 0, 0)),
        ],
        out_shape=[jax.ShapeDtypeStruct(dst.shape, dst.dtype),
                   jax.ShapeDtypeStruct((nb, GLA_HEADS, GLA_K_DIM, GLA_V_DIM), F32)],
        scratch_shapes=[pltpu.VMEM((GLA_HEADS, GLA_V_DIM, LANES), F32)],
        input_output_aliases={8: 0},
        compiler_params=_cparams("parallel", "arbitrary"),
        name="gla_prompt",
    )(z, z, z, z, z, w2, gate_b.reshape(1, GLA_K_WIDTH), gla_norm.reshape(1, GLA_V_DIM), dst)


def _gla_sample_kernel(q_ref, k_ref, v_ref, g_ref, glr_ref, w2_ref, gb_ref, gbc_ref, gn_ref, s0_ref,
                       dst_ref, o_ref, s_out_ref, *, ts):
    del dst_ref
    gk = _gla_gate(glr_ref, w2_ref, gb_ref)
    q = q_ref[...] * (GLA_K_DIM ** -0.5)
    k = k_ref[...]
    v = v_ref[...]
    dec = _gla_decays(gk, ts)
    o_heads = _gla_intra(q, k, v, dec, ts, ts)
    eb, esuf = dec[ts]
    qd = (q * eb).astype(BF)
    kd = (k * esuf).astype(BF)
    pre_t = lax.dot_general(w2_ref[...], glr_ref[...].astype(BF), (((0,), (1,)), ((), ())),
                            preferred_element_type=F32) + gbc_ref[...]
    dlast = jnp.exp(jnp.sum(_log_sigmoid(pre_t) / GLA_GATE_NORMALIZER, axis=1, keepdims=True))
    lane = lax.broadcasted_iota(jnp.int32, (ts, LANES), 1)
    for h in range(GLA_HEADS):
        pair = slice((h // 2) * LANES, (h // 2 + 1) * LANES)
        own = _shr(lane, GLA_K_DIM) == (h % 2)
        vh = v[:, h * GLA_V_DIM:(h + 1) * GLA_V_DIM].astype(BF)
        s_pair = s0_ref[pair, :]
        qm = jnp.where(own, qd[:, pair], jnp.zeros((ts, LANES), BF))
        o_heads[h] = o_heads[h] + jnp.dot(qm, s_pair.astype(BF), preferred_element_type=F32)
        upd = lax.dot_general(kd[:, pair], vh, (((0,), (0,)), ((), ())),
                              preferred_element_type=F32)
        hr = slice(h * GLA_K_DIM, (h + 1) * GLA_K_DIM)
        lo = (h % 2) * GLA_K_DIM
        s_out_ref[hr, :] = dlast[hr] * s0_ref[hr, :] + upd[lo:lo + GLA_K_DIM]
    _gla_finish(o_heads, g_ref, gn_ref, o_ref)


def _gla_sample(z, w2, gate_b, gla_norm, s0, dst, *, nb, ts, row0):
    rb = row0 // ts
    s0r = s0.reshape(nb, GLA_HEADS * GLA_K_DIM, GLA_V_DIM)
    o, s_new = pl.pallas_call(
        functools.partial(_gla_sample_kernel, ts=ts),
        grid=(nb,),
        in_specs=[
            pl.BlockSpec((ts, GLA_K_WIDTH), lambda b: (rb + b, 4)),
            pl.BlockSpec((ts, GLA_K_WIDTH), lambda b: (rb + b, 5)),
            pl.BlockSpec((ts, GLA_V_WIDTH), lambda b: (rb + b, 3)),
            pl.BlockSpec((ts, GLA_V_WIDTH), lambda b: (rb + b, 4)),
            pl.BlockSpec((ts, GLA_RANK_PAD), lambda b: (rb + b, 20)),
            pl.BlockSpec((GLA_RANK_PAD, GLA_K_WIDTH), lambda b: (0, 0)),
            pl.BlockSpec((1, GLA_K_WIDTH), lambda b: (0, 0)),
            pl.BlockSpec((GLA_K_WIDTH, 1), lambda b: (0, 0)),
            pl.BlockSpec((1, GLA_V_DIM), lambda b: (0, 0)),
            pl.BlockSpec((None, GLA_HEADS * GLA_K_DIM, GLA_V_DIM), lambda b: (b, 0, 0)),
            ANY_SPEC,
        ],
        out_specs=[
            pl.BlockSpec((ts, GLA_V_WIDTH), lambda b: (rb + b, 0)),
            pl.BlockSpec((None, GLA_HEADS * GLA_K_DIM, GLA_V_DIM), lambda b: (b, 0, 0)),
        ],
        out_shape=[jax.ShapeDtypeStruct(dst.shape, dst.dtype),
                   jax.ShapeDtypeStruct((nb, GLA_HEADS * GLA_K_DIM, GLA_V_DIM), F32)],
        input_output_aliases={10: 0},
        compiler_params=_cparams("parallel"),
        name="gla_sample",
    )(z, z, z, z, z, w2, gate_b.reshape(1, GLA_K_WIDTH), gate_b.reshape(GLA_K_WIDTH, 1),
      gla_norm.reshape(1, GLA_V_DIM), s0r, dst)
    return o, s_new.reshape(nb, GLA_HEADS, GLA_K_DIM, GLA_V_DIM)


def kernel(x_prompt, x_sample, cache_k, cache_v, page_table, state_pool, state_gla, ffn1_norm, ffn1_w_gate, ffn1_w_up, ffn1_w_down, mix_norm, ffn2_norm, ffn2_w_gate, ffn2_w_up, ffn2_w_down, even_w_in, even_w_out, pool_w, pool_scale, diff_lambda, diff_subln, odd_w_in, odd_w_out, sgu_norm, sgu_w, sgu_b, gla_gate_w2, gla_gate_b, gla_norm, final_norm):
    nb, t, d = x_prompt.shape
    nbs, ts, _ = x_sample.shape
    depth = ffn1_norm.shape[0]
    n_p, n_s = nb * t, nbs * ts
    past_len = page_table.shape[1] * PAGE_SIZE
    assert ts < min(SGU_CHUNK, GLA_CHUNK) and ts & (ts - 1) == 0 and n_p % n_s == 0
    assert t % SGU_CHUNK == 0 and t >= POOL_PAD

    n = n_p + n_s
    x = jnp.concatenate([x_prompt.reshape(n_p, d), x_sample.reshape(n_s, d)], axis=0)
    tk = _pick_tile(t, ATTN_KEY_CHUNK, LANES)
    tq = _pick_tile(t, ATTN_QUERY_CHUNKS * tk, tk)
    cos_p, sin_p = _rope_tables(jnp.arange(t))
    cos_s, sin_s = _rope_tables(past_len + jnp.arange(ts))
    cos_s, sin_s = jnp.tile(cos_s, (nbs, 1)), jnp.tile(sin_s, (nbs, 1))

    n_even = (depth + 1) // 2
    cache_kt = jnp.transpose(cache_k, (0, 1, 3, 4, 5, 2)).reshape(
        cache_k.shape[0], cache_k.shape[1], DA_WIDTH, PAGE_SIZE)
    cache_vh = jnp.transpose(cache_v, (0, 1, 3, 2, 4))
    kt_all = jnp.zeros((nb, n_even, DA_HEADS, DA_PAIR, t), F32)
    vh_all = jnp.zeros((nb, n_even, DA_HEADS, t, DA_PAIR), F32)
    pool_p, sgu_p, gla_p = [], [], []
    k_s, v_s, pool_s, sgu_s, gla_s = [], [], [], [], []
    for l in range(depth):
        i = l // 2
        x = _ffn(x, ffn1_norm[l], ffn1_w_gate, ffn1_w_up, ffn1_w_down, layer=l)
        if l % 2 == 0:
            lam_init = 0.8 - 0.6 * math.exp(-0.3 * l)
            w_in = even_w_in[i]
            w_in = jnp.concatenate([w_in[:, POOL_WIDTH:], w_in[:, :POOL_WIDTH]], axis=1).astype(BF)
            qt, kb, vt, kn2, zp, kt_all, vh_all = _inproj_prep(
                x, mix_norm[l], w_in, cos_p, sin_p, kt_all, vh_all, layer=i, nb=nb, t=t, tk=tk)
            z_s = _inproj(x, mix_norm[l], w_in, row0=n_p, nrows=n_s)
            pcol = 3 * DA_WIDTH // POOL_WIDTH
            w_bd = _block_diag(pool_w[i]).astype(BF)
            a1 = jnp.zeros((n, POOL_WIDTH), BF)
            a1 = _pool(zp, zp, w_bd, pool_scale[i], a1, nb=nb, t=t, zrow0=0, row0=0, col_blk=0,
                       prev_is_state=False, pos0=0)
            st_pad = jnp.pad(state_pool[i], ((0, 0), (POOL_PAD - POOL_HIST, 0), (0, 0)))
            a1 = _pool(z_s, st_pad.reshape(nbs * POOL_PAD, POOL_WIDTH), w_bd, pool_scale[i], a1,
                       nb=nbs, t=ts, zrow0=0, row0=n_p, col_blk=pcol, prev_is_state=True,
                       pos0=past_len)
            a2 = jnp.zeros((n, DA_WIDTH), BF)
            a2 = _attn_prompt(qt, kb, vt, kn2, diff_lambda[i], diff_subln[i], a2, tq=tq,
                              lam_init=lam_init)
            q_s, k_new = _rope_rows(z_s, cos_s, sin_s, row0=0, nrows=n_s)
            a2 = _attn_sample(q_s, k_new, z_s, cache_kt, cache_vh, page_table, diff_lambda[i],
                              diff_subln[i], a2, ts=ts, layer=i, zrow0=0, row0=n_p,
                              lam_init=lam_init)
            w_out = even_w_out[i].astype(BF)
            mix = (a1, a2, w_out[:POOL_WIDTH], w_out[POOL_WIDTH:])
            zs = z_s.reshape(nbs, ts, -1)
            pool_p.append(zp.reshape(nb, t, POOL_WIDTH)[:, t - POOL_HIST:])
            k_s.append(k_new.reshape(nbs, ts, DA_WIDTH))
            v_s.append(zs[:, :, 2 * DA_WIDTH:3 * DA_WIDTH])
            pool_s.append(jnp.concatenate([state_pool[i], zs[:, :, 3 * DA_WIDTH:]], axis=1)[:, -POOL_HIST:])
        else:
            w_in = jnp.pad(odd_w_in[i], ((0, 0), (0, GLA_RANK_PAD - GLA_GATE_RANK))).astype(BF)
            z = _inproj(x, mix_norm[l], w_in)
            w2 = jnp.pad(gla_gate_w2[i], ((0, GLA_RANK_PAD - GLA_GATE_RANK), (0, 0))).astype(BF)
            a1 = jnp.zeros((n, SGU_WIDTH), BF)
            a1, vr_p = _sgu_prompt(z, sgu_norm[i], sgu_w[i], sgu_b[i], a1, nb=nb, t=t)
            a1, vr_s = _sgu_sample(z, sgu_norm[i], sgu_w[i], sgu_b[i], a1, nb=nbs, ts=ts, row0=n_p)
            a2 = jnp.zeros((n, GLA_V_WIDTH), BF)
            a2, s_p = _gla_prompt(z, w2, gla_gate_b[i], gla_norm[i], a2, nb=nb, t=t)
            a2, s_s = _gla_sample(z, w2, gla_gate_b[i], gla_norm[i], state_gla[i], a2, nb=nbs, ts=ts,
                                  row0=n_p)
            w_out = odd_w_out[i].astype(BF)
            mix = (a1, a2, w_out[:SGU_WIDTH], w_out[SGU_WIDTH:])
            sgu_p.append(vr_p)
            gla_p.append(s_p)
            sgu_s.append(vr_s.reshape(nbs, ts, SGU_WIDTH))
            gla_s.append(s_s)
        x = _ffn(x, ffn2_norm[l], ffn2_w_gate, ffn2_w_up, ffn2_w_down, layer=l, mix=mix)
    y_p = _final_norm(x, final_norm, row0=0, nrows=n_p)
    y_s = _final_norm(x, final_norm, row0=n_p, nrows=n_s)
    k_rows_p = jnp.transpose(kt_all.reshape(nb, n_even, DA_HEADS, 2, DA_HEAD_DIM, t), (0, 1, 5, 2, 3, 4))
    v_rows_p = jnp.transpose(vh_all, (0, 1, 3, 2, 4))

    return (y_p.reshape(nb, t, d), y_s.reshape(nbs, ts, d), k_rows_p, v_rows_p,
            jnp.stack(pool_p, axis=0), jnp.stack(sgu_p, axis=0), jnp.stack(gla_p, axis=0),
            jnp.stack(k_s, axis=1).reshape(nbs, -1, ts, DA_HEADS, 2, DA_HEAD_DIM),
            jnp.stack(v_s, axis=1).reshape(nbs, -1, ts, DA_HEADS, DA_PAIR),
            jnp.stack(pool_s, axis=0), jnp.stack(sgu_s, axis=0), jnp.stack(gla_s, axis=0))
```
